```python
import jax
import jax.numpy as jnp
from jax import lax
import numpy as np

D_MODEL = 1024
BATCH = 8
SEQ = 2048
DEPTH = 2

CHUNK = 64
MEM_LEN = 256
RET_HEADS = 4
RET_HD = 128
RET_W = RET_HEADS * RET_HD
ROPE_BASE = 10000.0
RWKV_HEADS = 8
RWKV_HD = 64
RWKV_W = RWKV_HEADS * RWKV_HD
W_LORA = 64
A_LORA = 64
G_LORA = 128
RWKV_GN_EPS = 64e-5
XA_HEADS = 4
XA_HD = D_MODEL // XA_HEADS
N_EXPERTS = 16
N_GROUPS = 4
EXPERTS_PER_GROUP = N_EXPERTS // N_GROUPS
TOP_K = 2
D_EXPERT = 512
ALPHA = (2 * DEPTH) ** 0.25
BETA = (8 * DEPTH) ** -0.25
LN_EPS = 1e-5
RET_COLS = 4 * RET_W
RWKV_COLS = 3 * RWKV_W + W_LORA + A_LORA + G_LORA
GATE_COLS = 2 * D_MODEL
N_IN = RET_COLS + RWKV_COLS + GATE_COLS

kernel_name = 'hybrid_retention_rwkv7_memxattn_groupmoe_deepnorm'


def layer_norm(x, g, b, eps=LN_EPS):
    xf = x.astype(jnp.float32)
    mu = jnp.mean(xf, axis=-1, keepdims=True)
    var = jnp.mean(jnp.square(xf - mu), axis=-1, keepdims=True)
    y = (xf - mu) * lax.rsqrt(var + eps)
    return (y * g.astype(jnp.float32) + b.astype(jnp.float32)).astype(x.dtype)


def group_norm(y, eps):
    yf = y.astype(jnp.float32)
    mu = jnp.mean(yf, axis=-1, keepdims=True)
    var = jnp.mean(jnp.square(yf - mu), axis=-1, keepdims=True)
    yn = (yf - mu) * lax.rsqrt(var + eps)
    return yn.reshape(*y.shape[:-2], -1).astype(y.dtype)


def rotary(t, pos):
    half = t.shape[-1] // 2
    inv_freq = ROPE_BASE ** (-jnp.arange(half, dtype=jnp.float32) / half)
    ang = pos.astype(jnp.float32)[:, None] * inv_freq[None, :]
    cos = jnp.cos(ang)[None, :, None, :].astype(t.dtype)
    sin = jnp.sin(ang)[None, :, None, :].astype(t.dtype)
    t1, t2 = t[..., :half], t[..., half:]
    return jnp.concatenate([t1 * cos - t2 * sin, t1 * sin + t2 * cos], axis=-1)


def retention(q, k, v):
    B, S, H, hd = q.shape
    N = S // CHUNK
    dt = q.dtype
    log_gamma = jnp.log(1.0 - 2.0 ** (-5.0 - jnp.arange(H, dtype=jnp.float32)))
    c = jnp.arange(CHUNK, dtype=jnp.float32)
    inner_mask = jnp.exp(log_gamma[:, None, None] * jnp.abs(c[:, None] - c[None, :])).astype(dt)
    k_decay = jnp.exp(log_gamma[:, None] * (CHUNK - 1 - c)).astype(dt)
    q_decay = jnp.exp(log_gamma[:, None] * (c + 1.0)).astype(dt)
    n = jnp.arange(N, dtype=jnp.float32)
    dn = n[:, None] - 1.0 - n[None, :]
    chunk_decay = jnp.where(dn >= 0, jnp.exp(log_gamma[:, None, None] * CHUNK * jnp.maximum(dn, 0.0)), 0.0).astype(dt)
    to_chunks = lambda t: t.reshape(B, N, CHUNK, H, hd).transpose(0, 3, 1, 2, 4)
    qc, kc, vc = to_chunks(q), to_chunks(k * hd ** -0.5), to_chunks(v)
    scores = jnp.einsum('bhncd,bhnmd->bhncm', qc, kc) * inner_mask[:, None]
    inner = jnp.einsum('bhncm,bhnme->bhnce', scores, vc)
    kv = jnp.einsum('bhncd,bhnce->bhnde', kc * k_decay[:, None, :, None], vc)
    state_in = jnp.einsum('hni,bhide->bhnde', chunk_decay, kv)
    cross = jnp.einsum('bhncd,bhnde->bhnce', qc * q_decay[:, None, :, None], state_in)
    y = inner + cross
    return y.transpose(0, 2, 3, 1, 4).reshape(B, S, H, hd)


def token_shift(p, mu):
    prev = jnp.pad(p[:, :-1], ((0, 0), (1, 0), (0, 0)))
    return p + mu * (prev - p)


def rwkv7_time_mix(p, w_up, w0, a_up, a0, g_up, k_k, k_a, r_k, ln_g, ln_b):
    B, S, _ = p.shape
    dt = p.dtype
    r, k, v, dw, da, dg = jnp.split(
        p, [RWKV_W, 2 * RWKV_W, 3 * RWKV_W, 3 * RWKV_W + W_LORA, 3 * RWKV_W + W_LORA + A_LORA], axis=-1)
    w_log = -jax.nn.softplus(-(w0 + jnp.tanh(dw) @ w_up)) - 0.5
    decay = jnp.exp(-jnp.exp(w_log.astype(jnp.float32))).astype(dt)
    a = jax.nn.sigmoid(a0 + da @ a_up)
    g = jax.nn.sigmoid(dg) @ g_up
    heads = lambda t: t.reshape(B, S, RWKV_HEADS, RWKV_HD)
    kk = heads(k * k_k).astype(jnp.float32)
    kk = (kk / jnp.maximum(jnp.linalg.norm(kk, axis=-1, keepdims=True), 1e-12)).astype(dt)
    k = k * (1.0 + (a - 1.0) * k_a)
    r_h, k_h, v_h, a_h, w_h = heads(r), heads(k), heads(v), heads(a), heads(decay)

    def step(state, inp):
        r_t, w_t, k_t, v_t, kk_t, a_t = inp
        sa = jnp.einsum('bhvk,bhk->bhv', state, -kk_t)
        state = (state * w_t[:, :, None, :]
                 + sa[..., None] * (kk_t * a_t)[:, :, None, :]
                 + v_t[..., None] * k_t[:, :, None, :])
        return state, jnp.einsum('bhvk,bhk->bhv', state, r_t)

    time_major = lambda t: jnp.swapaxes(t, 0, 1).astype(dt)
    s0 = jnp.zeros((B, RWKV_HEADS, RWKV_HD, RWKV_HD), dt)
    _, y = lax.scan(step, s0, (time_major(r_h), time_major(w_h), time_major(k_h),
                               time_major(v_h), time_major(kk), time_major(a_h)))
    y = jnp.swapaxes(y, 0, 1)
    y = group_norm(y, RWKV_GN_EPS) * ln_g + ln_b
    bonus = jnp.sum(r_h * k_h * r_k, axis=-1, keepdims=True) * v_h
    return (y + bonus.reshape(B, S, RWKV_W)) * g


def hybrid_mixer(x, w_in, ret_gn_g, rwkv_mu, rwkv_w_up, rwkv_w0, rwkv_a_up, rwkv_a0, rwkv_g_up,
                 rwkv_k_k, rwkv_k_a, rwkv_r_k, rwkv_ln_g, rwkv_ln_b, w_ret_up, w_rwkv_up, w_out):
    B, S, _ = x.shape
    p = x @ w_in
    p_ret = p[..., :RET_COLS]
    p_rwkv = p[..., RET_COLS:RET_COLS + RWKV_COLS]
    p_gate = p[..., RET_COLS + RWKV_COLS:]
    q, k, v, g = jnp.split(p_ret, 4, axis=-1)
    heads = lambda t: t.reshape(B, S, RET_HEADS, RET_HD)
    pos = jnp.arange(S)
    y_ret = retention(rotary(heads(q), pos), rotary(heads(k), pos), heads(v))
    y_ret = jax.nn.silu(g) * (group_norm(y_ret, LN_EPS) * ret_gn_g)
    y_rwkv = rwkv7_time_mix(token_shift(p_rwkv, rwkv_mu), rwkv_w_up, rwkv_w0, rwkv_a_up, rwkv_a0,
                            rwkv_g_up, rwkv_k_k, rwkv_k_a, rwkv_r_k, rwkv_ln_g, rwkv_ln_b)
    gate_ret, gate_rwkv = jnp.split(jax.nn.sigmoid(p_gate), 2, axis=-1)
    merged = gate_ret * (y_ret @ w_ret_up) + gate_rwkv * (y_rwkv @ w_rwkv_up)
    return merged @ w_out


def memory_cross_attention(x, mem, wq, wkv, wo):
    B, S, _ = x.shape
    M = mem.shape[1]
    q = (x @ wq).reshape(B, S, XA_HEADS, XA_HD)
    k, v = jnp.split(mem @ wkv, 2, axis=-1)
    k = k.reshape(B, M, XA_HEADS, XA_HD)
    v = v.reshape(B, M, XA_HEADS, XA_HD)
    s = jnp.einsum('bshd,bmhd->bhsm', q, k).astype(jnp.float32) * XA_HD ** -0.5
    probs = jax.nn.softmax(s, axis=-1).astype(x.dtype)
    o = jnp.einsum('bhsm,bmhd->bshd', probs, v).reshape(B, S, D_MODEL)
    return o @ wo


def grouped_moe(x, router_w, router_bias, w_gate, w_up, w_down):
    B, S, D = x.shape
    t = x.reshape(-1, D)
    affinity = jax.nn.sigmoid((t @ router_w).astype(jnp.float32))
    choice = affinity + router_bias.astype(jnp.float32)
    group_score = jnp.sum(lax.top_k(choice.reshape(-1, N_GROUPS, EXPERTS_PER_GROUP), TOP_K)[0], axis=-1)
    best_group = jnp.argmax(group_score, axis=-1)
    in_group = (jnp.arange(N_EXPERTS) // EXPERTS_PER_GROUP)[None, :] == best_group[:, None]
    _, idx = lax.top_k(jnp.where(in_group, choice, -jnp.inf), TOP_K)
    wts = jnp.take_along_axis(affinity, idx, axis=-1)
    wts = wts / jnp.sum(wts, axis=-1, keepdims=True)
    combine = jnp.sum(jax.nn.one_hot(idx, N_EXPERTS, dtype=jnp.float32) * wts[..., None], axis=1).astype(x.dtype)
    y = jnp.zeros_like(t)
    for e in range(N_EXPERTS):
        h = jax.nn.silu(t @ w_gate[e]) * (t @ w_up[e])
        y = y + combine[:, e:e + 1] * (h @ w_down[e])
    return y.reshape(B, S, D)


def setup_inputs(seed: int = 0) -> dict:
    key = jax.random.key(seed)
    ks = iter(jax.random.split(key, 48))
    nrm = lambda shape, scale: scale * jax.random.normal(next(ks), shape, jnp.float32)
    L = DEPTH
    return {
        'x': nrm((BATCH, SEQ, D_MODEL), 1.0),
        'mem': nrm((BATCH, MEM_LEN, D_MODEL), 1.0),
        'ln_in_g': 1.0 + nrm((D_MODEL,), 0.02),
        'ln_in_b': nrm((D_MODEL,), 0.02),
        'router_w': nrm((D_MODEL, N_EXPERTS), D_MODEL ** -0.5),
        'router_bias': nrm((N_EXPERTS,), 0.01),
        'w_in': nrm((L, D_MODEL, N_IN), D_MODEL ** -0.5),
        'ret_gn_g': 1.0 + nrm((L, RET_W), 0.02),
        'rwkv_mu': jax.random.uniform(next(ks), (L, RWKV_COLS), jnp.float32),
        'rwkv_w_up': nrm((L, W_LORA, RWKV_W), 0.5 * W_LORA ** -0.5),
        'rwkv_w0': jnp.linspace(-6.0, -1.0, RWKV_W, dtype=jnp.float32)[None, :] + nrm((L, RWKV_W), 0.1),
        'rwkv_a_up': nrm((L, A_LORA, RWKV_W), 0.5 * A_LORA ** -0.5),
        'rwkv_a0': nrm((L, RWKV_W), 0.1),
        'rwkv_g_up': nrm((L, G_LORA, RWKV_W), G_LORA ** -0.5),
        'rwkv_k_k': 0.85 + nrm((L, RWKV_W), 0.05),
        'rwkv_k_a': 1.0 + nrm((L, RWKV_W), 0.05),
        'rwkv_r_k': nrm((L, RWKV_HEADS, RWKV_HD), 0.1),
        'rwkv_ln_g': 1.0 + nrm((L, RWKV_W), 0.02),
        'rwkv_ln_b': nrm((L, RWKV_W), 0.02),
        'w_ret_up': nrm((L, RET_W, D_MODEL), RET_W ** -0.5),
        'w_rwkv_up': nrm((L, RWKV_W, D_MODEL), RWKV_W ** -0.5),
        'w_out': nrm((L, D_MODEL, D_MODEL), BETA * D_MODEL ** -0.5),
        'ln1_g': 1.0 + nrm((L, D_MODEL), 0.02),
        'ln1_b': nrm((L, D_MODEL), 0.02),
        'xa_wq': nrm((L, D_MODEL, D_MODEL), D_MODEL ** -0.5),
        'xa_wkv': nrm((L, D_MODEL, 2 * D_MODEL), D_MODEL ** -0.5),
        'xa_wo': nrm((L, D_MODEL, D_MODEL), BETA * D_MODEL ** -0.5),
        'ln2_g': 1.0 + nrm((L, D_MODEL), 0.02),
        'ln2_b': nrm((L, D_MODEL), 0.02),
        'moe_w_gate': nrm((L, N_EXPERTS, D_MODEL, D_EXPERT), D_MODEL ** -0.5),
        'moe_w_up': nrm((L, N_EXPERTS, D_MODEL, D_EXPERT), D_MODEL ** -0.5),
        'moe_w_down': nrm((L, N_EXPERTS, D_EXPERT, D_MODEL), BETA * D_EXPERT ** -0.5),
        'ln3_g': 1.0 + nrm((L, D_MODEL), 0.02),
        'ln3_b': nrm((L, D_MODEL), 0.02),
    }


def reference(x, mem, ln_in_g, ln_in_b, router_w, router_bias, w_in, ret_gn_g, rwkv_mu, rwkv_w_up,
              rwkv_w0, rwkv_a_up, rwkv_a0, rwkv_g_up, rwkv_k_k, rwkv_k_a, rwkv_r_k, rwkv_ln_g, rwkv_ln_b,
              w_ret_up, w_rwkv_up, w_out, ln1_g, ln1_b, xa_wq, xa_wkv, xa_wo, ln2_g, ln2_b,
              moe_w_gate, moe_w_up, moe_w_down, ln3_g, ln3_b):
    x = layer_norm(x, ln_in_g, ln_in_b)
    for l in range(DEPTH):
        h = hybrid_mixer(x, w_in[l], ret_gn_g[l], rwkv_mu[l], rwkv_w_up[l], rwkv_w0[l], rwkv_a_up[l],
                         rwkv_a0[l], rwkv_g_up[l], rwkv_k_k[l], rwkv_k_a[l], rwkv_r_k[l], rwkv_ln_g[l],
                         rwkv_ln_b[l], w_ret_up[l], w_rwkv_up[l], w_out[l])
        x = layer_norm(ALPHA * x + h, ln1_g[l], ln1_b[l])
        h = memory_cross_attention(x, mem, xa_wq[l], xa_wkv[l], xa_wo[l])
        x = layer_norm(ALPHA * x + h, ln2_g[l], ln2_b[l])
        h = grouped_moe(x, router_w, router_bias, moe_w_gate[l], moe_w_up[l], moe_w_down[l])
        x = layer_norm(ALPHA * x + h, ln3_g[l], ln3_b[l])
    return x
```

```python
import functools
import math

import jax
import jax.numpy as jnp
from jax import lax
from jax.experimental import pallas as pl
from jax.experimental.pallas import tpu as pltpu

F32 = jnp.float32
BF16 = jnp.bfloat16

LANES = 128
VMEM_LIMIT = 56 * 1024 * 1024

CHUNK = 64
RET_HEADS = 4
RET_HD = 128
ROPE_BASE = 10000.0
RWKV_HD = 64
HD_SHIFT = 6
RWKV_GN_EPS = 64e-5
XA_HEADS = 4
N_GROUPS = 4
TOP_K = 2
LN_EPS = 1e-5
HEADS_PER_GROUP = 4
GROUP_W = HEADS_PER_GROUP * RWKV_HD


def _params(*sem):
    return pltpu.CompilerParams(dimension_semantics=sem, vmem_limit_bytes=VMEM_LIMIT)


def _ln(v, g, b, eps=LN_EPS):
    mu = jnp.mean(v, axis=-1, keepdims=True)
    d = v - mu
    var = jnp.mean(d * d, axis=-1, keepdims=True)
    return d * lax.rsqrt(var + eps) * g + b


def _dot(a, b):
    return jnp.dot(a.astype(BF16), b.astype(BF16), preferred_element_type=F32)


def _dot_nt(a, b):
    return lax.dot_general(a.astype(BF16), b.astype(BF16), (((1,), (1,)), ((), ())),
                           preferred_element_type=F32)


def _dot_tn(a, b):
    return lax.dot_general(a.astype(BF16), b.astype(BF16), (((0,), (0,)), ((), ())),
                           preferred_element_type=F32)


def _ln_kernel(x_ref, g_ref, b_ref, o_ref):
    o_ref[...] = _ln(x_ref[...], g_ref[...], b_ref[...])


def _layer_norm(x, g, b, tm=512):
    T, D = x.shape
    return pl.pallas_call(
        _ln_kernel, grid=(T // tm,),
        in_specs=[pl.BlockSpec((tm, D), lambda i: (i, 0)),
                  pl.BlockSpec((1, D), lambda i: (0, 0)),
                  pl.BlockSpec((1, D), lambda i: (0, 0))],
        out_specs=pl.BlockSpec((tm, D), lambda i: (i, 0)),
        out_shape=jax.ShapeDtypeStruct((T, D), F32),
        compiler_params=_params("parallel"), name="ln_in",
    )(x, g.reshape(1, D), b.reshape(1, D))


def _mm_kernel(a_ref, w_ref, o_ref, abf_ref):
    @pl.when(pl.program_id(1) == 0)
    def _():
        abf_ref[...] = a_ref[...].astype(BF16)

    o_ref[...] = jnp.dot(abf_ref[...], w_ref[...], preferred_element_type=F32).astype(o_ref.dtype)


def _matmul(a, w, tm, tn, out_dtype, name):
    M, K = a.shape
    N = w.shape[1]
    return pl.pallas_call(
        _mm_kernel, grid=(M // tm, N // tn),
        in_specs=[pl.BlockSpec((tm, K), lambda i, j: (i, 0)),
                  pl.BlockSpec((K, tn), lambda i, j: (0, j))],
        out_specs=pl.BlockSpec((tm, tn), lambda i, j: (i, j)),
        out_shape=jax.ShapeDtypeStruct((M, N), out_dtype),
        scratch_shapes=[pltpu.VMEM((tm, K), BF16)],
        compiler_params=_params("parallel", "arbitrary"), name=name,
    )(a, w)


def _ret_kernel(q_ref, k_ref, v_ref, g_ref, cos_ref, sin_ref, gn_ref, o_ref, state_ref, *, n_chunks):
    @pl.when(pl.program_id(1) == 0)
    def _():
        state_ref[...] = jnp.zeros_like(state_ref)

    C, hd = CHUNK, RET_HD
    ti = lax.broadcasted_iota(jnp.int32, (C, C), 0)
    si = lax.broadcasted_iota(jnp.int32, (C, C), 1)
    dist = jnp.abs(ti - si).astype(F32)
    cpos = lax.broadcasted_iota(jnp.int32, (C, 1), 0).astype(F32)
    for h in range(RET_HEADS):
        log_gamma = math.log(1.0 - 2.0 ** (-5.0 - h))
        inner_mask = jnp.exp(log_gamma * dist)
        k_decay = jnp.exp(log_gamma * (C - 1.0 - cpos))
        q_decay = jnp.exp(log_gamma * (cpos + 1.0))
        chunk_decay = math.exp(log_gamma * C)
        hs = slice(h * hd, (h + 1) * hd)
        for c in range(n_chunks):
            rs = slice(c * C, (c + 1) * C)
            cos, sin = cos_ref[rs, :], sin_ref[rs, :]
            q = q_ref[0, rs, hs]
            k = k_ref[0, rs, hs]
            v = v_ref[0, rs, hs]
            qr = q * cos + pltpu.roll(q, hd // 2, axis=1) * sin
            kr = (k * cos + pltpu.roll(k, hd // 2, axis=1) * sin) * (hd ** -0.5)
            state = state_ref[h]
            scores = _dot_nt(qr, kr) * inner_mask
            y = _dot(scores, v) + _dot(qr * q_decay, state)
            state_ref[h] = chunk_decay * state + _dot_tn(kr * k_decay, v)
            mu = jnp.mean(y, axis=-1, keepdims=True)
            d = y - mu
            var = jnp.mean(d * d, axis=-1, keepdims=True)
            yn = d * lax.rsqrt(var + LN_EPS) * gn_ref[:, hs]
            g = g_ref[0, rs, hs]
            o_ref[0, rs, hs] = (g * jax.nn.sigmoid(g) * yn).astype(o_ref.dtype)


def _retention(p_ret, cos2, sin2, gn_g, ts=128):
    B, S, W4 = p_ret.shape
    W = W4 // 4
    blk = lambda c: pl.BlockSpec((1, ts, W), lambda b, n, c=c: (b, n, c))
    return pl.pallas_call(
        functools.partial(_ret_kernel, n_chunks=ts // CHUNK), grid=(B, S // ts),
        in_specs=[blk(0), blk(1), blk(2), blk(3),
                  pl.BlockSpec((ts, RET_HD), lambda b, n: (n, 0)),
                  pl.BlockSpec((ts, RET_HD), lambda b, n: (n, 0)),
                  pl.BlockSpec((1, W), lambda b, n: (0, 0))],
        out_specs=pl.BlockSpec((1, ts, W), lambda b, n: (b, n, 0)),
        out_shape=jax.ShapeDtypeStruct((B, S, W), BF16),
        scratch_shapes=[pltpu.VMEM((RET_HEADS, RET_HD, RET_HD), F32)],
        compiler_params=_params("parallel", "arbitrary"), name="retention",
    )(p_ret, p_ret, p_ret, p_ret, cos2, sin2, gn_g.reshape(1, W))


def _head_sum(x):
    R, W = x.shape
    lo = lax.broadcasted_iota(jnp.int32, (R, LANES), 1) < RWKV_HD
    outs = []
    for j in range(W // LANES):
        xs = x[:, j * LANES:(j + 1) * LANES]
        s_lo = jnp.sum(jnp.where(lo, xs, 0.0), axis=-1, keepdims=True)
        s_hi = jnp.sum(jnp.where(lo, 0.0, xs), axis=-1, keepdims=True)
        outs.append(jnp.where(lo, s_lo, s_hi))
    return jnp.concatenate(outs, axis=1)


def _block_diag(x, head_of_lane):
    parts = [jnp.where(head_of_lane == h, x, 0.0) for h in range(HEADS_PER_GROUP)]
    return jnp.concatenate(parts, axis=0).astype(BF16)


def _unit_lower_inverse(a, t_idx, s_idx, head_of_lane):
    mm = lambda x, y: jnp.dot(x.astype(BF16), _block_diag(y, head_of_lane), preferred_element_type=F32)
    lower = s_idx < t_idx
    same4 = (t_idx >> 2) == (s_idx >> 2)
    same16 = (t_idx >> 4) == (s_idx >> 4)
    a1 = jnp.where(lower & same4, a, 0.0)
    a2 = jnp.where(lower & same16 & jnp.logical_not(same4), a, 0.0)
    a3 = jnp.where(lower & jnp.logical_not(same16), a, 0.0)
    eye = jnp.where(t_idx == s_idx, 1.0, 0.0)
    a1sq = mm(a1, a1)
    inv = eye + a1 + a1sq + mm(a1, a1sq)
    for a_lvl in (a2, a3):
        n = mm(inv, a_lvl)
        x = inv + mm(mm(n, n), inv)
        inv = x + mm(n, x)
    return inv


def _rwkv_kernel(pr_ref, pk_ref, pv_ref, pl_ref, mur_ref, muk_ref, muv_ref, mul_ref,
                 wup_ref, aup_ref, gup_ref, w0_ref, a0_ref, kk_ref, ka_ref, rk_ref, lng_ref, lnb_ref,
                 o_ref, carry_r, carry_k, carry_v, carry_l, state_ref):
    C = CHUNK
    first = pl.program_id(1) == 0

    @pl.when(first)
    def _():
        state_ref[...] = jnp.zeros_like(state_ref)
        carry_r[...] = jnp.zeros_like(carry_r)
        carry_k[...] = jnp.zeros_like(carry_k)
        carry_v[...] = jnp.zeros_like(carry_v)
        carry_l[...] = jnp.zeros_like(carry_l)

    def shifted(p_ref, mu_ref, carry):
        p = p_ref[0]
        row = lax.broadcasted_iota(jnp.int32, p.shape, 0)
        prev = jnp.where(row == 0, carry[0:1, :], pltpu.roll(p, 1, axis=0))
        carry[0:1, :] = p[C - 1:C, :]
        return p + mu_ref[...] * (prev - p)

    r = shifted(pr_ref, mur_ref, carry_r)
    k = shifted(pk_ref, muk_ref, carry_k)
    v = shifted(pv_ref, muv_ref, carry_v)
    lo = shifted(pl_ref, mul_ref, carry_l)
    dwa = lo[:, :LANES]
    dg = lo[:, LANES:]

    z = w0_ref[...] + _dot(jnp.tanh(dwa), wup_ref[...])
    logw = -math.exp(-0.5) * jax.nn.sigmoid(z)
    a = jax.nn.sigmoid(a0_ref[...] + _dot(dwa, aup_ref[...]))
    g = _dot(jax.nn.sigmoid(dg), gup_ref[...])

    kk = k * kk_ref[...]
    kk = kk / jnp.maximum(jnp.sqrt(_head_sum(kk * kk)), 1e-12)
    k = k * (1.0 + (a - 1.0) * ka_ref[...])

    ti = lax.broadcasted_iota(jnp.int32, (C, C), 0)
    si = lax.broadcasted_iota(jnp.int32, (C, C), 1)
    tril = jnp.where(si <= ti, 1.0, 0.0).astype(BF16)
    lw_hi = logw.astype(BF16)
    lw_lo = (logw - lw_hi.astype(F32)).astype(BF16)
    cum = (jnp.dot(tril, lw_hi, preferred_element_type=F32)
           + jnp.dot(tril, lw_lo, preferred_element_type=F32))
    cum_last = cum[C - 1:C, :]
    e_neg = jnp.exp(-cum)
    abar = -kk * jnp.exp(cum - logw)
    rbar = r * jnp.exp(cum)
    kka = kk * a
    btil = kka * e_neg
    ktil = k * e_neg
    e_end = jnp.exp(cum_last - cum)
    bend = kka * e_end
    kend = k * e_end
    gamma_end = jnp.exp(cum_last)

    t_idx = lax.broadcasted_iota(jnp.int32, (C, GROUP_W), 0)
    lane = lax.broadcasted_iota(jnp.int32, (C, GROUP_W), 1)
    s_idx = lane & (RWKV_HD - 1)
    head_of_lane = lane >> HD_SHIFT
    strict = s_idx < t_idx
    incl = s_idx <= t_idx
    bd_r = lax.broadcasted_iota(jnp.int32, (GROUP_W, GROUP_W), 0) >> HD_SHIFT
    bd_c = lax.broadcasted_iota(jnp.int32, (GROUP_W, GROUP_W), 1) >> HD_SHIFT
    on_diag = bd_r == bd_c

    ys = []
    for gi in range(r.shape[1] // GROUP_W):
        gs = slice(gi * GROUP_W, (gi + 1) * GROUP_W)
        v_g = v[:, gs]
        bd_v = _block_diag(v_g, head_of_lane)
        lhs = jnp.concatenate([abar[:, gs], rbar[:, gs]], axis=0)
        rhs = jnp.concatenate([_block_diag(btil[:, gs], head_of_lane),
                               _block_diag(ktil[:, gs], head_of_lane)], axis=0)
        gram = _dot_nt(lhs, rhs)
        a_ab = jnp.where(strict, gram[:C, :GROUP_W], 0.0)
        a_ak = jnp.where(strict, gram[:C, GROUP_W:], 0.0)
        m_rb = jnp.where(incl, gram[C:, :GROUP_W], 0.0)
        m_rk = jnp.where(incl, gram[C:, GROUP_W:], 0.0)
        inv = _unit_lower_inverse(a_ab, t_idx, s_idx, head_of_lane)
        st = state_ref[gi]
        st_bf = st.astype(BF16)
        from_state = _dot_nt(lhs, st_bf)
        u = jnp.dot(inv.astype(BF16),
                    _block_diag(from_state[:C] + jnp.dot(a_ak.astype(BF16), bd_v, preferred_element_type=F32),
                                head_of_lane),
                    preferred_element_type=F32)
        y = (from_state[C:]
             + jnp.dot(m_rb.astype(BF16), _block_diag(u, head_of_lane), preferred_element_type=F32)
             + jnp.dot(m_rk.astype(BF16), bd_v, preferred_element_type=F32))
        upd = _dot_tn(jnp.concatenate([u, v_g], axis=0),
                      jnp.concatenate([bend[:, gs], kend[:, gs]], axis=0))
        state_ref[gi] = st * gamma_end[:, gs] + jnp.where(on_diag, upd, 0.0)
        ys.append(y)
    y = jnp.concatenate(ys, axis=1)

    inv_hd = 1.0 / RWKV_HD
    mu = _head_sum(y) * inv_hd
    d = y - mu
    var = _head_sum(d * d) * inv_hd
    yn = d * lax.rsqrt(var + RWKV_GN_EPS) * lng_ref[...] + lnb_ref[...]
    bonus = _head_sum(r * k * rk_ref[...]) * v
    o_ref[0] = ((yn + bonus) * g).astype(o_ref.dtype)


def _rwkv(p_rwkv, mu, w_up, w0, a_up, a0, g_up, k_k, k_a, r_k, ln_g, ln_b):
    B, S, _ = p_rwkv.shape
    W = w0.shape[-1]
    C = CHUNK
    wup_pad = jnp.concatenate([w_up, jnp.zeros_like(a_up)], axis=0).astype(BF16)
    aup_pad = jnp.concatenate([jnp.zeros_like(w_up), a_up], axis=0).astype(BF16)
    row = lambda t: t.reshape(1, -1)
    vec = lambda n: pl.BlockSpec((1, n), lambda b, t: (0, 0))
    mat = lambda a: pl.BlockSpec(a.shape, lambda b, t: (0, 0))
    cblk = lambda c: pl.BlockSpec((1, C, W), lambda b, t, c=c: (b, t, c))
    lw = p_rwkv.shape[-1] - 3 * W
    n_groups = W // GROUP_W
    gup = g_up.astype(BF16)
    return pl.pallas_call(
        _rwkv_kernel, grid=(B, S // C),
        in_specs=[cblk(0), cblk(1), cblk(2),
                  pl.BlockSpec((1, C, lw), lambda b, t: (b, t, 3 * W // lw)),
                  vec(W), vec(W), vec(W), vec(lw),
                  mat(wup_pad), mat(aup_pad), mat(gup),
                  vec(W), vec(W), vec(W), vec(W), vec(W), vec(W), vec(W)],
        out_specs=pl.BlockSpec((1, C, W), lambda b, t: (b, t, 0)),
        out_shape=jax.ShapeDtypeStruct((B, S, W), BF16),
        scratch_shapes=[pltpu.VMEM((8, W), F32), pltpu.VMEM((8, W), F32), pltpu.VMEM((8, W), F32),
                        pltpu.VMEM((8, lw), F32),
                        pltpu.VMEM((n_groups, GROUP_W, GROUP_W), F32)],
        compiler_params=_params("parallel", "arbitrary"), name="rwkv7",
    )(p_rwkv, p_rwkv, p_rwkv, p_rwkv,
      row(mu[:W]), row(mu[W:2 * W]), row(mu[2 * W:3 * W]), row(mu[3 * W:]),
      wup_pad, aup_pad, gup,
      row(w0), row(a0), row(k_k), row(k_a), row(r_k), row(ln_g), row(ln_b))


def _merge_kernel(yr_ref, yw_ref, pg_ref, x_ref, wr_ref, ww_ref, wo_ref, g_ref, b_ref, o_ref, *, alpha):
    D = x_ref.shape[-1]
    gate = jax.nn.sigmoid(pg_ref[...])
    merged = (gate[:, :D] * jnp.dot(yr_ref[...], wr_ref[...], preferred_element_type=F32)
              + gate[:, D:] * jnp.dot(yw_ref[...], ww_ref[...], preferred_element_type=F32))
    h = jnp.dot(merged.astype(BF16), wo_ref[...], preferred_element_type=F32)
    o_ref[...] = _ln(alpha * x_ref[...] + h, g_ref[...], b_ref[...])


def _merge(y_ret, y_rwkv, p_gate, x, w_ret_up, w_rwkv_up, w_out, ln_g, ln_b, alpha, tm=512):
    T, D = x.shape
    W = y_ret.shape[1]
    full = lambda a: pl.BlockSpec(a.shape, lambda i: (0, 0))
    rowblk = lambda n: pl.BlockSpec((tm, n), lambda i: (i, 0))
    return pl.pallas_call(
        functools.partial(_merge_kernel, alpha=alpha), grid=(T // tm,),
        in_specs=[rowblk(W), rowblk(W), rowblk(2 * D), rowblk(D),
                  full(w_ret_up), full(w_rwkv_up), full(w_out),
                  pl.BlockSpec((1, D), lambda i: (0, 0)), pl.BlockSpec((1, D), lambda i: (0, 0))],
        out_specs=rowblk(D),
        out_shape=jax.ShapeDtypeStruct((T, D), F32),
        compiler_params=_params("parallel"), name="merge_out_ln",
    )(y_ret, y_rwkv, p_gate, x, w_ret_up, w_rwkv_up, w_out, ln_g.reshape(1, D), ln_b.reshape(1, D))


def _xattn_kernel(x_ref, kv_ref, wq_ref, wo_ref, g_ref, b_ref, o_ref, *, alpha):
    x = x_ref[0]
    D = x.shape[-1]
    hd = D // XA_HEADS
    q = jnp.dot(x.astype(BF16), wq_ref[...], preferred_element_type=F32)
    outs = []
    for h in range(XA_HEADS):
        k_h = kv_ref[0, :, h * hd:(h + 1) * hd]
        v_h = kv_ref[0, :, D + h * hd:D + (h + 1) * hd]
        s = _dot_nt(q[:, h * hd:(h + 1) * hd], k_h) * (hd ** -0.5)
        s = s - jnp.max(s, axis=-1, keepdims=True)
        e = jnp.exp(s)
        probs = e / jnp.sum(e, axis=-1, keepdims=True)
        outs.append(jnp.dot(probs.astype(BF16), v_h, preferred_element_type=F32))
    o = jnp.concatenate(outs, axis=1)
    h_out = jnp.dot(o.astype(BF16), wo_ref[...], preferred_element_type=F32)
    o_ref[0] = _ln(alpha * x + h_out, g_ref[...], b_ref[...])


def _cross_attention(x, kv, wq, wo, ln_g, ln_b, alpha, tm=512):
    B, S, D = x.shape
    M = kv.shape[1]
    full = lambda a: pl.BlockSpec(a.shape, lambda b, i: (0, 0))
    return pl.pallas_call(
        functools.partial(_xattn_kernel, alpha=alpha), grid=(B, S // tm),
        in_specs=[pl.BlockSpec((1, tm, D), lambda b, i: (b, i, 0)),
                  pl.BlockSpec((1, M, 2 * D), lambda b, i: (b, 0, 0)),
                  full(wq), full(wo),
                  pl.BlockSpec((1, D), lambda b, i: (0, 0)), pl.BlockSpec((1, D), lambda b, i: (0, 0))],
        out_specs=pl.BlockSpec((1, tm, D), lambda b, i: (b, i, 0)),
        out_shape=jax.ShapeDtypeStruct((B, S, D), F32),
        compiler_params=_params("parallel", "parallel"), name="xattn_ln",
    )(x, kv, wq, wo, ln_g.reshape(1, D), ln_b.reshape(1, D))


def _router_kernel(x_ref, wt_ref, bias_ref, o_ref):
    E = wt_ref.shape[0]
    per_group = E // N_GROUPS
    logits = lax.dot_general(wt_ref[...], x_ref[...], (((1,), (1,)), ((), ())),
                             precision=lax.Precision.HIGHEST, preferred_element_type=F32)
    aff = jax.nn.sigmoid(logits)
    choice = aff + bias_ref[...]
    rows = [choice[e:e + 1, :] for e in range(E)]
    scores = []
    for gidx in range(N_GROUPS):
        members = rows[gidx * per_group:(gidx + 1) * per_group]
        best = None
        for i in range(per_group):
            for j in range(i + 1, per_group):
                pair = members[i] + members[j]
                best = pair if best is None else jnp.maximum(best, pair)
        scores.append(best)
    top = scores[0]
    for s in scores[1:]:
        top = jnp.maximum(top, s)
    taken = jnp.zeros_like(top, dtype=jnp.bool_)
    in_best = []
    for s in scores:
        hit = jnp.logical_and(s == top, jnp.logical_not(taken))
        in_best.append(hit)
        taken = jnp.logical_or(taken, hit)
    sel_w = []
    for gidx in range(N_GROUPS):
        members = rows[gidx * per_group:(gidx + 1) * per_group]
        for i in range(per_group):
            rank = jnp.zeros_like(top)
            for j in range(per_group):
                if j == i:
                    continue
                ahead = (members[j] > members[i]) if j > i else (members[j] >= members[i])
                rank = rank + jnp.where(ahead, 1.0, 0.0)
            chosen = jnp.logical_and(in_best[gidx], rank < TOP_K)
            e = gidx * per_group + i
            sel_w.append(jnp.where(chosen, aff[e:e + 1, :], 0.0))
    total = sel_w[0]
    for w in sel_w[1:]:
        total = total + w
    o_ref[...] = jnp.concatenate(sel_w, axis=0) / total


def _router(x, router_w, router_bias, tm=1024):
    T, D = x.shape
    E = router_w.shape[1]
    return pl.pallas_call(
        _router_kernel, grid=(T // tm,),
        in_specs=[pl.BlockSpec((tm, D), lambda i: (i, 0)),
                  pl.BlockSpec((E, D), lambda i: (0, 0)),
                  pl.BlockSpec((E, 1), lambda i: (0, 0))],
        out_specs=pl.BlockSpec((E, tm), lambda i: (0, i)),
        out_shape=jax.ShapeDtypeStruct((E, T), F32),
        compiler_params=_params("parallel"), name="moe_router",
    )(x, router_w.T, router_bias.reshape(E, 1))


def _moe_kernel(x_ref, comb_ref, wg_ref, wu_ref, wd_ref, g_ref, b_ref, o_ref, xbf_ref, acc_ref, *, alpha):
    e = pl.program_id(1)

    @pl.when(e == 0)
    def _():
        xbf_ref[...] = x_ref[...].astype(BF16)
        acc_ref[...] = jnp.zeros_like(acc_ref)

    xb = xbf_ref[...]
    hg = jnp.dot(xb, wg_ref[0], preferred_element_type=F32)
    hu = jnp.dot(xb, wu_ref[0], preferred_element_type=F32)
    h = hg * jax.nn.sigmoid(hg) * hu
    comb = comb_ref[...]
    lane = lax.broadcasted_iota(jnp.int32, comb.shape, 1)
    w_e = jnp.sum(jnp.where(lane == e, comb, 0.0), axis=-1, keepdims=True)
    acc_ref[...] += w_e * jnp.dot(h.astype(BF16), wd_ref[0], preferred_element_type=F32)

    @pl.when(e == pl.num_programs(1) - 1)
    def _():
        o_ref[...] = _ln(alpha * x_ref[...] + acc_ref[...], g_ref[...], b_ref[...])


def _moe(x, comb, w_gate, w_up, w_down, ln_g, ln_b, alpha, tm=1024):
    T, D = x.shape
    E, _, F = w_gate.shape
    return pl.pallas_call(
        functools.partial(_moe_kernel, alpha=alpha), grid=(T // tm, E),
        in_specs=[pl.BlockSpec((tm, D), lambda i, e: (i, 0)),
                  pl.BlockSpec((tm, E), lambda i, e: (i, 0)),
                  pl.BlockSpec((1, D, F), lambda i, e: (e, 0, 0)),
                  pl.BlockSpec((1, D, F), lambda i, e: (e, 0, 0)),
                  pl.BlockSpec((1, F, D), lambda i, e: (e, 0, 0)),
                  pl.BlockSpec((1, D), lambda i, e: (0, 0)), pl.BlockSpec((1, D), lambda i, e: (0, 0))],
        out_specs=pl.BlockSpec((tm, D), lambda i, e: (i, 0)),
        out_shape=jax.ShapeDtypeStruct((T, D), F32),
        scratch_shapes=[pltpu.VMEM((tm, D), BF16), pltpu.VMEM((tm, D), F32)],
        compiler_params=_params("parallel", "arbitrary"), name="moe_experts_ln",
    )(x, comb, w_gate, w_up, w_down, ln_g.reshape(1, D), ln_b.reshape(1, D))


def _rotary_tables(S):
    half = RET_HD // 2
    inv_freq = ROPE_BASE ** (-jnp.arange(half, dtype=F32) / half)
    ang = jnp.arange(S).astype(F32)[:, None] * inv_freq[None, :]
    cos, sin = jnp.cos(ang), jnp.sin(ang)
    return jnp.concatenate([cos, cos], axis=1), jnp.concatenate([-sin, sin], axis=1)


def kernel(x, mem, ln_in_g, ln_in_b, router_w, router_bias, w_in, ret_gn_g, rwkv_mu, rwkv_w_up, rwkv_w0,
           rwkv_a_up, rwkv_a0, rwkv_g_up, rwkv_k_k, rwkv_k_a, rwkv_r_k, rwkv_ln_g, rwkv_ln_b,
           w_ret_up, w_rwkv_up, w_out, ln1_g, ln1_b, xa_wq, xa_wkv, xa_wo, ln2_g, ln2_b,
           moe_w_gate, moe_w_up, moe_w_down, ln3_g, ln3_b):
    B, S, D = x.shape
    T = B * S
    depth = w_in.shape[0]
    alpha = (2 * depth) ** 0.25
    ret_w = ret_gn_g.shape[-1]
    rwkv_w = rwkv_w0.shape[-1]
    ret_cols = 4 * ret_w
    rwkv_cols = rwkv_mu.shape[-1]
    cos2, sin2 = _rotary_tables(S)
    mem2 = mem.reshape(B * mem.shape[1], D)

    xs = _layer_norm(x.reshape(T, D), ln_in_g, ln_in_b)
    for l in range(depth):
        w_in_l = w_in[l].astype(BF16)
        p_ret = _matmul(xs, w_in_l[:, :ret_cols], 1024, 512, F32, "in_proj_ret")
        p_rwkv = _matmul(xs, w_in_l[:, ret_cols:ret_cols + rwkv_cols], 1024, rwkv_cols // 2, F32, "in_proj_rwkv")
        p_gate = _matmul(xs, w_in_l[:, ret_cols + rwkv_cols:], 1024, 512, F32, "in_proj_gate")
        y_ret = _retention(p_ret.reshape(B, S, ret_cols), cos2, sin2, ret_gn_g[l])
        y_rwkv = _rwkv(p_rwkv.reshape(B, S, rwkv_cols), rwkv_mu[l], rwkv_w_up[l], rwkv_w0[l], rwkv_a_up[l],
                       rwkv_a0[l], rwkv_g_up[l], rwkv_k_k[l], rwkv_k_a[l], rwkv_r_k[l].reshape(-1),
                       rwkv_ln_g[l], rwkv_ln_b[l])
        xs = _merge(y_ret.reshape(T, ret_w), y_rwkv.reshape(T, rwkv_w), p_gate, xs,
                    w_ret_up[l].astype(BF16), w_rwkv_up[l].astype(BF16), w_out[l].astype(BF16),
                    ln1_g[l], ln1_b[l], alpha)
        kv = _matmul(mem2, xa_wkv[l].astype(BF16), 512, 512, BF16, "xattn_kv")
        xs = _cross_attention(xs.reshape(B, S, D), kv.reshape(B, -1, 2 * D), xa_wq[l].astype(BF16),
                              xa_wo[l].astype(BF16), ln2_g[l], ln2_b[l], alpha).reshape(T, D)
        comb = _router(xs, router_w, router_bias).T
        xs = _moe(xs, comb, moe_w_gate[l].astype(BF16), moe_w_up[l].astype(BF16), moe_w_down[l].astype(BF16),
                  ln3_g[l], ln3_b[l], alpha)
    return xs.reshape(B, S, D)
```

```python
import functools
import math

import jax
import jax.numpy as jnp
from jax import lax
from jax.experimental import pallas as pl
from jax.experimental.pallas import tpu as pltpu

F32 = jnp.float32
BF16 = jnp.bfloat16

LANES = 128
VMEM_LIMIT = 56 * 1024 * 1024

CHUNK = 64
RET_HEADS = 4
RET_HD = 128
ROPE_BASE = 10000.0
RWKV_HD = 64
HD_SHIFT = 6
RWKV_GN_EPS = 64e-5
XA_HEADS = 4
N_GROUPS = 4
TOP_K = 2
LN_EPS = 1e-5
HEADS_PER_GROUP = 4
GROUP_W = HEADS_PER_GROUP * RWKV_HD


def _params(*sem):
    return pltpu.CompilerParams(dimension_semantics=sem, vmem_limit_bytes=VMEM_LIMIT)


def _ln(v, g, b, eps=LN_EPS):
    mu = jnp.mean(v, axis=-1, keepdims=True)
    d = v - mu
    var = jnp.mean(d * d, axis=-1, keepdims=True)
    return d * lax.rsqrt(var + eps) * g + b


def _dot(a, b):
    return jnp.dot(a.astype(BF16), b.astype(BF16), preferred_element_type=F32)


def _dot_nt(a, b):
    return lax.dot_general(a.astype(BF16), b.astype(BF16), (((1,), (1,)), ((), ())),
                           preferred_element_type=F32)


def _dot_tn(a, b):
    return lax.dot_general(a.astype(BF16), b.astype(BF16), (((0,), (0,)), ((), ())),
                           preferred_element_type=F32)


def _ln_kernel(x_ref, g_ref, b_ref, o_ref):
    o_ref[...] = _ln(x_ref[...], g_ref[...], b_ref[...])


def _layer_norm(x, g, b, tm=512):
    T, D = x.shape
    return pl.pallas_call(
        _ln_kernel, grid=(T // tm,),
        in_specs=[pl.BlockSpec((tm, D), lambda i: (i, 0)),
                  pl.BlockSpec((1, D), lambda i: (0, 0)),
                  pl.BlockSpec((1, D), lambda i: (0, 0))],
        out_specs=pl.BlockSpec((tm, D), lambda i: (i, 0)),
        out_shape=jax.ShapeDtypeStruct((T, D), F32),
        compiler_params=_params("parallel"), name="ln_in",
    )(x, g.reshape(1, D), b.reshape(1, D))


def _mm_kernel(a_ref, w_ref, o_ref, abf_ref):
    @pl.when(pl.program_id(1) == 0)
    def _():
        abf_ref[...] = a_ref[...].astype(BF16)

    o_ref[...] = jnp.dot(abf_ref[...], w_ref[...], preferred_element_type=F32).astype(o_ref.dtype)


def _matmul(a, w, tm, tn, out_dtype, name):
    M, K = a.shape
    N = w.shape[1]
    return pl.pallas_call(
        _mm_kernel, grid=(M // tm, N // tn),
        in_specs=[pl.BlockSpec((tm, K), lambda i, j: (i, 0)),
                  pl.BlockSpec((K, tn), lambda i, j: (0, j))],
        out_specs=pl.BlockSpec((tm, tn), lambda i, j: (i, j)),
        out_shape=jax.ShapeDtypeStruct((M, N), out_dtype),
        scratch_shapes=[pltpu.VMEM((tm, K), BF16)],
        compiler_params=_params("parallel", "arbitrary"), name=name,
    )(a, w)


def _ret_kernel(q_ref, k_ref, v_ref, g_ref, cos_ref, sin_ref, gn_ref, o_ref, state_ref, *, n_chunks):
    @pl.when(pl.program_id(1) == 0)
    def _():
        state_ref[...] = jnp.zeros_like(state_ref)

    C, hd = CHUNK, RET_HD
    ti = lax.broadcasted_iota(jnp.int32, (C, C), 0)
    si = lax.broadcasted_iota(jnp.int32, (C, C), 1)
    dist = jnp.abs(ti - si).astype(F32)
    cpos = lax.broadcasted_iota(jnp.int32, (C, 1), 0).astype(F32)
    for h in range(RET_HEADS):
        log_gamma = math.log(1.0 - 2.0 ** (-5.0 - h))
        inner_mask = jnp.exp(log_gamma * dist)
        k_decay = jnp.exp(log_gamma * (C - 1.0 - cpos))
        q_decay = jnp.exp(log_gamma * (cpos + 1.0))
        chunk_decay = math.exp(log_gamma * C)
        hs = slice(h * hd, (h + 1) * hd)
        for c in range(n_chunks):
            rs = slice(c * C, (c + 1) * C)
            cos, sin = cos_ref[rs, :], sin_ref[rs, :]
            q = q_ref[0, rs, hs].astype(F32)
            k = k_ref[0, rs, hs].astype(F32)
            v = v_ref[0, rs, hs]
            qr = q * cos + pltpu.roll(q, hd // 2, axis=1) * sin
            kr = (k * cos + pltpu.roll(k, hd // 2, axis=1) * sin) * (hd ** -0.5)
            state = state_ref[h]
            scores = _dot_nt(qr, kr) * inner_mask
            y = _dot(scores, v) + _dot(qr * q_decay, state)
            state_ref[h] = chunk_decay * state + _dot_tn(kr * k_decay, v)
            mu = jnp.mean(y, axis=-1, keepdims=True)
            d = y - mu
            var = jnp.mean(d * d, axis=-1, keepdims=True)
            yn = d * lax.rsqrt(var + LN_EPS) * gn_ref[:, hs]
            g = g_ref[0, rs, hs].astype(F32)
            o_ref[0, rs, hs] = (g * jax.nn.sigmoid(g) * yn).astype(o_ref.dtype)


def _retention(p_ret, cos2, sin2, gn_g, ts=128):
    B, S, W4 = p_ret.shape
    W = W4 // 4
    blk = lambda c: pl.BlockSpec((1, ts, W), lambda b, n, c=c: (b, n, c))
    return pl.pallas_call(
        functools.partial(_ret_kernel, n_chunks=ts // CHUNK), grid=(B, S // ts),
        in_specs=[blk(0), blk(1), blk(2), blk(3),
                  pl.BlockSpec((ts, RET_HD), lambda b, n: (n, 0)),
                  pl.BlockSpec((ts, RET_HD), lambda b, n: (n, 0)),
                  pl.BlockSpec((1, W), lambda b, n: (0, 0))],
        out_specs=pl.BlockSpec((1, ts, W), lambda b, n: (b, n, 0)),
        out_shape=jax.ShapeDtypeStruct((B, S, W), BF16),
        scratch_shapes=[pltpu.VMEM((RET_HEADS, RET_HD, RET_HD), F32)],
        compiler_params=_params("parallel", "arbitrary"), name="retention",
    )(p_ret, p_ret, p_ret, p_ret, cos2, sin2, gn_g.reshape(1, W))


def _head_sum(x):
    R, W = x.shape
    lo = lax.broadcasted_iota(jnp.int32, (R, LANES), 1) < RWKV_HD
    outs = []
    for j in range(W // LANES):
        xs = x[:, j * LANES:(j + 1) * LANES]
        s_lo = jnp.sum(jnp.where(lo, xs, 0.0), axis=-1, keepdims=True)
        s_hi = jnp.sum(jnp.where(lo, 0.0, xs), axis=-1, keepdims=True)
        outs.append(jnp.where(lo, s_lo, s_hi))
    return jnp.concatenate(outs, axis=1)


def _block_diag(x, head_of_lane):
    parts = [jnp.where(head_of_lane == h, x, 0.0) for h in range(HEADS_PER_GROUP)]
    return jnp.concatenate(parts, axis=0).astype(BF16)


def _packed_mm(xs, ys, head_of_lane):
    bds = [_block_diag(y, head_of_lane) for y in ys]
    return [jnp.dot(x.astype(BF16), bd, preferred_element_type=F32) for x, bd in zip(xs, bds)]


def _unit_lower_inverse(a_list, t_idx, s_idx, head_of_lane):
    mm = functools.partial(_packed_mm, head_of_lane=head_of_lane)
    add = lambda xs, ys: [x + y for x, y in zip(xs, ys)]
    lower = s_idx < t_idx
    same4 = (t_idx >> 2) == (s_idx >> 2)
    same16 = (t_idx >> 4) == (s_idx >> 4)
    m1 = lower & same4
    m2 = lower & same16 & jnp.logical_not(same4)
    m3 = lower & jnp.logical_not(same16)
    eye = jnp.where(t_idx == s_idx, 1.0, 0.0)
    a1 = [jnp.where(m1, a, 0.0) for a in a_list]
    a1sq = mm(a1, a1)
    a1cu = mm(a1, a1sq)
    inv = [eye + p + q + r for p, q, r in zip(a1, a1sq, a1cu)]
    for m in (m2, m3):
        n = mm(inv, [jnp.where(m, a, 0.0) for a in a_list])
        x = add(inv, mm(mm(n, n), inv))
        inv = add(x, mm(n, x))
    return inv


def _rwkv_kernel(*refs):
    state_ref = refs[-1]

    @pl.when(pl.program_id(1) == 0)
    def _():
        for ref in refs[-5:]:
            ref[...] = jnp.zeros_like(ref)

    seqs = [_rwkv_prepare(bi, *refs) for bi in range(state_ref.shape[0])]
    _rwkv_chains(seqs, state_ref)
    for bi, seq in enumerate(seqs):
        _rwkv_finish(bi, seq, *refs)


def _rwkv_prepare(bi, pr_ref, pk_ref, pv_ref, pl_ref, mur_ref, muk_ref, muv_ref, mul_ref,
                  wup_ref, aup_ref, gup_ref, w0_ref, a0_ref, kk_ref, ka_ref, rk_ref, lng_ref, lnb_ref,
                  o_ref, carry_r, carry_k, carry_v, carry_l, state_ref):
    C = CHUNK

    def shifted(p_ref, mu_ref, carry):
        p = p_ref[bi]
        row = lax.broadcasted_iota(jnp.int32, p.shape, 0)
        prev = jnp.where(row == 0, carry[bi, 0:1, :], pltpu.roll(p, 1, axis=0))
        carry[bi, 0:1, :] = p[C - 1:C, :]
        return p + mu_ref[...] * (prev - p)

    r = shifted(pr_ref, mur_ref, carry_r)
    k = shifted(pk_ref, muk_ref, carry_k)
    v = shifted(pv_ref, muv_ref, carry_v)
    lo = shifted(pl_ref, mul_ref, carry_l)
    dwa = lo[:, :LANES]
    dg = lo[:, LANES:]

    z = w0_ref[...] + _dot(jnp.tanh(dwa), wup_ref[...])
    logw = -math.exp(-0.5) * jax.nn.sigmoid(z)
    a = jax.nn.sigmoid(a0_ref[...] + _dot(dwa, aup_ref[...]))
    g = _dot(jax.nn.sigmoid(dg), gup_ref[...])

    kk = k * kk_ref[...]
    kk = kk / jnp.maximum(jnp.sqrt(_head_sum(kk * kk)), 1e-12)
    k = k * (1.0 + (a - 1.0) * ka_ref[...])

    ti = lax.broadcasted_iota(jnp.int32, (C, C), 0)
    si = lax.broadcasted_iota(jnp.int32, (C, C), 1)
    tril = jnp.where(si <= ti, 1.0, 0.0).astype(BF16)
    lw_hi = logw.astype(BF16)
    lw_lo = (logw - lw_hi.astype(F32)).astype(BF16)
    cum = (jnp.dot(tril, lw_hi, preferred_element_type=F32)
           + jnp.dot(tril, lw_lo, preferred_element_type=F32))
    cum_last = cum[C - 1:C, :]
    e_neg = jnp.exp(-cum)
    abar = -kk * jnp.exp(cum - logw)
    rbar = r * jnp.exp(cum)
    kka = kk * a
    btil = kka * e_neg
    ktil = k * e_neg
    e_end = jnp.exp(cum_last - cum)
    bend = kka * e_end
    kend = k * e_end
    gamma_end = jnp.exp(cum_last)
    return dict(r=r, k=k, v=v, g=g, abar=abar, rbar=rbar, btil=btil, ktil=ktil, bend=bend, kend=kend,
                gamma_end=gamma_end)


def _rwkv_chains(seqs, state_ref):
    C = CHUNK
    n_groups = state_ref.shape[1]
    t_idx = lax.broadcasted_iota(jnp.int32, (C, GROUP_W), 0)
    lane = lax.broadcasted_iota(jnp.int32, (C, GROUP_W), 1)
    s_idx = lane & (RWKV_HD - 1)
    head_of_lane = lane >> HD_SHIFT
    strict = s_idx < t_idx
    incl = s_idx <= t_idx
    bd_r = lax.broadcasted_iota(jnp.int32, (GROUP_W, GROUP_W), 0) >> HD_SHIFT
    bd_c = lax.broadcasted_iota(jnp.int32, (GROUP_W, GROUP_W), 1) >> HD_SHIFT
    on_diag = bd_r == bd_c

    chains = [(bi, gi) for bi in range(len(seqs)) for gi in range(n_groups)]
    part = lambda name: [seqs[bi][name][:, gi * GROUP_W:(gi + 1) * GROUP_W] for bi, gi in chains]
    mm = functools.partial(_packed_mm, head_of_lane=head_of_lane)
    v_g = part("v")
    lhs = [jnp.concatenate([p, q], axis=0) for p, q in zip(part("abar"), part("rbar"))]
    rhs = [jnp.concatenate([_block_diag(p, head_of_lane), _block_diag(q, head_of_lane)], axis=0)
           for p, q in zip(part("btil"), part("ktil"))]
    gram = [_dot_nt(p, q) for p, q in zip(lhs, rhs)]
    a_ab = [jnp.where(strict, gm[:C, :GROUP_W], 0.0) for gm in gram]
    a_ak = [jnp.where(strict, gm[:C, GROUP_W:], 0.0) for gm in gram]
    m_rb = [jnp.where(incl, gm[C:, :GROUP_W], 0.0) for gm in gram]
    m_rk = [jnp.where(incl, gm[C:, GROUP_W:], 0.0) for gm in gram]
    inv = _unit_lower_inverse(a_ab, t_idx, s_idx, head_of_lane)
    st = [state_ref[bi, gi] for bi, gi in chains]
    from_state = [_dot_nt(p, q) for p, q in zip(lhs, st)]
    bd_v = [_block_diag(p, head_of_lane) for p in v_g]
    local = [jnp.dot(p.astype(BF16), bd, preferred_element_type=F32) for p, bd in zip(a_ak + m_rk, bd_v + bd_v)]
    n = len(chains)
    u = mm(inv, [fs[:C] + lc for fs, lc in zip(from_state, local[:n])])
    via_u = mm(m_rb, u)
    upd = [_dot_tn(jnp.concatenate([p, q], axis=0), jnp.concatenate([b, k], axis=0))
           for p, q, b, k in zip(u, v_g, part("bend"), part("kend"))]
    for (bi, gi), s_old, up, ge in zip(chains, st, upd, part("gamma_end")):
        state_ref[bi, gi] = s_old * ge + jnp.where(on_diag, up, 0.0)
    ys = [fs[C:] + p + q for fs, p, q in zip(from_state, via_u, local[n:])]
    for bi, seq in enumerate(seqs):
        seq["y"] = jnp.concatenate(ys[bi * n_groups:(bi + 1) * n_groups], axis=1)


def _rwkv_finish(bi, seq, pr_ref, pk_ref, pv_ref, pl_ref, mur_ref, muk_ref, muv_ref, mul_ref,
                 wup_ref, aup_ref, gup_ref, w0_ref, a0_ref, kk_ref, ka_ref, rk_ref, lng_ref, lnb_ref,
                 o_ref, carry_r, carry_k, carry_v, carry_l, state_ref):
    y, r, k, v, g = seq["y"], seq["r"], seq["k"], seq["v"], seq["g"]
    inv_hd = 1.0 / RWKV_HD
    mu = _head_sum(y) * inv_hd
    d = y - mu
    var = _head_sum(d * d) * inv_hd
    yn = d * lax.rsqrt(var + RWKV_GN_EPS) * lng_ref[...] + lnb_ref[...]
    bonus = _head_sum(r * k * rk_ref[...]) * v
    o_ref[bi] = ((yn + bonus) * g).astype(o_ref.dtype)


def _rwkv(p_rwkv, mu, w_up, w0, a_up, a0, g_up, k_k, k_a, r_k, ln_g, ln_b, nb=4):
    B, S, _ = p_rwkv.shape
    W = w0.shape[-1]
    C = CHUNK
    wup_pad = jnp.concatenate([w_up, jnp.zeros_like(a_up)], axis=0).astype(BF16)
    aup_pad = jnp.concatenate([jnp.zeros_like(w_up), a_up], axis=0).astype(BF16)
    row = lambda t: t.reshape(1, -1)
    vec = lambda n: pl.BlockSpec((1, n), lambda b, t: (0, 0))
    mat = lambda a: pl.BlockSpec(a.shape, lambda b, t: (0, 0))
    cblk = lambda c: pl.BlockSpec((nb, C, W), lambda b, t, c=c: (b, t, c))
    lw = p_rwkv.shape[-1] - 3 * W
    n_groups = W // GROUP_W
    gup = g_up.astype(BF16)
    return pl.pallas_call(
        _rwkv_kernel, grid=(B // nb, S // C),
        in_specs=[cblk(0), cblk(1), cblk(2),
                  pl.BlockSpec((nb, C, lw), lambda b, t: (b, t, 3 * W // lw)),
                  vec(W), vec(W), vec(W), vec(lw),
                  mat(wup_pad), mat(aup_pad), mat(gup),
                  vec(W), vec(W), vec(W), vec(W), vec(W), vec(W), vec(W)],
        out_specs=pl.BlockSpec((nb, C, W), lambda b, t: (b, t, 0)),
        out_shape=jax.ShapeDtypeStruct((B, S, W), BF16),
        scratch_shapes=[pltpu.VMEM((nb, 8, W), F32), pltpu.VMEM((nb, 8, W), F32), pltpu.VMEM((nb, 8, W), F32),
                        pltpu.VMEM((nb, 8, lw), F32),
                        pltpu.VMEM((nb, n_groups, GROUP_W, GROUP_W), F32)],
        compiler_params=_params("parallel", "arbitrary"), name="rwkv7",
    )(p_rwkv, p_rwkv, p_rwkv, p_rwkv,
      row(mu[:W]), row(mu[W:2 * W]), row(mu[2 * W:3 * W]), row(mu[3 * W:]),
      wup_pad, aup_pad, gup,
      row(w0), row(a0), row(k_k), row(k_a), row(r_k), row(ln_g), row(ln_b))


def _merge_kernel(yr_ref, yw_ref, pg_ref, x_ref, wr_ref, ww_ref, wo_ref, g_ref, b_ref, o_ref, *, alpha):
    D = x_ref.shape[-1]
    gate = jax.nn.sigmoid(pg_ref[...].astype(F32))
    merged = (gate[:, :D] * jnp.dot(yr_ref[...], wr_ref[...], preferred_element_type=F32)
              + gate[:, D:] * jnp.dot(yw_ref[...], ww_ref[...], preferred_element_type=F32))
    h = jnp.dot(merged.astype(BF16), wo_ref[...], preferred_element_type=F32)
    o_ref[...] = _ln(alpha * x_ref[...] + h, g_ref[...], b_ref[...])


def _merge(y_ret, y_rwkv, p_gate, x, w_ret_up, w_rwkv_up, w_out, ln_g, ln_b, alpha, tm=512):
    T, D = x.shape
    W = y_ret.shape[1]
    full = lambda a: pl.BlockSpec(a.shape, lambda i: (0, 0))
    rowblk = lambda n: pl.BlockSpec((tm, n), lambda i: (i, 0))
    return pl.pallas_call(
        functools.partial(_merge_kernel, alpha=alpha), grid=(T // tm,),
        in_specs=[rowblk(W), rowblk(W), rowblk(2 * D), rowblk(D),
                  full(w_ret_up), full(w_rwkv_up), full(w_out),
                  pl.BlockSpec((1, D), lambda i: (0, 0)), pl.BlockSpec((1, D), lambda i: (0, 0))],
        out_specs=rowblk(D),
        out_shape=jax.ShapeDtypeStruct((T, D), F32),
        compiler_params=_params("parallel"), name="merge_out_ln",
    )(y_ret, y_rwkv, p_gate, x, w_ret_up, w_rwkv_up, w_out, ln_g.reshape(1, D), ln_b.reshape(1, D))


def _xattn_kernel(x_ref, kv_ref, wq_ref, wo_ref, g_ref, b_ref, o_ref, *, alpha):
    x = x_ref[0]
    D = x.shape[-1]
    hd = D // XA_HEADS
    q = jnp.dot(x.astype(BF16), wq_ref[...], preferred_element_type=F32)
    outs = []
    for h in range(XA_HEADS):
        k_h = kv_ref[0, :, h * hd:(h + 1) * hd]
        v_h = kv_ref[0, :, D + h * hd:D + (h + 1) * hd]
        s = _dot_nt(q[:, h * hd:(h + 1) * hd], k_h) * (hd ** -0.5)
        s = s - jnp.max(s, axis=-1, keepdims=True)
        e = jnp.exp(s)
        probs = e / jnp.sum(e, axis=-1, keepdims=True)
        outs.append(jnp.dot(probs.astype(BF16), v_h, preferred_element_type=F32))
    o = jnp.concatenate(outs, axis=1)
    h_out = jnp.dot(o.astype(BF16), wo_ref[...], preferred_element_type=F32)
    o_ref[0] = _ln(alpha * x + h_out, g_ref[...], b_ref[...])


def _cross_attention(x, kv, wq, wo, ln_g, ln_b, alpha, tm=512):
    B, S, D = x.shape
    M = kv.shape[1]
    full = lambda a: pl.BlockSpec(a.shape, lambda b, i: (0, 0))
    return pl.pallas_call(
        functools.partial(_xattn_kernel, alpha=alpha), grid=(B, S // tm),
        in_specs=[pl.BlockSpec((1, tm, D), lambda b, i: (b, i, 0)),
                  pl.BlockSpec((1, M, 2 * D), lambda b, i: (b, 0, 0)),
                  full(wq), full(wo),
                  pl.BlockSpec((1, D), lambda b, i: (0, 0)), pl.BlockSpec((1, D), lambda b, i: (0, 0))],
        out_specs=pl.BlockSpec((1, tm, D), lambda b, i: (b, i, 0)),
        out_shape=jax.ShapeDtypeStruct((B, S, D), F32),
        compiler_params=_params("parallel", "parallel"), name="xattn_ln",
    )(x, kv, wq, wo, ln_g.reshape(1, D), ln_b.reshape(1, D))


def _router_kernel(x_ref, wt_ref, bias_ref, o_ref):
    E = wt_ref.shape[0]
    per_group = E // N_GROUPS
    logits = lax.dot_general(wt_ref[...], x_ref[...], (((1,), (1,)), ((), ())),
                             precision=lax.Precision.HIGHEST, preferred_element_type=F32)
    aff = jax.nn.sigmoid(logits)
    choice = aff + bias_ref[...]
    rows = [choice[e:e + 1, :] for e in range(E)]
    scores = []
    for gidx in range(N_GROUPS):
        members = rows[gidx * per_group:(gidx + 1) * per_group]
        best = None
        for i in range(per_group):
            for j in range(i + 1, per_group):
                pair = members[i] + members[j]
                best = pair if best is None else jnp.maximum(best, pair)
        scores.append(best)
    top = scores[0]
    for s in scores[1:]:
        top = jnp.maximum(top, s)
    taken = jnp.zeros_like(top, dtype=jnp.bool_)
    in_best = []
    for s in scores:
        hit = jnp.logical_and(s == top, jnp.logical_not(taken))
        in_best.append(hit)
        taken = jnp.logical_or(taken, hit)
    sel_w = []
    for gidx in range(N_GROUPS):
        members = rows[gidx * per_group:(gidx + 1) * per_group]
        for i in range(per_group):
            rank = jnp.zeros_like(top)
            for j in range(per_group):
                if j == i:
                    continue
                ahead = (members[j] > members[i]) if j > i else (members[j] >= members[i])
                rank = rank + jnp.where(ahead, 1.0, 0.0)
            chosen = jnp.logical_and(in_best[gidx], rank < TOP_K)
            e = gidx * per_group + i
            sel_w.append(jnp.where(chosen, aff[e:e + 1, :], 0.0))
    total = sel_w[0]
    for w in sel_w[1:]:
        total = total + w
    o_ref[...] = jnp.concatenate(sel_w, axis=0) / total


def _router(x, router_w, router_bias, tm=1024):
    T, D = x.shape
    E = router_w.shape[1]
    return pl.pallas_call(
        _router_kernel, grid=(T // tm,),
        in_specs=[pl.BlockSpec((tm, D), lambda i: (i, 0)),
                  pl.BlockSpec((E, D), lambda i: (0, 0)),
                  pl.BlockSpec((E, 1), lambda i: (0, 0))],
        out_specs=pl.BlockSpec((E, tm), lambda i: (0, i)),
        out_shape=jax.ShapeDtypeStruct((E, T), F32),
        compiler_params=_params("parallel"), name="moe_router",
    )(x, router_w.T, router_bias.reshape(E, 1))


def _moe_kernel(x_ref, comb_ref, wg_ref, wu_ref, wd_ref, g_ref, b_ref, o_ref, xbf_ref, acc_ref, *, alpha):
    e = pl.program_id(1)

    @pl.when(e == 0)
    def _():
        xbf_ref[...] = x_ref[...].astype(BF16)
        acc_ref[...] = jnp.zeros_like(acc_ref)

    xb = xbf_ref[...]
    hg = jnp.dot(xb, wg_ref[0], preferred_element_type=F32)
    hu = jnp.dot(xb, wu_ref[0], preferred_element_type=F32)
    h = hg * jax.nn.sigmoid(hg) * hu
    comb = comb_ref[...]
    lane = lax.broadcasted_iota(jnp.int32, comb.shape, 1)
    w_e = jnp.sum(jnp.where(lane == e, comb, 0.0), axis=-1, keepdims=True)
    acc_ref[...] += w_e * jnp.dot(h.astype(BF16), wd_ref[0], preferred_element_type=F32)

    @pl.when(e == pl.num_programs(1) - 1)
    def _():
        o_ref[...] = _ln(alpha * x_ref[...] + acc_ref[...], g_ref[...], b_ref[...])


def _moe(x, comb, w_gate, w_up, w_down, ln_g, ln_b, alpha, tm=1024):
    T, D = x.shape
    E, _, F = w_gate.shape
    return pl.pallas_call(
        functools.partial(_moe_kernel, alpha=alpha), grid=(T // tm, E),
        in_specs=[pl.BlockSpec((tm, D), lambda i, e: (i, 0)),
                  pl.BlockSpec((tm, E), lambda i, e: (i, 0)),
                  pl.BlockSpec((1, D, F), lambda i, e: (e, 0, 0)),
                  pl.BlockSpec((1, D, F), lambda i, e: (e, 0, 0)),
                  pl.BlockSpec((1, F, D), lambda i, e: (e, 0, 0)),
                  pl.BlockSpec((1, D), lambda i, e: (0, 0)), pl.BlockSpec((1, D), lambda i, e: (0, 0))],
        out_specs=pl.BlockSpec((tm, D), lambda i, e: (i, 0)),
        out_shape=jax.ShapeDtypeStruct((T, D), F32),
        scratch_shapes=[pltpu.VMEM((tm, D), BF16), pltpu.VMEM((tm, D), F32)],
        compiler_params=_params("parallel", "arbitrary"), name="moe_experts_ln",
    )(x, comb, w_gate, w_up, w_down, ln_g.reshape(1, D), ln_b.reshape(1, D))


def _rotary_tables(S):
    half = RET_HD // 2
    inv_freq = ROPE_BASE ** (-jnp.arange(half, dtype=F32) / half)
    ang = jnp.arange(S).astype(F32)[:, None] * inv_freq[None, :]
    cos, sin = jnp.cos(ang), jnp.sin(ang)
    return jnp.concatenate([cos, cos], axis=1), jnp.concatenate([-sin, sin], axis=1)


def kernel(x, mem, ln_in_g, ln_in_b, router_w, router_bias, w_in, ret_gn_g, rwkv_mu, rwkv_w_up, rwkv_w0,
           rwkv_a_up, rwkv_a0, rwkv_g_up, rwkv_k_k, rwkv_k_a, rwkv_r_k, rwkv_ln_g, rwkv_ln_b,
           w_ret_up, w_rwkv_up, w_out, ln1_g, ln1_b, xa_wq, xa_wkv, xa_wo, ln2_g, ln2_b,
           moe_w_gate, moe_w_up, moe_w_down, ln3_g, ln3_b):
    B, S, D = x.shape
    T = B * S
    depth = w_in.shape[0]
    alpha = (2 * depth) ** 0.25
    ret_w = ret_gn_g.shape[-1]
    rwkv_w = rwkv_w0.shape[-1]
    ret_cols = 4 * ret_w
    rwkv_cols = rwkv_mu.shape[-1]
    cos2, sin2 = _rotary_tables(S)
    mem2 = mem.reshape(B * mem.shape[1], D)

    xs = _layer_norm(x.reshape(T, D), ln_in_g, ln_in_b)
    for l in range(depth):
        w_in_l = w_in[l].astype(BF16)
        p_ret = _matmul(xs, w_in_l[:, :ret_cols], 1024, 512, BF16, "in_proj_ret")
        p_rwkv = _matmul(xs, w_in_l[:, ret_cols:ret_cols + rwkv_cols], 1024, rwkv_cols // 2, F32, "in_proj_rwkv")
        p_gate = _matmul(xs, w_in_l[:, ret_cols + rwkv_cols:], 1024, 512, BF16, "in_proj_gate")
        y_ret = _retention(p_ret.reshape(B, S, ret_cols), cos2, sin2, ret_gn_g[l])
        y_rwkv = _rwkv(p_rwkv.reshape(B, S, rwkv_cols), rwkv_mu[l], rwkv_w_up[l], rwkv_w0[l], rwkv_a_up[l],
                       rwkv_a0[l], rwkv_g_up[l], rwkv_k_k[l], rwkv_k_a[l], rwkv_r_k[l].reshape(-1),
                       rwkv_ln_g[l], rwkv_ln_b[l])
        xs = _merge(y_ret.reshape(T, ret_w), y_rwkv.reshape(T, rwkv_w), p_gate, xs,
                    w_ret_up[l].astype(BF16), w_rwkv_up[l].astype(BF16), w_out[l].astype(BF16),
                    ln1_g[l], ln1_b[l], alpha)
        kv = _matmul(mem2, xa_wkv[l].astype(BF16), 512, 512, BF16, "xattn_kv")
        xs = _cross_attention(xs.reshape(B, S, D), kv.reshape(B, -1, 2 * D), xa_wq[l].astype(BF16),
                              xa_wo[l].astype(BF16), ln2_g[l], ln2_b[l], alpha).reshape(T, D)
        comb = _router(xs, router_w, router_bias).T
        xs = _moe(xs, comb, moe_w_gate[l].astype(BF16), moe_w_up[l].astype(BF16), moe_w_down[l].astype(BF16),
                  ln3_g[l], ln3_b[l], alpha)
    return xs.reshape(B, S, D)
```

```python
import functools
import math

import jax
import jax.numpy as jnp
from jax import lax
from jax.experimental import pallas as pl
from jax.experimental.pallas import tpu as pltpu

F32 = jnp.float32
BF16 = jnp.bfloat16

LANES = 128
VMEM_LIMIT = 56 * 1024 * 1024

CHUNK = 64
RET_HEADS = 4
RET_HD = 128
ROPE_BASE = 10000.0
RWKV_HD = 64
HD_SHIFT = 6
RWKV_GN_EPS = 64e-5
XA_HEADS = 4
N_GROUPS = 4
TOP_K = 2
LN_EPS = 1e-5
HEADS_PER_GROUP = 4
GROUP_W = HEADS_PER_GROUP * RWKV_HD


def _params(*sem):
    return pltpu.CompilerParams(dimension_semantics=sem, vmem_limit_bytes=VMEM_LIMIT)


def _ln(v, g, b, eps=LN_EPS):
    mu = jnp.mean(v, axis=-1, keepdims=True)
    d = v - mu
    var = jnp.mean(d * d, axis=-1, keepdims=True)
    return d * lax.rsqrt(var + eps) * g + b


def _dot(a, b):
    return jnp.dot(a.astype(BF16), b.astype(BF16), preferred_element_type=F32)


def _dot_nt(a, b):
    return lax.dot_general(a.astype(BF16), b.astype(BF16), (((1,), (1,)), ((), ())),
                           preferred_element_type=F32)


def _dot_tn(a, b):
    return lax.dot_general(a.astype(BF16), b.astype(BF16), (((0,), (0,)), ((), ())),
                           preferred_element_type=F32)


def _ln_kernel(x_ref, g_ref, b_ref, o_ref):
    o_ref[...] = _ln(x_ref[...], g_ref[...], b_ref[...])


def _layer_norm(x, g, b, tm=512):
    T, D = x.shape
    return pl.pallas_call(
        _ln_kernel, grid=(T // tm,),
        in_specs=[pl.BlockSpec((tm, D), lambda i: (i, 0)),
                  pl.BlockSpec((1, D), lambda i: (0, 0)),
                  pl.BlockSpec((1, D), lambda i: (0, 0))],
        out_specs=pl.BlockSpec((tm, D), lambda i: (i, 0)),
        out_shape=jax.ShapeDtypeStruct((T, D), F32),
        compiler_params=_params("parallel"), name="ln_in",
    )(x, g.reshape(1, D), b.reshape(1, D))


def _mm_kernel(a_ref, w_ref, o_ref, abf_ref):
    @pl.when(pl.program_id(1) == 0)
    def _():
        abf_ref[...] = a_ref[...].astype(BF16)

    o_ref[...] = jnp.dot(abf_ref[...], w_ref[...], preferred_element_type=F32).astype(o_ref.dtype)


def _matmul(a, w, tm, tn, out_dtype, name):
    M, K = a.shape
    N = w.shape[1]
    return pl.pallas_call(
        _mm_kernel, grid=(M // tm, N // tn),
        in_specs=[pl.BlockSpec((tm, K), lambda i, j: (i, 0)),
                  pl.BlockSpec((K, tn), lambda i, j: (0, j))],
        out_specs=pl.BlockSpec((tm, tn), lambda i, j: (i, j)),
        out_shape=jax.ShapeDtypeStruct((M, N), out_dtype),
        scratch_shapes=[pltpu.VMEM((tm, K), BF16)],
        compiler_params=_params("parallel", "arbitrary"), name=name,
    )(a, w)


def _ret_kernel(q_ref, k_ref, v_ref, g_ref, cos_ref, sin_ref, gn_ref, o_ref, state_ref, *, n_chunks):
    @pl.when(pl.program_id(1) == 0)
    def _():
        state_ref[...] = jnp.zeros_like(state_ref)

    C, hd = CHUNK, RET_HD
    ti = lax.broadcasted_iota(jnp.int32, (C, C), 0)
    si = lax.broadcasted_iota(jnp.int32, (C, C), 1)
    dist = jnp.abs(ti - si).astype(F32)
    cpos = lax.broadcasted_iota(jnp.int32, (C, 1), 0).astype(F32)
    for h in range(RET_HEADS):
        log_gamma = math.log(1.0 - 2.0 ** (-5.0 - h))
        inner_mask = jnp.exp(log_gamma * dist)
        k_decay = jnp.exp(log_gamma * (C - 1.0 - cpos))
        q_decay = jnp.exp(log_gamma * (cpos + 1.0))
        chunk_decay = math.exp(log_gamma * C)
        hs = slice(h * hd, (h + 1) * hd)
        for c in range(n_chunks):
            rs = slice(c * C, (c + 1) * C)
            cos, sin = cos_ref[rs, :], sin_ref[rs, :]
            q = q_ref[0, rs, hs].astype(F32)
            k = k_ref[0, rs, hs].astype(F32)
            v = v_ref[0, rs, hs]
            qr = q * cos + pltpu.roll(q, hd // 2, axis=1) * sin
            kr = (k * cos + pltpu.roll(k, hd // 2, axis=1) * sin) * (hd ** -0.5)
            state = state_ref[h]
            scores = _dot_nt(qr, kr) * inner_mask
            y = _dot(scores, v) + _dot(qr * q_decay, state)
            state_ref[h] = chunk_decay * state + _dot_tn(kr * k_decay, v)
            mu = jnp.mean(y, axis=-1, keepdims=True)
            d = y - mu
            var = jnp.mean(d * d, axis=-1, keepdims=True)
            yn = d * lax.rsqrt(var + LN_EPS) * gn_ref[:, hs]
            g = g_ref[0, rs, hs].astype(F32)
            o_ref[0, rs, hs] = (g * jax.nn.sigmoid(g) * yn).astype(o_ref.dtype)


def _retention(p_ret, cos2, sin2, gn_g, ts=128):
    B, S, W4 = p_ret.shape
    W = W4 // 4
    blk = lambda c: pl.BlockSpec((1, ts, W), lambda b, n, c=c: (b, n, c))
    return pl.pallas_call(
        functools.partial(_ret_kernel, n_chunks=ts // CHUNK), grid=(B, S // ts),
        in_specs=[blk(0), blk(1), blk(2), blk(3),
                  pl.BlockSpec((ts, RET_HD), lambda b, n: (n, 0)),
                  pl.BlockSpec((ts, RET_HD), lambda b, n: (n, 0)),
                  pl.BlockSpec((1, W), lambda b, n: (0, 0))],
        out_specs=pl.BlockSpec((1, ts, W), lambda b, n: (b, n, 0)),
        out_shape=jax.ShapeDtypeStruct((B, S, W), BF16),
        scratch_shapes=[pltpu.VMEM((RET_HEADS, RET_HD, RET_HD), F32)],
        compiler_params=_params("parallel", "arbitrary"), name="retention",
    )(p_ret, p_ret, p_ret, p_ret, cos2, sin2, gn_g.reshape(1, W))


def _head_sum(x):
    R, W = x.shape
    lo = lax.broadcasted_iota(jnp.int32, (R, LANES), 1) < RWKV_HD
    outs = []
    for j in range(W // LANES):
        xs = x[:, j * LANES:(j + 1) * LANES]
        s_lo = jnp.sum(jnp.where(lo, xs, 0.0), axis=-1, keepdims=True)
        s_hi = jnp.sum(jnp.where(lo, 0.0, xs), axis=-1, keepdims=True)
        outs.append(jnp.where(lo, s_lo, s_hi))
    return jnp.concatenate(outs, axis=1)


def _block_diag(x, head_of_lane):
    parts = [jnp.where(head_of_lane == h, x, 0.0) for h in range(HEADS_PER_GROUP)]
    return jnp.concatenate(parts, axis=0).astype(BF16)


def _packed_mm(xs, ys, head_of_lane):
    bds = [_block_diag(y, head_of_lane) for y in ys]
    return [jnp.dot(x.astype(BF16), bd, preferred_element_type=F32) for x, bd in zip(xs, bds)]


def _unit_lower_inverse(a_list, t_idx, s_idx, head_of_lane):
    mm = functools.partial(_packed_mm, head_of_lane=head_of_lane)
    add = lambda xs, ys: [x + y for x, y in zip(xs, ys)]
    lower = s_idx < t_idx
    same4 = (t_idx >> 2) == (s_idx >> 2)
    same16 = (t_idx >> 4) == (s_idx >> 4)
    m1 = lower & same4
    m2 = lower & same16 & jnp.logical_not(same4)
    m3 = lower & jnp.logical_not(same16)
    eye = jnp.where(t_idx == s_idx, 1.0, 0.0)
    a1 = [jnp.where(m1, a, 0.0) for a in a_list]
    a1sq = mm(a1, a1)
    a1cu = mm(a1, a1sq)
    inv = [eye + p + q + r for p, q, r in zip(a1, a1sq, a1cu)]
    for m in (m2, m3):
        n = mm(inv, [jnp.where(m, a, 0.0) for a in a_list])
        x = add(inv, mm(mm(n, n), inv))
        inv = add(x, mm(n, x))
    return inv


def _rwkv_kernel(*refs):
    state_ref = refs[-1]

    @pl.when(pl.program_id(1) == 0)
    def _():
        for ref in refs[-5:]:
            ref[...] = jnp.zeros_like(ref)

    seqs = [_rwkv_prepare(bi, *refs) for bi in range(state_ref.shape[0])]
    _rwkv_chains(seqs, state_ref)
    for bi, seq in enumerate(seqs):
        _rwkv_finish(bi, seq, *refs)


def _rwkv_prepare(bi, pr_ref, pk_ref, pv_ref, pl_ref, mur_ref, muk_ref, muv_ref, mul_ref,
                  wup_ref, aup_ref, gup_ref, w0_ref, a0_ref, kk_ref, ka_ref, rk_ref, lng_ref, lnb_ref,
                  o_ref, carry_r, carry_k, carry_v, carry_l, state_ref):
    C = CHUNK

    def shifted(p_ref, mu_ref, carry):
        p = p_ref[bi]
        row = lax.broadcasted_iota(jnp.int32, p.shape, 0)
        prev = jnp.where(row == 0, carry[bi, 0:1, :], pltpu.roll(p, 1, axis=0))
        carry[bi, 0:1, :] = p[C - 1:C, :]
        return p + mu_ref[...] * (prev - p)

    r = shifted(pr_ref, mur_ref, carry_r)
    k = shifted(pk_ref, muk_ref, carry_k)
    v = shifted(pv_ref, muv_ref, carry_v)
    lo = shifted(pl_ref, mul_ref, carry_l)
    dwa = lo[:, :LANES]
    dg = lo[:, LANES:]

    z = w0_ref[...] + _dot(jnp.tanh(dwa), wup_ref[...])
    logw = -math.exp(-0.5) * jax.nn.sigmoid(z)
    a = jax.nn.sigmoid(a0_ref[...] + _dot(dwa, aup_ref[...]))
    g = _dot(jax.nn.sigmoid(dg), gup_ref[...])

    kk = k * kk_ref[...]
    kk = kk / jnp.maximum(jnp.sqrt(_head_sum(kk * kk)), 1e-12)
    k = k * (1.0 + (a - 1.0) * ka_ref[...])

    ti = lax.broadcasted_iota(jnp.int32, (C, C), 0)
    si = lax.broadcasted_iota(jnp.int32, (C, C), 1)
    tril = jnp.where(si <= ti, 1.0, 0.0).astype(BF16)
    lw_hi = logw.astype(BF16)
    lw_lo = (logw - lw_hi.astype(F32)).astype(BF16)
    cum = (jnp.dot(tril, lw_hi, preferred_element_type=F32)
           + jnp.dot(tril, lw_lo, preferred_element_type=F32))
    cum_last = cum[C - 1:C, :]
    e_neg = jnp.exp(-cum)
    abar = -kk * jnp.exp(cum - logw)
    rbar = r * jnp.exp(cum)
    kka = kk * a
    btil = kka * e_neg
    ktil = k * e_neg
    e_end = jnp.exp(cum_last - cum)
    bend = kka * e_end
    kend = k * e_end
    gamma_end = jnp.exp(cum_last)
    return dict(r=r, k=k, v=v, g=g, abar=abar, rbar=rbar, btil=btil, ktil=ktil, bend=bend, kend=kend,
                gamma_end=gamma_end)


def _rwkv_chains(seqs, state_ref):
    C = CHUNK
    n_groups = state_ref.shape[1]
    t_idx = lax.broadcasted_iota(jnp.int32, (C, GROUP_W), 0)
    lane = lax.broadcasted_iota(jnp.int32, (C, GROUP_W), 1)
    s_idx = lane & (RWKV_HD - 1)
    head_of_lane = lane >> HD_SHIFT
    strict = s_idx < t_idx
    incl = s_idx <= t_idx
    bd_r = lax.broadcasted_iota(jnp.int32, (GROUP_W, GROUP_W), 0) >> HD_SHIFT
    bd_c = lax.broadcasted_iota(jnp.int32, (GROUP_W, GROUP_W), 1) >> HD_SHIFT
    on_diag = bd_r == bd_c

    chains = [(bi, gi) for bi in range(len(seqs)) for gi in range(n_groups)]
    part = lambda name: [seqs[bi][name][:, gi * GROUP_W:(gi + 1) * GROUP_W] for bi, gi in chains]
    mm = functools.partial(_packed_mm, head_of_lane=head_of_lane)
    v_g = part("v")
    lhs = [jnp.concatenate([p, q], axis=0) for p, q in zip(part("abar"), part("rbar"))]
    rhs = [jnp.concatenate([_block_diag(p, head_of_lane), _block_diag(q, head_of_lane)], axis=0)
           for p, q in zip(part("btil"), part("ktil"))]
    gram = [_dot_nt(p, q) for p, q in zip(lhs, rhs)]
    a_ab = [jnp.where(strict, gm[:C, :GROUP_W], 0.0) for gm in gram]
    a_ak = [jnp.where(strict, gm[:C, GROUP_W:], 0.0) for gm in gram]
    m_rb = [jnp.where(incl, gm[C:, :GROUP_W], 0.0) for gm in gram]
    m_rk = [jnp.where(incl, gm[C:, GROUP_W:], 0.0) for gm in gram]
    inv = _unit_lower_inverse(a_ab, t_idx, s_idx, head_of_lane)
    st = [state_ref[bi, gi] for bi, gi in chains]
    from_state = [_dot_nt(p, q) for p, q in zip(lhs, st)]
    bd_v = [_block_diag(p, head_of_lane) for p in v_g]
    local = [jnp.dot(p.astype(BF16), bd, preferred_element_type=F32) for p, bd in zip(a_ak + m_rk, bd_v + bd_v)]
    n = len(chains)
    u = mm(inv, [fs[:C] + lc for fs, lc in zip(from_state, local[:n])])
    via_u = mm(m_rb, u)
    upd = [_dot_tn(jnp.concatenate([p, q], axis=0), jnp.concatenate([b, k], axis=0))
           for p, q, b, k in zip(u, v_g, part("bend"), part("kend"))]
    for (bi, gi), s_old, up, ge in zip(chains, st, upd, part("gamma_end")):
        state_ref[bi, gi] = s_old * ge + jnp.where(on_diag, up, 0.0)
    ys = [fs[C:] + p + q for fs, p, q in zip(from_state, via_u, local[n:])]
    for bi, seq in enumerate(seqs):
        seq["y"] = jnp.concatenate(ys[bi * n_groups:(bi + 1) * n_groups], axis=1)


def _rwkv_finish(bi, seq, pr_ref, pk_ref, pv_ref, pl_ref, mur_ref, muk_ref, muv_ref, mul_ref,
                 wup_ref, aup_ref, gup_ref, w0_ref, a0_ref, kk_ref, ka_ref, rk_ref, lng_ref, lnb_ref,
                 o_ref, carry_r, carry_k, carry_v, carry_l, state_ref):
    y, r, k, v, g = seq["y"], seq["r"], seq["k"], seq["v"], seq["g"]
    inv_hd = 1.0 / RWKV_HD
    mu = _head_sum(y) * inv_hd
    d = y - mu
    var = _head_sum(d * d) * inv_hd
    yn = d * lax.rsqrt(var + RWKV_GN_EPS) * lng_ref[...] + lnb_ref[...]
    bonus = _head_sum(r * k * rk_ref[...]) * v
    o_ref[bi] = ((yn + bonus) * g).astype(o_ref.dtype)


def _rwkv(p_rwkv, mu, w_up, w0, a_up, a0, g_up, k_k, k_a, r_k, ln_g, ln_b, nb=4):
    B, S, _ = p_rwkv.shape
    W = w0.shape[-1]
    C = CHUNK
    wup_pad = jnp.concatenate([w_up, jnp.zeros_like(a_up)], axis=0).astype(BF16)
    aup_pad = jnp.concatenate([jnp.zeros_like(w_up), a_up], axis=0).astype(BF16)
    row = lambda t: t.reshape(1, -1)
    vec = lambda n: pl.BlockSpec((1, n), lambda b, t: (0, 0))
    mat = lambda a: pl.BlockSpec(a.shape, lambda b, t: (0, 0))
    cblk = lambda c: pl.BlockSpec((nb, C, W), lambda b, t, c=c: (b, t, c))
    lw = p_rwkv.shape[-1] - 3 * W
    n_groups = W // GROUP_W
    gup = g_up.astype(BF16)
    return pl.pallas_call(
        _rwkv_kernel, grid=(B // nb, S // C),
        in_specs=[cblk(0), cblk(1), cblk(2),
                  pl.BlockSpec((nb, C, lw), lambda b, t: (b, t, 3 * W // lw)),
                  vec(W), vec(W), vec(W), vec(lw),
                  mat(wup_pad), mat(aup_pad), mat(gup),
                  vec(W), vec(W), vec(W), vec(W), vec(W), vec(W), vec(W)],
        out_specs=pl.BlockSpec((nb, C, W), lambda b, t: (b, t, 0)),
        out_shape=jax.ShapeDtypeStruct((B, S, W), BF16),
        scratch_shapes=[pltpu.VMEM((nb, 8, W), F32), pltpu.VMEM((nb, 8, W), F32), pltpu.VMEM((nb, 8, W), F32),
                        pltpu.VMEM((nb, 8, lw), F32),
                        pltpu.VMEM((nb, n_groups, GROUP_W, GROUP_W), F32)],
        compiler_params=_params("parallel", "arbitrary"), name="rwkv7",
    )(p_rwkv, p_rwkv, p_rwkv, p_rwkv,
      row(mu[:W]), row(mu[W:2 * W]), row(mu[2 * W:3 * W]), row(mu[3 * W:]),
      wup_pad, aup_pad, gup,
      row(w0), row(a0), row(k_k), row(k_a), row(r_k), row(ln_g), row(ln_b))


def _merge_kernel(yr_ref, yw_ref, pg_ref, x_ref, wr_ref, ww_ref, wo_ref, g_ref, b_ref, o_ref, *, alpha):
    D = x_ref.shape[-1]
    gate = jax.nn.sigmoid(pg_ref[...].astype(F32))
    merged = (gate[:, :D] * jnp.dot(yr_ref[...], wr_ref[...], preferred_element_type=F32)
              + gate[:, D:] * jnp.dot(yw_ref[...], ww_ref[...], preferred_element_type=F32))
    h = jnp.dot(merged.astype(BF16), wo_ref[...], preferred_element_type=F32)
    o_ref[...] = _ln(alpha * x_ref[...] + h, g_ref[...], b_ref[...])


def _merge(y_ret, y_rwkv, p_gate, x, w_ret_up, w_rwkv_up, w_out, ln_g, ln_b, alpha, tm=512):
    T, D = x.shape
    W = y_ret.shape[1]
    full = lambda a: pl.BlockSpec(a.shape, lambda i: (0, 0))
    rowblk = lambda n: pl.BlockSpec((tm, n), lambda i: (i, 0))
    return pl.pallas_call(
        functools.partial(_merge_kernel, alpha=alpha), grid=(T // tm,),
        in_specs=[rowblk(W), rowblk(W), rowblk(2 * D), rowblk(D),
                  full(w_ret_up), full(w_rwkv_up), full(w_out),
                  pl.BlockSpec((1, D), lambda i: (0, 0)), pl.BlockSpec((1, D), lambda i: (0, 0))],
        out_specs=rowblk(D),
        out_shape=jax.ShapeDtypeStruct((T, D), F32),
        compiler_params=_params("parallel"), name="merge_out_ln",
    )(y_ret, y_rwkv, p_gate, x, w_ret_up, w_rwkv_up, w_out, ln_g.reshape(1, D), ln_b.reshape(1, D))


def _xattn_kernel(x_ref, kv_ref, wq_ref, wo_ref, g_ref, b_ref, rwt_ref, rb_ref, o_ref, cls_ref, *, alpha):
    x = x_ref[0]
    D = x.shape[-1]
    hd = D // XA_HEADS
    q = jnp.dot(x.astype(BF16), wq_ref[...], preferred_element_type=F32)
    outs = []
    for h in range(XA_HEADS):
        k_h = kv_ref[0, :, h * hd:(h + 1) * hd]
        v_h = kv_ref[0, :, D + h * hd:D + (h + 1) * hd]
        s = _dot_nt(q[:, h * hd:(h + 1) * hd], k_h) * (hd ** -0.5)
        s = s - jnp.max(s, axis=-1, keepdims=True)
        e = jnp.exp(s)
        probs = e / jnp.sum(e, axis=-1, keepdims=True)
        outs.append(jnp.dot(probs.astype(BF16), v_h, preferred_element_type=F32))
    o = jnp.concatenate(outs, axis=1)
    h_out = jnp.dot(o.astype(BF16), wo_ref[...], preferred_element_type=F32)
    x2 = _ln(alpha * x + h_out, g_ref[...], b_ref[...])
    comb, cls = _route(x2, rwt_ref[...], rb_ref[...])
    E, tm = comb.shape
    comb_pad = jnp.concatenate([comb, jnp.zeros((LANES - E, tm), F32)], axis=0)
    o_ref[0, :, :D] = x2
    o_ref[0, :, D:] = comb_pad.T
    cls_ref[0] = cls


def _cross_attention(x, kv, wq, wo, ln_g, ln_b, router_w, router_bias, alpha, tm=512):
    B, S, D = x.shape
    M = kv.shape[1]
    E = router_w.shape[1]
    full = lambda a: pl.BlockSpec(a.shape, lambda b, i: (0, 0))
    n_t = S // tm
    return pl.pallas_call(
        functools.partial(_xattn_kernel, alpha=alpha), grid=(B, n_t),
        in_specs=[pl.BlockSpec((1, tm, D), lambda b, i: (b, i, 0)),
                  pl.BlockSpec((1, M, 2 * D), lambda b, i: (b, 0, 0)),
                  full(wq), full(wo),
                  pl.BlockSpec((1, D), lambda b, i: (0, 0)), pl.BlockSpec((1, D), lambda b, i: (0, 0)),
                  pl.BlockSpec((E, D), lambda b, i: (0, 0)), pl.BlockSpec((E, 1), lambda b, i: (0, 0))],
        out_specs=[pl.BlockSpec((1, tm, D + LANES), lambda b, i: (b, i, 0)),
                   pl.BlockSpec((1, 1, tm), lambda b, i: (b * n_t + i, 0, 0))],
        out_shape=[jax.ShapeDtypeStruct((B, S, D + LANES), F32),
                   jax.ShapeDtypeStruct((B * n_t, 1, tm), jnp.int32)],
        compiler_params=_params("parallel", "parallel"), name="xattn_ln_route",
    )(x, kv, wq, wo, ln_g.reshape(1, D), ln_b.reshape(1, D), router_w.T, router_bias.reshape(E, 1))


def _expert_pairs(per_group):
    return [(i, j) for i in range(per_group) for j in range(i + 1, per_group)]


def _route(x, wt, bias):
    E = wt.shape[0]
    per_group = E // N_GROUPS
    logits = lax.dot_general(wt, x, (((1,), (1,)), ((), ())),
                             precision=lax.Precision.HIGHEST, preferred_element_type=F32)
    aff = jax.nn.sigmoid(logits)
    choice = aff + bias
    rows = [choice[e:e + 1, :] for e in range(E)]
    scores = []
    for gidx in range(N_GROUPS):
        members = rows[gidx * per_group:(gidx + 1) * per_group]
        best = None
        for i in range(per_group):
            for j in range(i + 1, per_group):
                pair = members[i] + members[j]
                best = pair if best is None else jnp.maximum(best, pair)
        scores.append(best)
    top = scores[0]
    for s in scores[1:]:
        top = jnp.maximum(top, s)
    taken = jnp.zeros_like(top, dtype=jnp.bool_)
    in_best = []
    for s in scores:
        hit = jnp.logical_and(s == top, jnp.logical_not(taken))
        in_best.append(hit)
        taken = jnp.logical_or(taken, hit)
    sel_w = []
    cls = jnp.zeros(top.shape, jnp.int32)
    pairs = _expert_pairs(per_group)
    for gidx in range(N_GROUPS):
        members = rows[gidx * per_group:(gidx + 1) * per_group]
        chosen = []
        for i in range(per_group):
            rank = jnp.zeros_like(top)
            for j in range(per_group):
                if j == i:
                    continue
                ahead = (members[j] > members[i]) if j > i else (members[j] >= members[i])
                rank = rank + jnp.where(ahead, 1.0, 0.0)
            chosen.append(jnp.logical_and(in_best[gidx], rank < TOP_K))
            e = gidx * per_group + i
            sel_w.append(jnp.where(chosen[i], aff[e:e + 1, :], 0.0))
        for p, (i, j) in enumerate(pairs):
            cls = jnp.where(jnp.logical_and(chosen[i], chosen[j]), gidx * len(pairs) + p, cls)
    total = sel_w[0]
    for w in sel_w[1:]:
        total = total + w
    return jnp.concatenate(sel_w, axis=0) / total, cls


def _rank_kernel(cls_ref, rank_ref, cnt_ref, base_ref):
    @pl.when(pl.program_id(0) == 0)
    def _():
        base_ref[...] = jnp.zeros_like(base_ref)

    cls = cls_ref[0]
    tm = cls.shape[-1]
    n_cls = base_ref.shape[0]
    onehot = lax.broadcasted_iota(jnp.int32, (n_cls, tm), 0) == cls
    earlier = (lax.broadcasted_iota(jnp.int32, (tm, tm), 0)
               < lax.broadcasted_iota(jnp.int32, (tm, tm), 1))
    ones = jnp.where(onehot, 1.0, 0.0)
    before = jnp.dot(ones.astype(BF16), jnp.where(earlier, 1.0, 0.0).astype(BF16),
                     preferred_element_type=F32)
    base = base_ref[...]
    rank = jnp.sum(jnp.where(onehot, base + before, 0.0), axis=0, keepdims=True)
    rank_ref[0] = rank.astype(jnp.int32)
    total = base + jnp.sum(ones, axis=1, keepdims=True)
    base_ref[...] = total
    cnt_ref[...] = jnp.broadcast_to(total, cnt_ref.shape)


def _class_ranks(cls, n_cls_pad):
    n_t, _, tm = cls.shape
    return pl.pallas_call(
        _rank_kernel, grid=(n_t,),
        in_specs=[pl.BlockSpec((1, 1, tm), lambda i: (i, 0, 0))],
        out_specs=[pl.BlockSpec((1, 1, tm), lambda i: (i, 0, 0)),
                   pl.BlockSpec((n_cls_pad, LANES), lambda i: (0, 0))],
        out_shape=[jax.ShapeDtypeStruct((n_t, 1, tm), jnp.int32),
                   jax.ShapeDtypeStruct((n_cls_pad, LANES), F32)],
        scratch_shapes=[pltpu.VMEM((n_cls_pad, 1), F32)],
        compiler_params=_params("arbitrary"), name="moe_class_rank",
    )(cls)


def _row_copy(src, src_row, dst, dst_row, sem):
    return pltpu.make_async_copy(src.at[pl.ds(src_row, 1)], dst.at[pl.ds(dst_row, 1)], sem)


def _dispatch_kernel(dest_ref, x_ref, o_hbm, sem):
    tm = x_ref.shape[0]
    base = pl.program_id(0) * tm

    def start(r, carry):
        _row_copy(x_ref, r, o_hbm, dest_ref[base + r], sem).start()
        return carry

    def wait(r, carry):
        _row_copy(x_ref, r, o_hbm, dest_ref[base + r], sem).wait()
        return carry

    lax.fori_loop(0, tm, start, 0, unroll=8)
    lax.fori_loop(0, tm, wait, 0, unroll=8)


def _dispatch(xa, dest, tm=512):
    T, W = xa.shape
    return pl.pallas_call(
        _dispatch_kernel,
        grid_spec=pltpu.PrefetchScalarGridSpec(
            num_scalar_prefetch=1, grid=(T // tm,),
            in_specs=[pl.BlockSpec((tm, W), lambda i, dest: (i, 0))],
            out_specs=pl.BlockSpec(memory_space=pl.ANY),
            scratch_shapes=[pltpu.SemaphoreType.DMA(())]),
        out_shape=jax.ShapeDtypeStruct((T, W), xa.dtype),
        compiler_params=_params("arbitrary"), name="moe_dispatch",
    )(dest, xa)


def _undispatch_kernel(dest_ref, y_hbm, o_ref, sem):
    tm = o_ref.shape[0]
    base = pl.program_id(0) * tm

    def start(r, carry):
        _row_copy(y_hbm, dest_ref[base + r], o_ref, r, sem).start()
        return carry

    def wait(r, carry):
        _row_copy(y_hbm, dest_ref[base + r], o_ref, r, sem).wait()
        return carry

    lax.fori_loop(0, tm, start, 0, unroll=8)
    lax.fori_loop(0, tm, wait, 0, unroll=8)


def _undispatch(ys, dest, tm=512):
    T, D = ys.shape
    return pl.pallas_call(
        _undispatch_kernel,
        grid_spec=pltpu.PrefetchScalarGridSpec(
            num_scalar_prefetch=1, grid=(T // tm,),
            in_specs=[pl.BlockSpec(memory_space=pl.ANY)],
            out_specs=pl.BlockSpec((tm, D), lambda i, dest: (i, 0)),
            scratch_shapes=[pltpu.SemaphoreType.DMA(())]),
        out_shape=jax.ShapeDtypeStruct((T, D), ys.dtype),
        compiler_params=_params("arbitrary"), name="moe_undispatch",
    )(dest, ys)


STEP_FIRST, STEP_LAST, STEP_VALID = 1, 2, 4


def _moe_kernel(tile_ref, exp_ref, flag_ref, xs_ref, wg_ref, wu_ref, wd_ref, g_ref, b_ref, o_ref,
                xbf_ref, acc_ref, *, alpha):
    step = pl.program_id(0)
    flags = flag_ref[step]
    e = exp_ref[step]
    D = o_ref.shape[-1]

    @pl.when((flags & STEP_FIRST) != 0)
    def _():
        xbf_ref[...] = xs_ref[:, :D].astype(BF16)
        acc_ref[...] = jnp.zeros_like(acc_ref)

    @pl.when((flags & STEP_VALID) != 0)
    def _():
        xb = xbf_ref[...]
        hg = jnp.dot(xb, wg_ref[0], preferred_element_type=F32)
        hu = jnp.dot(xb, wu_ref[0], preferred_element_type=F32)
        h = hg * jax.nn.sigmoid(hg) * hu
        comb = xs_ref[:, D:]
        lane = lax.broadcasted_iota(jnp.int32, comb.shape, 1)
        w_e = jnp.sum(jnp.where(lane == e, comb, 0.0), axis=-1, keepdims=True)
        acc_ref[...] += w_e * jnp.dot(h.astype(BF16), wd_ref[0], preferred_element_type=F32)

    @pl.when((flags & STEP_LAST) != 0)
    def _():
        o_ref[...] = _ln(alpha * xs_ref[:, :D] + acc_ref[...], g_ref[...], b_ref[...])


def _moe_steps(counts, n_experts, n_tiles, tm):
    n_cls = counts.shape[0]
    per_group = n_experts // N_GROUPS
    pairs = _expert_pairs(per_group)
    uses = [[0.0] * n_experts for _ in range(n_cls)]
    for c in range(n_cls):
        gidx, (i, j) = c // len(pairs), pairs[c % len(pairs)]
        uses[c][gidx * per_group + i] = 1.0
        uses[c][gidx * per_group + j] = 1.0
    uses = jnp.array(uses, F32)
    ends = jnp.cumsum(counts)
    starts = ends - counts
    lo = (jnp.arange(n_tiles, dtype=jnp.int32) * tm)[:, None]
    present = jnp.logical_and(starts[None, :] < lo + tm, ends[None, :] > lo)
    need = (jnp.dot(present.astype(F32), uses) > 0).reshape(-1)
    n_steps = min(n_tiles * n_experts, TOP_K * (n_tiles + n_cls - 1))
    n_valid = jnp.sum(need.astype(jnp.int32))
    idx = jnp.nonzero(need, size=n_steps, fill_value=0)[0].astype(jnp.int32)
    k = jnp.arange(n_steps, dtype=jnp.int32)
    valid = k < n_valid
    idx = jnp.where(valid, idx, idx[n_valid - 1])
    tile, expert = idx // n_experts, idx % n_experts
    prev_tile = jnp.concatenate([jnp.full((1,), -1, jnp.int32), tile[:-1]])
    next_tile = jnp.concatenate([tile[1:], jnp.full((1,), -1, jnp.int32)])
    first = jnp.logical_and(valid, tile != prev_tile)
    last = jnp.logical_and(valid, jnp.logical_or(tile != next_tile, k == n_valid - 1))
    flags = (first * STEP_FIRST + last * STEP_LAST + valid * STEP_VALID).astype(jnp.int32)
    return tile, expert, flags


def _moe_experts(xs, steps, w_gate, w_up, w_down, ln_g, ln_b, alpha, tm):
    T, W = xs.shape
    D = W - LANES
    E, _, F = w_gate.shape
    tile, expert, flags = steps
    return pl.pallas_call(
        functools.partial(_moe_kernel, alpha=alpha),
        grid_spec=pltpu.PrefetchScalarGridSpec(
            num_scalar_prefetch=3, grid=(tile.shape[0],),
            in_specs=[pl.BlockSpec((tm, W), lambda s, t, e, f: (t[s], 0)),
                      pl.BlockSpec((1, D, F), lambda s, t, e, f: (e[s], 0, 0)),
                      pl.BlockSpec((1, D, F), lambda s, t, e, f: (e[s], 0, 0)),
                      pl.BlockSpec((1, F, D), lambda s, t, e, f: (e[s], 0, 0)),
                      pl.BlockSpec((1, D), lambda s, t, e, f: (0, 0)),
                      pl.BlockSpec((1, D), lambda s, t, e, f: (0, 0))],
            out_specs=pl.BlockSpec((tm, D), lambda s, t, e, f: (t[s], 0)),
            scratch_shapes=[pltpu.VMEM((tm, D), BF16), pltpu.VMEM((tm, D), F32)]),
        out_shape=jax.ShapeDtypeStruct((T, D), F32),
        compiler_params=_params("arbitrary"), name="moe_experts_ln",
    )(tile, expert, flags, xs, w_gate, w_up, w_down, ln_g.reshape(1, D), ln_b.reshape(1, D))


def _moe(xa, cls, w_gate, w_up, w_down, ln_g, ln_b, alpha, tm=256):
    T = xa.shape[0]
    E = w_gate.shape[0]
    n_cls = N_GROUPS * len(_expert_pairs(E // N_GROUPS))
    rank, cnt = _class_ranks(cls, -(-n_cls // 8) * 8)
    counts = cnt[:n_cls, 0].astype(jnp.int32)
    starts = jnp.cumsum(counts) - counts
    dest = (starts[cls.reshape(T)] + rank.reshape(T)).astype(jnp.int32)
    xs = _dispatch(xa, dest)
    ys = _moe_experts(xs, _moe_steps(counts, E, T // tm, tm), w_gate, w_up, w_down, ln_g, ln_b, alpha, tm)
    return _undispatch(ys, dest)


def _rotary_tables(S):
    half = RET_HD // 2
    inv_freq = ROPE_BASE ** (-jnp.arange(half, dtype=F32) / half)
    ang = jnp.arange(S).astype(F32)[:, None] * inv_freq[None, :]
    cos, sin = jnp.cos(ang), jnp.sin(ang)
    return jnp.concatenate([cos, cos], axis=1), jnp.concatenate([-sin, sin], axis=1)


def kernel(x, mem, ln_in_g, ln_in_b, router_w, router_bias, w_in, ret_gn_g, rwkv_mu, rwkv_w_up, rwkv_w0,
           rwkv_a_up, rwkv_a0, rwkv_g_up, rwkv_k_k, rwkv_k_a, rwkv_r_k, rwkv_ln_g, rwkv_ln_b,
           w_ret_up, w_rwkv_up, w_out, ln1_g, ln1_b, xa_wq, xa_wkv, xa_wo, ln2_g, ln2_b,
           moe_w_gate, moe_w_up, moe_w_down, ln3_g, ln3_b):
    B, S, D = x.shape
    T = B * S
    depth = w_in.shape[0]
    alpha = (2 * depth) ** 0.25
    ret_w = ret_gn_g.shape[-1]
    rwkv_w = rwkv_w0.shape[-1]
    ret_cols = 4 * ret_w
    rwkv_cols = rwkv_mu.shape[-1]
    cos2, sin2 = _rotary_tables(S)
    mem2 = mem.reshape(B * mem.shape[1], D)

    xs = _layer_norm(x.reshape(T, D), ln_in_g, ln_in_b)
    for l in range(depth):
        w_in_l = w_in[l].astype(BF16)
        p_ret = _matmul(xs, w_in_l[:, :ret_cols], 1024, 512, BF16, "in_proj_ret")
        p_rwkv = _matmul(xs, w_in_l[:, ret_cols:ret_cols + rwkv_cols], 1024, rwkv_cols // 2, F32, "in_proj_rwkv")
        p_gate = _matmul(xs, w_in_l[:, ret_cols + rwkv_cols:], 1024, 512, BF16, "in_proj_gate")
        y_ret = _retention(p_ret.reshape(B, S, ret_cols), cos2, sin2, ret_gn_g[l])
        y_rwkv = _rwkv(p_rwkv.reshape(B, S, rwkv_cols), rwkv_mu[l], rwkv_w_up[l], rwkv_w0[l], rwkv_a_up[l],
                       rwkv_a0[l], rwkv_g_up[l], rwkv_k_k[l], rwkv_k_a[l], rwkv_r_k[l].reshape(-1),
                       rwkv_ln_g[l], rwkv_ln_b[l])
        xs = _merge(y_ret.reshape(T, ret_w), y_rwkv.reshape(T, rwkv_w), p_gate, xs,
                    w_ret_up[l].astype(BF16), w_rwkv_up[l].astype(BF16), w_out[l].astype(BF16),
                    ln1_g[l], ln1_b[l], alpha)
        kv = _matmul(mem2, xa_wkv[l].astype(BF16), 512, 512, BF16, "xattn_kv")
        xa, cls = _cross_attention(xs.reshape(B, S, D), kv.reshape(B, -1, 2 * D), xa_wq[l].astype(BF16),
                                   xa_wo[l].astype(BF16), ln2_g[l], ln2_b[l], router_w, router_bias, alpha)
        xs = _moe(xa.reshape(T, D + LANES), cls, moe_w_gate[l].astype(BF16), moe_w_up[l].astype(BF16),
                  moe_w_down[l].astype(BF16), ln3_g[l], ln3_b[l], alpha)
    return xs.reshape(B, S, D)
```

```python
import functools
import math

import jax
import jax.numpy as jnp
from jax import lax
from jax.experimental import pallas as pl
from jax.experimental.pallas import tpu as pltpu

F32 = jnp.float32
BF16 = jnp.bfloat16

LANES = 128
VMEM_LIMIT = 56 * 1024 * 1024

CHUNK = 64
RET_HEADS = 4
RET_HD = 128
ROPE_BASE = 10000.0
RWKV_HD = 64
HD_SHIFT = 6
RWKV_GN_EPS = 64e-5
XA_HEADS = 4
N_GROUPS = 4
TOP_K = 2
LN_EPS = 1e-5
HEADS_PER_GROUP = 4
GROUP_W = HEADS_PER_GROUP * RWKV_HD


def _params(*sem):
    return pltpu.CompilerParams(dimension_semantics=sem, vmem_limit_bytes=VMEM_LIMIT)


def _ln(v, g, b, eps=LN_EPS):
    mu = jnp.mean(v, axis=-1, keepdims=True)
    d = v - mu
    var = jnp.mean(d * d, axis=-1, keepdims=True)
    return d * lax.rsqrt(var + eps) * g + b


def _dot(a, b):
    return jnp.dot(a.astype(BF16), b.astype(BF16), preferred_element_type=F32)


def _dot_nt(a, b):
    return lax.dot_general(a.astype(BF16), b.astype(BF16), (((1,), (1,)), ((), ())),
                           preferred_element_type=F32)


def _dot_tn(a, b):
    return lax.dot_general(a.astype(BF16), b.astype(BF16), (((0,), (0,)), ((), ())),
                           preferred_element_type=F32)


def _ln_kernel(x_ref, g_ref, b_ref, o_ref):
    o_ref[...] = _ln(x_ref[...], g_ref[...], b_ref[...])


def _layer_norm(x, g, b, tm=512):
    T, D = x.shape
    return pl.pallas_call(
        _ln_kernel, grid=(T // tm,),
        in_specs=[pl.BlockSpec((tm, D), lambda i: (i, 0)),
                  pl.BlockSpec((1, D), lambda i: (0, 0)),
                  pl.BlockSpec((1, D), lambda i: (0, 0))],
        out_specs=pl.BlockSpec((tm, D), lambda i: (i, 0)),
        out_shape=jax.ShapeDtypeStruct((T, D), F32),
        compiler_params=_params("parallel"), name="ln_in",
    )(x, g.reshape(1, D), b.reshape(1, D))


def _mm_kernel(a_ref, w_ref, o_ref, abf_ref):
    @pl.when(pl.program_id(1) == 0)
    def _():
        abf_ref[...] = a_ref[...].astype(BF16)

    o_ref[...] = jnp.dot(abf_ref[...], w_ref[...], preferred_element_type=F32).astype(o_ref.dtype)


def _matmul(a, w, tm, tn, out_dtype, name):
    M, K = a.shape
    N = w.shape[1]
    return pl.pallas_call(
        _mm_kernel, grid=(M // tm, N // tn),
        in_specs=[pl.BlockSpec((tm, K), lambda i, j: (i, 0)),
                  pl.BlockSpec((K, tn), lambda i, j: (0, j))],
        out_specs=pl.BlockSpec((tm, tn), lambda i, j: (i, j)),
        out_shape=jax.ShapeDtypeStruct((M, N), out_dtype),
        scratch_shapes=[pltpu.VMEM((tm, K), BF16)],
        compiler_params=_params("parallel", "arbitrary"), name=name,
    )(a, w)


def _ret_kernel(q_ref, k_ref, v_ref, g_ref, cos_ref, sin_ref, gn_ref, o_ref, state_ref, *, n_chunks):
    @pl.when(pl.program_id(1) == 0)
    def _():
        state_ref[...] = jnp.zeros_like(state_ref)

    C, hd = CHUNK, RET_HD
    ti = lax.broadcasted_iota(jnp.int32, (C, C), 0)
    si = lax.broadcasted_iota(jnp.int32, (C, C), 1)
    dist = jnp.abs(ti - si).astype(F32)
    cpos = lax.broadcasted_iota(jnp.int32, (C, 1), 0).astype(F32)
    for h in range(RET_HEADS):
        log_gamma = math.log(1.0 - 2.0 ** (-5.0 - h))
        inner_mask = jnp.exp(log_gamma * dist)
        k_decay = jnp.exp(log_gamma * (C - 1.0 - cpos))
        q_decay = jnp.exp(log_gamma * (cpos + 1.0))
        chunk_decay = math.exp(log_gamma * C)
        hs = slice(h * hd, (h + 1) * hd)
        for c in range(n_chunks):
            rs = slice(c * C, (c + 1) * C)
            cos, sin = cos_ref[rs, :], sin_ref[rs, :]
            q = q_ref[0, rs, hs].astype(F32)
            k = k_ref[0, rs, hs].astype(F32)
            v = v_ref[0, rs, hs]
            qr = q * cos + pltpu.roll(q, hd // 2, axis=1) * sin
            kr = (k * cos + pltpu.roll(k, hd // 2, axis=1) * sin) * (hd ** -0.5)
            state = state_ref[h]
            scores = _dot_nt(qr, kr) * inner_mask
            y = _dot(scores, v) + _dot(qr * q_decay, state)
            state_ref[h] = chunk_decay * state + _dot_tn(kr * k_decay, v)
            mu = jnp.mean(y, axis=-1, keepdims=True)
            d = y - mu
            var = jnp.mean(d * d, axis=-1, keepdims=True)
            yn = d * lax.rsqrt(var + LN_EPS) * gn_ref[:, hs]
            g = g_ref[0, rs, hs].astype(F32)
            o_ref[0, rs, hs] = (g * jax.nn.sigmoid(g) * yn).astype(o_ref.dtype)


def _retention(p_ret, cos2, sin2, gn_g, ts=128):
    B, S, W4 = p_ret.shape
    W = W4 // 4
    blk = lambda c: pl.BlockSpec((1, ts, W), lambda b, n, c=c: (b, n, c))
    return pl.pallas_call(
        functools.partial(_ret_kernel, n_chunks=ts // CHUNK), grid=(B, S // ts),
        in_specs=[blk(0), blk(1), blk(2), blk(3),
                  pl.BlockSpec((ts, RET_HD), lambda b, n: (n, 0)),
                  pl.BlockSpec((ts, RET_HD), lambda b, n: (n, 0)),
                  pl.BlockSpec((1, W), lambda b, n: (0, 0))],
        out_specs=pl.BlockSpec((1, ts, W), lambda b, n: (b, n, 0)),
        out_shape=jax.ShapeDtypeStruct((B, S, W), BF16),
        scratch_shapes=[pltpu.VMEM((RET_HEADS, RET_HD, RET_HD), F32)],
        compiler_params=_params("parallel", "arbitrary"), name="retention",
    )(p_ret, p_ret, p_ret, p_ret, cos2, sin2, gn_g.reshape(1, W))


def _head_sum(x):
    R, W = x.shape
    lo = lax.broadcasted_iota(jnp.int32, (R, LANES), 1) < RWKV_HD
    outs = []
    for j in range(W // LANES):
        xs = x[:, j * LANES:(j + 1) * LANES]
        s_lo = jnp.sum(jnp.where(lo, xs, 0.0), axis=-1, keepdims=True)
        s_hi = jnp.sum(jnp.where(lo, 0.0, xs), axis=-1, keepdims=True)
        outs.append(jnp.where(lo, s_lo, s_hi))
    return jnp.concatenate(outs, axis=1)


def _block_diag(x, head_of_lane):
    parts = [jnp.where(head_of_lane == h, x, 0.0) for h in range(HEADS_PER_GROUP)]
    return jnp.concatenate(parts, axis=0).astype(BF16)


def _packed_mm(xs, ys, head_of_lane):
    bds = [_block_diag(y, head_of_lane) for y in ys]
    return [jnp.dot(x.astype(BF16), bd, preferred_element_type=F32) for x, bd in zip(xs, bds)]


def _unit_lower_inverse(a_list, t_idx, s_idx, head_of_lane):
    mm = functools.partial(_packed_mm, head_of_lane=head_of_lane)
    add = lambda xs, ys: [x + y for x, y in zip(xs, ys)]
    lower = s_idx < t_idx
    same4 = (t_idx >> 2) == (s_idx >> 2)
    same16 = (t_idx >> 4) == (s_idx >> 4)
    m1 = lower & same4
    m2 = lower & same16 & jnp.logical_not(same4)
    m3 = lower & jnp.logical_not(same16)
    eye = jnp.where(t_idx == s_idx, 1.0, 0.0)
    a1 = [jnp.where(m1, a, 0.0) for a in a_list]
    a1sq = mm(a1, a1)
    a1cu = mm(a1, a1sq)
    inv = [eye + p + q + r for p, q, r in zip(a1, a1sq, a1cu)]
    for m in (m2, m3):
        n = mm(inv, [jnp.where(m, a, 0.0) for a in a_list])
        x = add(inv, mm(mm(n, n), inv))
        inv = add(x, mm(n, x))
    return inv


def _rwkv_kernel(*refs):
    state_ref = refs[-1]

    @pl.when(pl.program_id(1) == 0)
    def _():
        for ref in refs[-5:]:
            ref[...] = jnp.zeros_like(ref)

    seqs = [_rwkv_prepare(bi, *refs) for bi in range(state_ref.shape[0])]
    _rwkv_chains(seqs, state_ref)
    for bi, seq in enumerate(seqs):
        _rwkv_finish(bi, seq, *refs)


def _rwkv_prepare(bi, pr_ref, pk_ref, pv_ref, pl_ref, mur_ref, muk_ref, muv_ref, mul_ref,
                  wup_ref, aup_ref, gup_ref, w0_ref, a0_ref, kk_ref, ka_ref, rk_ref, lng_ref, lnb_ref,
                  o_ref, carry_r, carry_k, carry_v, carry_l, state_ref):
    C = CHUNK

    def shifted(p_ref, mu_ref, carry):
        p = p_ref[bi]
        row = lax.broadcasted_iota(jnp.int32, p.shape, 0)
        prev = jnp.where(row == 0, carry[bi, 0:1, :], pltpu.roll(p, 1, axis=0))
        carry[bi, 0:1, :] = p[C - 1:C, :]
        return p + mu_ref[...] * (prev - p)

    r = shifted(pr_ref, mur_ref, carry_r)
    k = shifted(pk_ref, muk_ref, carry_k)
    v = shifted(pv_ref, muv_ref, carry_v)
    lo = shifted(pl_ref, mul_ref, carry_l)
    dwa = lo[:, :LANES]
    dg = lo[:, LANES:]

    z = w0_ref[...] + _dot(jnp.tanh(dwa), wup_ref[...])
    logw = -math.exp(-0.5) * jax.nn.sigmoid(z)
    a = jax.nn.sigmoid(a0_ref[...] + _dot(dwa, aup_ref[...]))
    g = _dot(jax.nn.sigmoid(dg), gup_ref[...])

    kk = k * kk_ref[...]
    kk = kk / jnp.maximum(jnp.sqrt(_head_sum(kk * kk)), 1e-12)
    k = k * (1.0 + (a - 1.0) * ka_ref[...])

    ti = lax.broadcasted_iota(jnp.int32, (C, C), 0)
    si = lax.broadcasted_iota(jnp.int32, (C, C), 1)
    tril = jnp.where(si <= ti, 1.0, 0.0).astype(BF16)
    lw_hi = logw.astype(BF16)
    lw_lo = (logw - lw_hi.astype(F32)).astype(BF16)
    cum = (jnp.dot(tril, lw_hi, preferred_element_type=F32)
           + jnp.dot(tril, lw_lo, preferred_element_type=F32))
    cum_last = cum[C - 1:C, :]
    e_neg = jnp.exp(-cum)
    abar = -kk * jnp.exp(cum - logw)
    rbar = r * jnp.exp(cum)
    kka = kk * a
    btil = kka * e_neg
    ktil = k * e_neg
    e_end = jnp.exp(cum_last - cum)
    bend = kka * e_end
    kend = k * e_end
    gamma_end = jnp.exp(cum_last)
    return dict(r=r, k=k, v=v, g=g, abar=abar, rbar=rbar, btil=btil, ktil=ktil, bend=bend, kend=kend,
                gamma_end=gamma_end)


def _rwkv_chains(seqs, state_ref):
    C = CHUNK
    n_groups = state_ref.shape[1]
    t_idx = lax.broadcasted_iota(jnp.int32, (C, GROUP_W), 0)
    lane = lax.broadcasted_iota(jnp.int32, (C, GROUP_W), 1)
    s_idx = lane & (RWKV_HD - 1)
    head_of_lane = lane >> HD_SHIFT
    strict = s_idx < t_idx
    incl = s_idx <= t_idx
    bd_r = lax.broadcasted_iota(jnp.int32, (GROUP_W, GROUP_W), 0) >> HD_SHIFT
    bd_c = lax.broadcasted_iota(jnp.int32, (GROUP_W, GROUP_W), 1) >> HD_SHIFT
    on_diag = bd_r == bd_c

    chains = [(bi, gi) for bi in range(len(seqs)) for gi in range(n_groups)]
    part = lambda name: [seqs[bi][name][:, gi * GROUP_W:(gi + 1) * GROUP_W] for bi, gi in chains]
    mm = functools.partial(_packed_mm, head_of_lane=head_of_lane)
    v_g = part("v")
    lhs = [jnp.concatenate([p, q], axis=0) for p, q in zip(part("abar"), part("rbar"))]
    rhs = [jnp.concatenate([_block_diag(p, head_of_lane), _block_diag(q, head_of_lane)], axis=0)
           for p, q in zip(part("btil"), part("ktil"))]
    gram = [_dot_nt(p, q) for p, q in zip(lhs, rhs)]
    a_ab = [jnp.where(strict, gm[:C, :GROUP_W], 0.0) for gm in gram]
    a_ak = [jnp.where(strict, gm[:C, GROUP_W:], 0.0) for gm in gram]
    m_rb = [jnp.where(incl, gm[C:, :GROUP_W], 0.0) for gm in gram]
    m_rk = [jnp.where(incl, gm[C:, GROUP_W:], 0.0) for gm in gram]
    inv = _unit_lower_inverse(a_ab, t_idx, s_idx, head_of_lane)
    st = [state_ref[bi, gi] for bi, gi in chains]
    from_state = [_dot_nt(p, q) for p, q in zip(lhs, st)]
    bd_v = [_block_diag(p, head_of_lane) for p in v_g]
    local = [jnp.dot(p.astype(BF16), bd, preferred_element_type=F32) for p, bd in zip(a_ak + m_rk, bd_v + bd_v)]
    n = len(chains)
    u = mm(inv, [fs[:C] + lc for fs, lc in zip(from_state, local[:n])])
    via_u = mm(m_rb, u)
    upd = [_dot_tn(jnp.concatenate([p, q], axis=0), jnp.concatenate([b, k], axis=0))
           for p, q, b, k in zip(u, v_g, part("bend"), part("kend"))]
    for (bi, gi), s_old, up, ge in zip(chains, st, upd, part("gamma_end")):
        state_ref[bi, gi] = s_old * ge + jnp.where(on_diag, up, 0.0)
    ys = [fs[C:] + p + q for fs, p, q in zip(from_state, via_u, local[n:])]
    for bi, seq in enumerate(seqs):
        seq["y"] = jnp.concatenate(ys[bi * n_groups:(bi + 1) * n_groups], axis=1)


def _rwkv_finish(bi, seq, pr_ref, pk_ref, pv_ref, pl_ref, mur_ref, muk_ref, muv_ref, mul_ref,
                 wup_ref, aup_ref, gup_ref, w0_ref, a0_ref, kk_ref, ka_ref, rk_ref, lng_ref, lnb_ref,
                 o_ref, carry_r, carry_k, carry_v, carry_l, state_ref):
    y, r, k, v, g = seq["y"], seq["r"], seq["k"], seq["v"], seq["g"]
    inv_hd = 1.0 / RWKV_HD
    mu = _head_sum(y) * inv_hd
    d = y - mu
    var = _head_sum(d * d) * inv_hd
    yn = d * lax.rsqrt(var + RWKV_GN_EPS) * lng_ref[...] + lnb_ref[...]
    bonus = _head_sum(r * k * rk_ref[...]) * v
    o_ref[bi] = ((yn + bonus) * g).astype(o_ref.dtype)


def _rwkv(p_rwkv, mu, w_up, w0, a_up, a0, g_up, k_k, k_a, r_k, ln_g, ln_b, nb=4):
    B, S, _ = p_rwkv.shape
    W = w0.shape[-1]
    C = CHUNK
    wup_pad = jnp.concatenate([w_up, jnp.zeros_like(a_up)], axis=0).astype(BF16)
    aup_pad = jnp.concatenate([jnp.zeros_like(w_up), a_up], axis=0).astype(BF16)
    row = lambda t: t.reshape(1, -1)
    vec = lambda n: pl.BlockSpec((1, n), lambda b, t: (0, 0))
    mat = lambda a: pl.BlockSpec(a.shape, lambda b, t: (0, 0))
    cblk = lambda c: pl.BlockSpec((nb, C, W), lambda b, t, c=c: (b, t, c))
    lw = p_rwkv.shape[-1] - 3 * W
    n_groups = W // GROUP_W
    gup = g_up.astype(BF16)
    return pl.pallas_call(
        _rwkv_kernel, grid=(B // nb, S // C),
        in_specs=[cblk(0), cblk(1), cblk(2),
                  pl.BlockSpec((nb, C, lw), lambda b, t: (b, t, 3 * W // lw)),
                  vec(W), vec(W), vec(W), vec(lw),
                  mat(wup_pad), mat(aup_pad), mat(gup),
                  vec(W), vec(W), vec(W), vec(W), vec(W), vec(W), vec(W)],
        out_specs=pl.BlockSpec((nb, C, W), lambda b, t: (b, t, 0)),
        out_shape=jax.ShapeDtypeStruct((B, S, W), BF16),
        scratch_shapes=[pltpu.VMEM((nb, 8, W), F32), pltpu.VMEM((nb, 8, W), F32), pltpu.VMEM((nb, 8, W), F32),
                        pltpu.VMEM((nb, 8, lw), F32),
                        pltpu.VMEM((nb, n_groups, GROUP_W, GROUP_W), F32)],
        compiler_params=_params("parallel", "arbitrary"), name="rwkv7",
    )(p_rwkv, p_rwkv, p_rwkv, p_rwkv,
      row(mu[:W]), row(mu[W:2 * W]), row(mu[2 * W:3 * W]), row(mu[3 * W:]),
      wup_pad, aup_pad, gup,
      row(w0), row(a0), row(k_k), row(k_a), row(r_k), row(ln_g), row(ln_b))


def _merge_kernel(yr_ref, yw_ref, pg_ref, x_ref, wr_ref, ww_ref, wo_ref, g_ref, b_ref, o_ref, *, alpha):
    D = x_ref.shape[-1]
    gate = jax.nn.sigmoid(pg_ref[...].astype(F32))
    merged = (gate[:, :D] * jnp.dot(yr_ref[...], wr_ref[...], preferred_element_type=F32)
              + gate[:, D:] * jnp.dot(yw_ref[...], ww_ref[...], preferred_element_type=F32))
    h = jnp.dot(merged.astype(BF16), wo_ref[...], preferred_element_type=F32)
    o_ref[...] = _ln(alpha * x_ref[...] + h, g_ref[...], b_ref[...])


def _merge(y_ret, y_rwkv, p_gate, x, w_ret_up, w_rwkv_up, w_out, ln_g, ln_b, alpha, tm=512):
    T, D = x.shape
    W = y_ret.shape[1]
    full = lambda a: pl.BlockSpec(a.shape, lambda i: (0, 0))
    rowblk = lambda n: pl.BlockSpec((tm, n), lambda i: (i, 0))
    return pl.pallas_call(
        functools.partial(_merge_kernel, alpha=alpha), grid=(T // tm,),
        in_specs=[rowblk(W), rowblk(W), rowblk(2 * D), rowblk(D),
                  full(w_ret_up), full(w_rwkv_up), full(w_out),
                  pl.BlockSpec((1, D), lambda i: (0, 0)), pl.BlockSpec((1, D), lambda i: (0, 0))],
        out_specs=rowblk(D),
        out_shape=jax.ShapeDtypeStruct((T, D), F32),
        compiler_params=_params("parallel"), name="merge_out_ln",
    )(y_ret, y_rwkv, p_gate, x, w_ret_up, w_rwkv_up, w_out, ln_g.reshape(1, D), ln_b.reshape(1, D))


def _xattn_kernel(x_ref, kv_ref, wq_ref, wo_ref, g_ref, b_ref, rwt_ref, rb_ref, o_ref, cls_ref, *, alpha):
    x = x_ref[0]
    D = x.shape[-1]
    hd = D // XA_HEADS
    q = jnp.dot(x.astype(BF16), wq_ref[...], preferred_element_type=F32)
    outs = []
    for h in range(XA_HEADS):
        k_h = kv_ref[0, :, h * hd:(h + 1) * hd]
        v_h = kv_ref[0, :, D + h * hd:D + (h + 1) * hd]
        s = _dot_nt(q[:, h * hd:(h + 1) * hd], k_h) * (hd ** -0.5)
        s = s - jnp.max(s, axis=-1, keepdims=True)
        e = jnp.exp(s)
        probs = e / jnp.sum(e, axis=-1, keepdims=True)
        outs.append(jnp.dot(probs.astype(BF16), v_h, preferred_element_type=F32))
    o = jnp.concatenate(outs, axis=1)
    h_out = jnp.dot(o.astype(BF16), wo_ref[...], preferred_element_type=F32)
    x2 = _ln(alpha * x + h_out, g_ref[...], b_ref[...])
    comb, cls = _route(x2, rwt_ref[...], rb_ref[...])
    E, tm = comb.shape
    comb_pad = jnp.concatenate([comb, jnp.zeros((LANES - E, tm), F32)], axis=0)
    o_ref[0, :, :D] = x2
    o_ref[0, :, D:] = comb_pad.T
    cls_ref[0] = cls


def _cross_attention(x, kv, wq, wo, ln_g, ln_b, router_w, router_bias, alpha, tm=512):
    B, S, D = x.shape
    M = kv.shape[1]
    E = router_w.shape[1]
    full = lambda a: pl.BlockSpec(a.shape, lambda b, i: (0, 0))
    n_t = S // tm
    return pl.pallas_call(
        functools.partial(_xattn_kernel, alpha=alpha), grid=(B, n_t),
        in_specs=[pl.BlockSpec((1, tm, D), lambda b, i: (b, i, 0)),
                  pl.BlockSpec((1, M, 2 * D), lambda b, i: (b, 0, 0)),
                  full(wq), full(wo),
                  pl.BlockSpec((1, D), lambda b, i: (0, 0)), pl.BlockSpec((1, D), lambda b, i: (0, 0)),
                  pl.BlockSpec((D, LANES), lambda b, i: (0, 0)), pl.BlockSpec((E, 1), lambda b, i: (0, 0))],
        out_specs=[pl.BlockSpec((1, tm, D + LANES), lambda b, i: (b, i, 0)),
                   pl.BlockSpec((1, 1, tm), lambda b, i: (b * n_t + i, 0, 0))],
        out_shape=[jax.ShapeDtypeStruct((B, S, D + LANES), F32),
                   jax.ShapeDtypeStruct((B * n_t, 1, tm), jnp.int32)],
        compiler_params=_params("parallel", "parallel"), name="xattn_ln_route",
    )(x, kv, wq, wo, ln_g.reshape(1, D), ln_b.reshape(1, D),
      jnp.pad(router_w, ((0, 0), (0, LANES - E))), router_bias.reshape(E, 1))


def _expert_pairs(per_group):
    todo = [(i, j) for i in range(per_group) for j in range(i + 1, per_group)]
    order = [todo.pop(0)]
    while todo:
        nxt = next((p for p in todo if set(p) & set(order[-1])), todo[0])
        todo.remove(nxt)
        order.append(nxt)
    return order


def _split_bf16(v):
    hi = v.astype(BF16)
    return hi, (v - hi.astype(F32)).astype(BF16)


def _route(x, w, bias):
    E = bias.shape[0]
    per_group = E // N_GROUPS
    x_hi, x_lo = _split_bf16(x)
    w_hi, w_lo = _split_bf16(w)
    logits = (jnp.dot(x_hi, w_hi, preferred_element_type=F32) + jnp.dot(x_hi, w_lo, preferred_element_type=F32)
              + jnp.dot(x_lo, w_hi, preferred_element_type=F32)).T[:E]
    aff = jax.nn.sigmoid(logits)
    choice = aff + bias
    rows = [choice[e:e + 1, :] for e in range(E)]
    scores = []
    for gidx in range(N_GROUPS):
        members = rows[gidx * per_group:(gidx + 1) * per_group]
        best = None
        for i in range(per_group):
            for j in range(i + 1, per_group):
                pair = members[i] + members[j]
                best = pair if best is None else jnp.maximum(best, pair)
        scores.append(best)
    top = scores[0]
    for s in scores[1:]:
        top = jnp.maximum(top, s)
    taken = jnp.zeros_like(top, dtype=jnp.bool_)
    in_best = []
    for s in scores:
        hit = jnp.logical_and(s == top, jnp.logical_not(taken))
        in_best.append(hit)
        taken = jnp.logical_or(taken, hit)
    sel_w = []
    cls = jnp.zeros(top.shape, jnp.int32)
    pairs = _expert_pairs(per_group)
    for gidx in range(N_GROUPS):
        members = rows[gidx * per_group:(gidx + 1) * per_group]
        chosen = []
        for i in range(per_group):
            rank = jnp.zeros_like(top)
            for j in range(per_group):
                if j == i:
                    continue
                ahead = (members[j] > members[i]) if j > i else (members[j] >= members[i])
                rank = rank + jnp.where(ahead, 1.0, 0.0)
            chosen.append(jnp.logical_and(in_best[gidx], rank < TOP_K))
            e = gidx * per_group + i
            sel_w.append(jnp.where(chosen[i], aff[e:e + 1, :], 0.0))
        for p, (i, j) in enumerate(pairs):
            cls = jnp.where(jnp.logical_and(chosen[i], chosen[j]), gidx * len(pairs) + p, cls)
    total = sel_w[0]
    for w in sel_w[1:]:
        total = total + w
    return jnp.concatenate(sel_w, axis=0) / total, cls


def _rank_kernel(cls_ref, rank_ref, cnt_ref, base_ref):
    @pl.when(pl.program_id(0) == 0)
    def _():
        base_ref[...] = jnp.zeros_like(base_ref)

    cls = cls_ref[0]
    tm = cls.shape[-1]
    n_cls = base_ref.shape[0]
    onehot = lax.broadcasted_iota(jnp.int32, (n_cls, tm), 0) == cls
    earlier = (lax.broadcasted_iota(jnp.int32, (tm, tm), 0)
               < lax.broadcasted_iota(jnp.int32, (tm, tm), 1))
    ones = jnp.where(onehot, 1.0, 0.0)
    before = jnp.dot(ones.astype(BF16), jnp.where(earlier, 1.0, 0.0).astype(BF16),
                     preferred_element_type=F32)
    base = base_ref[...]
    rank = jnp.sum(jnp.where(onehot, base + before, 0.0), axis=0, keepdims=True)
    rank_ref[0] = rank.astype(jnp.int32)
    total = base + jnp.sum(ones, axis=1, keepdims=True)
    base_ref[...] = total
    cnt_ref[...] = jnp.broadcast_to(total, cnt_ref.shape)


def _class_ranks(cls, n_cls_pad):
    n_t, _, tm = cls.shape
    return pl.pallas_call(
        _rank_kernel, grid=(n_t,),
        in_specs=[pl.BlockSpec((1, 1, tm), lambda i: (i, 0, 0))],
        out_specs=[pl.BlockSpec((1, 1, tm), lambda i: (i, 0, 0)),
                   pl.BlockSpec((n_cls_pad, LANES), lambda i: (0, 0))],
        out_shape=[jax.ShapeDtypeStruct((n_t, 1, tm), jnp.int32),
                   jax.ShapeDtypeStruct((n_cls_pad, LANES), F32)],
        scratch_shapes=[pltpu.VMEM((n_cls_pad, 1), F32)],
        compiler_params=_params("arbitrary"), name="moe_class_rank",
    )(cls)


def _row_copy(src, src_row, dst, dst_row, sem):
    return pltpu.make_async_copy(src.at[pl.ds(src_row, 1)], dst.at[pl.ds(dst_row, 1)], sem)


def _dispatch_kernel(dest_ref, x_ref, o_hbm, sem):
    tm = x_ref.shape[0]
    base = pl.program_id(0) * tm

    def start(r, carry):
        _row_copy(x_ref, r, o_hbm, dest_ref[base + r], sem).start()
        return carry

    def wait(r, carry):
        _row_copy(x_ref, r, o_hbm, dest_ref[base + r], sem).wait()
        return carry

    lax.fori_loop(0, tm, start, 0, unroll=8)
    lax.fori_loop(0, tm, wait, 0, unroll=8)


def _dispatch(xa, dest, tm=512):
    T, W = xa.shape
    return pl.pallas_call(
        _dispatch_kernel,
        grid_spec=pltpu.PrefetchScalarGridSpec(
            num_scalar_prefetch=1, grid=(T // tm,),
            in_specs=[pl.BlockSpec((tm, W), lambda i, dest: (i, 0))],
            out_specs=pl.BlockSpec(memory_space=pl.ANY),
            scratch_shapes=[pltpu.SemaphoreType.DMA(())]),
        out_shape=jax.ShapeDtypeStruct((T, W), xa.dtype),
        compiler_params=_params("arbitrary"), name="moe_dispatch",
    )(dest, xa)


def _undispatch_kernel(dest_ref, y_hbm, o_ref, sem):
    tm = o_ref.shape[0]
    base = pl.program_id(0) * tm

    def start(r, carry):
        _row_copy(y_hbm, dest_ref[base + r], o_ref, r, sem).start()
        return carry

    def wait(r, carry):
        _row_copy(y_hbm, dest_ref[base + r], o_ref, r, sem).wait()
        return carry

    lax.fori_loop(0, tm, start, 0, unroll=8)
    lax.fori_loop(0, tm, wait, 0, unroll=8)


def _undispatch(ys, dest, tm=512):
    T, D = ys.shape
    return pl.pallas_call(
        _undispatch_kernel,
        grid_spec=pltpu.PrefetchScalarGridSpec(
            num_scalar_prefetch=1, grid=(T // tm,),
            in_specs=[pl.BlockSpec(memory_space=pl.ANY)],
            out_specs=pl.BlockSpec((tm, D), lambda i, dest: (i, 0)),
            scratch_shapes=[pltpu.SemaphoreType.DMA(())]),
        out_shape=jax.ShapeDtypeStruct((T, D), ys.dtype),
        compiler_params=_params("arbitrary"), name="moe_undispatch",
    )(dest, ys)


STEP_FIRST, STEP_LAST, STEP_VALID = 1, 2, 4


def _moe_kernel(tile_ref, exp_ref, flag_ref, xs_ref, wg_ref, wu_ref, wd_ref, g_ref, b_ref, o_ref,
                xbf_ref, acc_ref, *, alpha):
    step = pl.program_id(0)
    flags = flag_ref[step]
    e = exp_ref[step]
    D = o_ref.shape[-1]

    @pl.when((flags & STEP_FIRST) != 0)
    def _():
        xbf_ref[...] = xs_ref[:, :D].astype(BF16)
        acc_ref[...] = jnp.zeros_like(acc_ref)

    @pl.when((flags & STEP_VALID) != 0)
    def _():
        xb = xbf_ref[...]
        hg = jnp.dot(xb, wg_ref[0], preferred_element_type=F32)
        hu = jnp.dot(xb, wu_ref[0], preferred_element_type=F32)
        h = hg * jax.nn.sigmoid(hg) * hu
        comb = xs_ref[:, D:]
        lane = lax.broadcasted_iota(jnp.int32, comb.shape, 1)
        w_e = jnp.sum(jnp.where(lane == e, comb, 0.0), axis=-1, keepdims=True)
        acc_ref[...] += w_e * jnp.dot(h.astype(BF16), wd_ref[0], preferred_element_type=F32)

    @pl.when((flags & STEP_LAST) != 0)
    def _():
        o_ref[...] = _ln(alpha * xs_ref[:, :D] + acc_ref[...], g_ref[...], b_ref[...])


def _moe_steps(counts, n_experts, n_tiles, tm):
    n_cls = counts.shape[0]
    per_group = n_experts // N_GROUPS
    pairs = _expert_pairs(per_group)
    uses = [[0.0] * n_experts for _ in range(n_cls)]
    for c in range(n_cls):
        gidx, (i, j) = c // len(pairs), pairs[c % len(pairs)]
        uses[c][gidx * per_group + i] = 1.0
        uses[c][gidx * per_group + j] = 1.0
    uses = jnp.array(uses, F32)
    ends = jnp.cumsum(counts)
    starts = ends - counts
    lo = (jnp.arange(n_tiles, dtype=jnp.int32) * tm)[:, None]
    present = jnp.logical_and(starts[None, :] < lo + tm, ends[None, :] > lo)
    need = jnp.dot(present.astype(F32), uses) > 0
    odd = (jnp.arange(n_tiles, dtype=jnp.int32) % 2 == 1)[:, None]
    need = jnp.where(odd, need[:, ::-1], need).reshape(-1)
    n_steps = min(n_tiles * n_experts, TOP_K * (n_tiles + n_cls - 1))
    n_valid = jnp.sum(need.astype(jnp.int32))
    idx = jnp.nonzero(need, size=n_steps, fill_value=0)[0].astype(jnp.int32)
    k = jnp.arange(n_steps, dtype=jnp.int32)
    valid = k < n_valid
    idx = jnp.where(valid, idx, idx[n_valid - 1])
    tile, expert = idx // n_experts, idx % n_experts
    expert = jnp.where(tile % 2 == 1, n_experts - 1 - expert, expert)
    prev_tile = jnp.concatenate([jnp.full((1,), -1, jnp.int32), tile[:-1]])
    next_tile = jnp.concatenate([tile[1:], jnp.full((1,), -1, jnp.int32)])
    first = jnp.logical_and(valid, tile != prev_tile)
    last = jnp.logical_and(valid, jnp.logical_or(tile != next_tile, k == n_valid - 1))
    flags = (first * STEP_FIRST + last * STEP_LAST + valid * STEP_VALID).astype(jnp.int32)
    return tile, expert, flags


def _moe_experts(xs, steps, w_gate, w_up, w_down, ln_g, ln_b, alpha, tm):
    T, W = xs.shape
    D = W - LANES
    E, _, F = w_gate.shape
    tile, expert, flags = steps
    return pl.pallas_call(
        functools.partial(_moe_kernel, alpha=alpha),
        grid_spec=pltpu.PrefetchScalarGridSpec(
            num_scalar_prefetch=3, grid=(tile.shape[0],),
            in_specs=[pl.BlockSpec((tm, W), lambda s, t, e, f: (t[s], 0)),
                      pl.BlockSpec((1, D, F), lambda s, t, e, f: (e[s], 0, 0)),
                      pl.BlockSpec((1, D, F), lambda s, t, e, f: (e[s], 0, 0)),
                      pl.BlockSpec((1, F, D), lambda s, t, e, f: (e[s], 0, 0)),
                      pl.BlockSpec((1, D), lambda s, t, e, f: (0, 0)),
                      pl.BlockSpec((1, D), lambda s, t, e, f: (0, 0))],
            out_specs=pl.BlockSpec((tm, D), lambda s, t, e, f: (t[s], 0)),
            scratch_shapes=[pltpu.VMEM((tm, D), BF16), pltpu.VMEM((tm, D), F32)]),
        out_shape=jax.ShapeDtypeStruct((T, D), F32),
        compiler_params=_params("arbitrary"), name="moe_experts_ln",
    )(tile, expert, flags, xs, w_gate, w_up, w_down, ln_g.reshape(1, D), ln_b.reshape(1, D))


def _moe(xa, cls, w_gate, w_up, w_down, ln_g, ln_b, alpha, tm=512):
    T = xa.shape[0]
    E = w_gate.shape[0]
    n_cls = N_GROUPS * len(_expert_pairs(E // N_GROUPS))
    rank, cnt = _class_ranks(cls, -(-n_cls // 8) * 8)
    counts = cnt[:n_cls, 0].astype(jnp.int32)
    starts = jnp.cumsum(counts) - counts
    dest = (starts[cls.reshape(T)] + rank.reshape(T)).astype(jnp.int32)
    xs = _dispatch(xa, dest)
    ys = _moe_experts(xs, _moe_steps(counts, E, T // tm, tm), w_gate, w_up, w_down, ln_g, ln_b, alpha, tm)
    return _undispatch(ys, dest)


def _rotary_tables(S):
    half = RET_HD // 2
    inv_freq = ROPE_BASE ** (-jnp.arange(half, dtype=F32) / half)
    ang = jnp.arange(S).astype(F32)[:, None] * inv_freq[None, :]
    cos, sin = jnp.cos(ang), jnp.sin(ang)
    return jnp.concatenate([cos, cos], axis=1), jnp.concatenate([-sin, sin], axis=1)


def kernel(x, mem, ln_in_g, ln_in_b, router_w, router_bias, w_in, ret_gn_g, rwkv_mu, rwkv_w_up, rwkv_w0,
           rwkv_a_up, rwkv_a0, rwkv_g_up, rwkv_k_k, rwkv_k_a, rwkv_r_k, rwkv_ln_g, rwkv_ln_b,
           w_ret_up, w_rwkv_up, w_out, ln1_g, ln1_b, xa_wq, xa_wkv, xa_wo, ln2_g, ln2_b,
           moe_w_gate, moe_w_up, moe_w_down, ln3_g, ln3_b):
    B, S, D = x.shape
    T = B * S
    depth = w_in.shape[0]
    alpha = (2 * depth) ** 0.25
    ret_w = ret_gn_g.shape[-1]
    rwkv_w = rwkv_w0.shape[-1]
    ret_cols = 4 * ret_w
    rwkv_cols = rwkv_mu.shape[-1]
    cos2, sin2 = _rotary_tables(S)
    mem2 = mem.reshape(B * mem.shape[1], D)

    xs = _layer_norm(x.reshape(T, D), ln_in_g, ln_in_b)
    for l in range(depth):
        w_in_l = w_in[l].astype(BF16)
        p_ret = _matmul(xs, w_in_l[:, :ret_cols], 1024, 512, BF16, "in_proj_ret")
        p_rwkv = _matmul(xs, w_in_l[:, ret_cols:ret_cols + rwkv_cols], 1024, rwkv_cols // 2, F32, "in_proj_rwkv")
        p_gate = _matmul(xs, w_in_l[:, ret_cols + rwkv_cols:], 1024, 512, BF16, "in_proj_gate")
        y_ret = _retention(p_ret.reshape(B, S, ret_cols), cos2, sin2, ret_gn_g[l])
        y_rwkv = _rwkv(p_rwkv.reshape(B, S, rwkv_cols), rwkv_mu[l], rwkv_w_up[l], rwkv_w0[l], rwkv_a_up[l],
                       rwkv_a0[l], rwkv_g_up[l], rwkv_k_k[l], rwkv_k_a[l], rwkv_r_k[l].reshape(-1),
                       rwkv_ln_g[l], rwkv_ln_b[l])
        xs = _merge(y_ret.reshape(T, ret_w), y_rwkv.reshape(T, rwkv_w), p_gate, xs,
                    w_ret_up[l].astype(BF16), w_rwkv_up[l].astype(BF16), w_out[l].astype(BF16),
                    ln1_g[l], ln1_b[l], alpha)
        kv = _matmul(mem2, xa_wkv[l].astype(BF16), 512, 512, BF16, "xattn_kv")
        xa, cls = _cross_attention(xs.reshape(B, S, D), kv.reshape(B, -1, 2 * D), xa_wq[l].astype(BF16),
                                   xa_wo[l].astype(BF16), ln2_g[l], ln2_b[l], router_w, router_bias, alpha)
        xs = _moe(xa.reshape(T, D + LANES), cls, moe_w_gate[l].astype(BF16), moe_w_up[l].astype(BF16),
                  moe_w_down[l].astype(BF16), ln3_g[l], ln3_b[l], alpha)
    return xs.reshape(B, S, D)
```

```python
import functools
import math

import jax
import jax.numpy as jnp
from jax import lax
from jax.experimental import pallas as pl
from jax.experimental.pallas import tpu as pltpu

F32 = jnp.float32
BF16 = jnp.bfloat16

LANES = 128
VMEM_LIMIT = 56 * 1024 * 1024

CHUNK = 64
RET_HEADS = 4
RET_HD = 128
ROPE_BASE = 10000.0
RWKV_HD = 64
HD_SHIFT = 6
RWKV_GN_EPS = 64e-5
XA_HEADS = 4
N_GROUPS = 4
TOP_K = 2
LN_EPS = 1e-5
IN_PROJ_TN = 1024
HEADS_PER_GROUP = 4
GROUP_W = HEADS_PER_GROUP * RWKV_HD


def _params(*sem):
    return pltpu.CompilerParams(dimension_semantics=sem, vmem_limit_bytes=VMEM_LIMIT)


def _ln(v, g, b, eps=LN_EPS):
    mu = jnp.mean(v, axis=-1, keepdims=True)
    d = v - mu
    var = jnp.mean(d * d, axis=-1, keepdims=True)
    return d * lax.rsqrt(var + eps) * g + b


def _dot(a, b):
    return jnp.dot(a.astype(BF16), b.astype(BF16), preferred_element_type=F32)


def _dot_nt(a, b):
    return lax.dot_general(a.astype(BF16), b.astype(BF16), (((1,), (1,)), ((), ())),
                           preferred_element_type=F32)


def _dot_tn(a, b):
    return lax.dot_general(a.astype(BF16), b.astype(BF16), (((0,), (0,)), ((), ())),
                           preferred_element_type=F32)


def _ln_kernel(x_ref, g_ref, b_ref, o_ref):
    o_ref[...] = _ln(x_ref[...], g_ref[...], b_ref[...])


def _layer_norm(x, g, b, tm=512):
    T, D = x.shape
    return pl.pallas_call(
        _ln_kernel, grid=(T // tm,),
        in_specs=[pl.BlockSpec((tm, D), lambda i: (i, 0)),
                  pl.BlockSpec((1, D), lambda i: (0, 0)),
                  pl.BlockSpec((1, D), lambda i: (0, 0))],
        out_specs=pl.BlockSpec((tm, D), lambda i: (i, 0)),
        out_shape=jax.ShapeDtypeStruct((T, D), F32),
        compiler_params=_params("parallel"), name="ln_in",
    )(x, g.reshape(1, D), b.reshape(1, D))


def _mm_kernel(a_ref, w_ref, o_ref, abf_ref):
    @pl.when(pl.program_id(1) == 0)
    def _():
        abf_ref[...] = a_ref[...].astype(BF16)

    o_ref[...] = jnp.dot(abf_ref[...], w_ref[...], preferred_element_type=F32).astype(o_ref.dtype)


def _matmul(a, w, tm, tn, out_dtype, name):
    M, K = a.shape
    N = w.shape[1]
    return pl.pallas_call(
        _mm_kernel, grid=(M // tm, N // tn),
        in_specs=[pl.BlockSpec((tm, K), lambda i, j: (i, 0)),
                  pl.BlockSpec((K, tn), lambda i, j: (0, j))],
        out_specs=pl.BlockSpec((tm, tn), lambda i, j: (i, j)),
        out_shape=jax.ShapeDtypeStruct((M, N), out_dtype),
        scratch_shapes=[pltpu.VMEM((tm, K), BF16)],
        compiler_params=_params("parallel", "arbitrary"), name=name,
    )(a, w)


def _ret_kernel(q_ref, k_ref, v_ref, g_ref, cos_ref, sin_ref, gn_ref, o_ref, state_ref, *, n_chunks):
    @pl.when(pl.program_id(1) == 0)
    def _():
        state_ref[...] = jnp.zeros_like(state_ref)

    C, hd = CHUNK, RET_HD
    ti = lax.broadcasted_iota(jnp.int32, (C, C), 0)
    si = lax.broadcasted_iota(jnp.int32, (C, C), 1)
    dist = jnp.abs(ti - si).astype(F32)
    cpos = lax.broadcasted_iota(jnp.int32, (C, 1), 0).astype(F32)
    log_gamma = [math.log(1.0 - 2.0 ** (-5.0 - h)) for h in range(RET_HEADS)]
    inner_mask = [jnp.exp(lg * dist) for lg in log_gamma]
    k_decay = [jnp.exp(lg * (C - 1.0 - cpos)) for lg in log_gamma]
    q_decay = [jnp.exp(lg * (cpos + 1.0)) for lg in log_gamma]
    pairs = [(c, h) for c in range(n_chunks) for h in range(RET_HEADS)]
    rows = lambda c: slice(c * C, (c + 1) * C)
    cols = lambda h: slice(h * hd, (h + 1) * hd)

    def rotated(ref, c, h):
        t = ref[0, rows(c), cols(h)].astype(F32)
        return t * cos_ref[rows(c), :] + pltpu.roll(t, hd // 2, axis=1) * sin_ref[rows(c), :]

    qr = [rotated(q_ref, c, h) for c, h in pairs]
    kr = [rotated(k_ref, c, h) * (hd ** -0.5) for c, h in pairs]
    v = [v_ref[0, rows(c), cols(h)] for c, h in pairs]
    scores = [_dot_nt(a, b) * inner_mask[h] for a, b, (c, h) in zip(qr, kr, pairs)]
    inner = [_dot(s, t) for s, t in zip(scores, v)]
    kv = [_dot_tn(b * k_decay[h], t) for b, t, (c, h) in zip(kr, v, pairs)]
    states = []
    for h in range(RET_HEADS):
        state = state_ref[h]
        for c in range(n_chunks):
            states.append(((c, h), state))
            state = math.exp(log_gamma[h] * C) * state + kv[pairs.index((c, h))]
        state_ref[h] = state
    entering = dict(states)
    cross = [_dot(a * q_decay[h], entering[(c, h)]) for a, (c, h) in zip(qr, pairs)]
    for y_in, y_cr, (c, h) in zip(inner, cross, pairs):
        y = y_in + y_cr
        mu = jnp.mean(y, axis=-1, keepdims=True)
        d = y - mu
        var = jnp.mean(d * d, axis=-1, keepdims=True)
        yn = d * lax.rsqrt(var + LN_EPS) * gn_ref[:, cols(h)]
        g = g_ref[0, rows(c), cols(h)].astype(F32)
        o_ref[0, rows(c), cols(h)] = (g * jax.nn.sigmoid(g) * yn).astype(o_ref.dtype)


def _retention(p_ret, cos2, sin2, gn_g, ts=256):
    B, S, _ = p_ret.shape
    W = gn_g.shape[-1]
    blk = lambda c: pl.BlockSpec((1, ts, W), lambda b, n, c=c: (b, n, c))
    return pl.pallas_call(
        functools.partial(_ret_kernel, n_chunks=ts // CHUNK), grid=(B, S // ts),
        in_specs=[blk(0), blk(1), blk(2), blk(3),
                  pl.BlockSpec((ts, RET_HD), lambda b, n: (n, 0)),
                  pl.BlockSpec((ts, RET_HD), lambda b, n: (n, 0)),
                  pl.BlockSpec((1, W), lambda b, n: (0, 0))],
        out_specs=pl.BlockSpec((1, ts, W), lambda b, n: (b, n, 0)),
        out_shape=jax.ShapeDtypeStruct((B, S, W), BF16),
        scratch_shapes=[pltpu.VMEM((RET_HEADS, RET_HD, RET_HD), F32)],
        compiler_params=_params("parallel", "arbitrary"), name="retention",
    )(p_ret, p_ret, p_ret, p_ret, cos2, sin2, gn_g.reshape(1, W))


def _head_sum(x):
    R, W = x.shape
    lo = lax.broadcasted_iota(jnp.int32, (R, LANES), 1) < RWKV_HD
    outs = []
    for j in range(W // LANES):
        xs = x[:, j * LANES:(j + 1) * LANES]
        s_lo = jnp.sum(jnp.where(lo, xs, 0.0), axis=-1, keepdims=True)
        s_hi = jnp.sum(jnp.where(lo, 0.0, xs), axis=-1, keepdims=True)
        outs.append(jnp.where(lo, s_lo, s_hi))
    return jnp.concatenate(outs, axis=1)


def _block_diag(x, head_of_lane):
    parts = [jnp.where(head_of_lane == h, x, 0.0) for h in range(HEADS_PER_GROUP)]
    return jnp.concatenate(parts, axis=0).astype(BF16)


def _packed_mm(xs, ys, head_of_lane):
    bds = [_block_diag(y, head_of_lane) for y in ys]
    return [jnp.dot(x.astype(BF16), bd, preferred_element_type=F32) for x, bd in zip(xs, bds)]


def _unit_lower_inverse(a_list, t_idx, s_idx, head_of_lane):
    mm = functools.partial(_packed_mm, head_of_lane=head_of_lane)
    add = lambda xs, ys: [x + y for x, y in zip(xs, ys)]
    lower = s_idx < t_idx
    same4 = (t_idx >> 2) == (s_idx >> 2)
    same16 = (t_idx >> 4) == (s_idx >> 4)
    m1 = lower & same4
    m2 = lower & same16 & jnp.logical_not(same4)
    m3 = lower & jnp.logical_not(same16)
    eye = jnp.where(t_idx == s_idx, 1.0, 0.0)
    a1 = [jnp.where(m1, a, 0.0) for a in a_list]
    a1sq = mm(a1, a1)
    a1cu = mm(a1, a1sq)
    inv = [eye + p + q + r for p, q, r in zip(a1, a1sq, a1cu)]
    for m in (m2, m3):
        n = mm(inv, [jnp.where(m, a, 0.0) for a in a_list])
        x = add(inv, mm(mm(n, n), inv))
        inv = add(x, mm(n, x))
    return inv


def _rwkv_kernel(*refs):
    state_ref = refs[-1]

    @pl.when(pl.program_id(1) == 0)
    def _():
        for ref in refs[-5:]:
            ref[...] = jnp.zeros_like(ref)

    seqs = [_rwkv_prepare(bi, *refs) for bi in range(state_ref.shape[0])]
    _rwkv_chains(seqs, state_ref)
    for bi, seq in enumerate(seqs):
        _rwkv_finish(bi, seq, *refs)


def _rwkv_prepare(bi, pr_ref, pk_ref, pv_ref, pl_ref, mur_ref, muk_ref, muv_ref, mul_ref,
                  wup_ref, aup_ref, gup_ref, w0_ref, a0_ref, kk_ref, ka_ref, rk_ref, lng_ref, lnb_ref,
                  o_ref, carry_r, carry_k, carry_v, carry_l, state_ref):
    C = CHUNK

    def shifted(p_ref, mu_ref, carry):
        p = p_ref[bi].astype(F32)
        row = lax.broadcasted_iota(jnp.int32, p.shape, 0)
        prev = jnp.where(row == 0, carry[bi, 0:1, :], pltpu.roll(p, 1, axis=0))
        carry[bi, 0:1, :] = p[C - 1:C, :]
        return p + mu_ref[...] * (prev - p)

    r = shifted(pr_ref, mur_ref, carry_r)
    k = shifted(pk_ref, muk_ref, carry_k)
    v = shifted(pv_ref, muv_ref, carry_v)
    lo = shifted(pl_ref, mul_ref, carry_l)
    dwa = lo[:, :LANES]
    dg = lo[:, LANES:]

    z = w0_ref[...] + _dot(jnp.tanh(dwa), wup_ref[...])
    logw = -math.exp(-0.5) * jax.nn.sigmoid(z)
    a = jax.nn.sigmoid(a0_ref[...] + _dot(dwa, aup_ref[...]))
    g = _dot(jax.nn.sigmoid(dg), gup_ref[...])

    kk = k * kk_ref[...]
    kk = kk / jnp.maximum(jnp.sqrt(_head_sum(kk * kk)), 1e-12)
    k = k * (1.0 + (a - 1.0) * ka_ref[...])

    ti = lax.broadcasted_iota(jnp.int32, (C, C), 0)
    si = lax.broadcasted_iota(jnp.int32, (C, C), 1)
    tril = jnp.where(si <= ti, 1.0, 0.0).astype(BF16)
    lw_hi = logw.astype(BF16)
    lw_lo = (logw - lw_hi.astype(F32)).astype(BF16)
    cum = (jnp.dot(tril, lw_hi, preferred_element_type=F32)
           + jnp.dot(tril, lw_lo, preferred_element_type=F32))
    cum_last = cum[C - 1:C, :]
    e_neg = jnp.exp(-cum)
    abar = -kk * jnp.exp(cum - logw)
    rbar = r * jnp.exp(cum)
    kka = kk * a
    btil = kka * e_neg
    ktil = k * e_neg
    e_end = jnp.exp(cum_last - cum)
    bend = kka * e_end
    kend = k * e_end
    gamma_end = jnp.exp(cum_last)
    return dict(r=r, k=k, v=v, g=g, abar=abar, rbar=rbar, btil=btil, ktil=ktil, bend=bend, kend=kend,
                gamma_end=gamma_end)


def _rwkv_chains(seqs, state_ref):
    C = CHUNK
    n_groups = state_ref.shape[1]
    t_idx = lax.broadcasted_iota(jnp.int32, (C, GROUP_W), 0)
    lane = lax.broadcasted_iota(jnp.int32, (C, GROUP_W), 1)
    s_idx = lane & (RWKV_HD - 1)
    head_of_lane = lane >> HD_SHIFT
    strict = s_idx < t_idx
    incl = s_idx <= t_idx
    bd_r = lax.broadcasted_iota(jnp.int32, (GROUP_W, GROUP_W), 0) >> HD_SHIFT
    bd_c = lax.broadcasted_iota(jnp.int32, (GROUP_W, GROUP_W), 1) >> HD_SHIFT
    on_diag = bd_r == bd_c

    chains = [(bi, gi) for bi in range(len(seqs)) for gi in range(n_groups)]
    part = lambda name: [seqs[bi][name][:, gi * GROUP_W:(gi + 1) * GROUP_W] for bi, gi in chains]
    mm = functools.partial(_packed_mm, head_of_lane=head_of_lane)
    v_g = part("v")
    lhs = [jnp.concatenate([p, q], axis=0) for p, q in zip(part("abar"), part("rbar"))]
    rhs = [jnp.concatenate([_block_diag(p, head_of_lane), _block_diag(q, head_of_lane)], axis=0)
           for p, q in zip(part("btil"), part("ktil"))]
    gram = [_dot_nt(p, q) for p, q in zip(lhs, rhs)]
    a_ab = [jnp.where(strict, gm[:C, :GROUP_W], 0.0) for gm in gram]
    a_ak = [jnp.where(strict, gm[:C, GROUP_W:], 0.0) for gm in gram]
    m_rb = [jnp.where(incl, gm[C:, :GROUP_W], 0.0) for gm in gram]
    m_rk = [jnp.where(incl, gm[C:, GROUP_W:], 0.0) for gm in gram]
    inv = _unit_lower_inverse(a_ab, t_idx, s_idx, head_of_lane)
    st = [state_ref[bi, gi] for bi, gi in chains]
    from_state = [_dot_nt(p, q) for p, q in zip(lhs, st)]
    bd_v = [_block_diag(p, head_of_lane) for p in v_g]
    local = [jnp.dot(p.astype(BF16), bd, preferred_element_type=F32) for p, bd in zip(a_ak + m_rk, bd_v + bd_v)]
    n = len(chains)
    u = mm(inv, [fs[:C] + lc for fs, lc in zip(from_state, local[:n])])
    via_u = mm(m_rb, u)
    upd = [_dot_tn(jnp.concatenate([p, q], axis=0), jnp.concatenate([b, k], axis=0))
           for p, q, b, k in zip(u, v_g, part("bend"), part("kend"))]
    for (bi, gi), s_old, up, ge in zip(chains, st, upd, part("gamma_end")):
        state_ref[bi, gi] = s_old * ge + jnp.where(on_diag, up, 0.0)
    ys = [fs[C:] + p + q for fs, p, q in zip(from_state, via_u, local[n:])]
    for bi, seq in enumerate(seqs):
        seq["y"] = jnp.concatenate(ys[bi * n_groups:(bi + 1) * n_groups], axis=1)


def _rwkv_finish(bi, seq, pr_ref, pk_ref, pv_ref, pl_ref, mur_ref, muk_ref, muv_ref, mul_ref,
                 wup_ref, aup_ref, gup_ref, w0_ref, a0_ref, kk_ref, ka_ref, rk_ref, lng_ref, lnb_ref,
                 o_ref, carry_r, carry_k, carry_v, carry_l, state_ref):
    y, r, k, v, g = seq["y"], seq["r"], seq["k"], seq["v"], seq["g"]
    inv_hd = 1.0 / RWKV_HD
    mu = _head_sum(y) * inv_hd
    d = y - mu
    var = _head_sum(d * d) * inv_hd
    yn = d * lax.rsqrt(var + RWKV_GN_EPS) * lng_ref[...] + lnb_ref[...]
    bonus = _head_sum(r * k * rk_ref[...]) * v
    o_ref[bi] = ((yn + bonus) * g).astype(o_ref.dtype)


def _rwkv(p_rwkv, col0, mu, w_up, w0, a_up, a0, g_up, k_k, k_a, r_k, ln_g, ln_b, nb=4):
    B, S, _ = p_rwkv.shape
    W = w0.shape[-1]
    C = CHUNK
    wup_pad = jnp.concatenate([w_up, jnp.zeros_like(a_up)], axis=0).astype(BF16)
    aup_pad = jnp.concatenate([jnp.zeros_like(w_up), a_up], axis=0).astype(BF16)
    row = lambda t: t.reshape(1, -1)
    vec = lambda n: pl.BlockSpec((1, n), lambda b, t: (0, 0))
    mat = lambda a: pl.BlockSpec(a.shape, lambda b, t: (0, 0))
    lw = mu.shape[-1] - 3 * W
    assert col0 % W == 0 and (col0 + 3 * W) % lw == 0 and B % nb == 0 and S % C == 0
    cblk = lambda c: pl.BlockSpec((nb, C, W), lambda b, t, c=c: (b, t, col0 // W + c))
    n_groups = W // GROUP_W
    gup = g_up.astype(BF16)
    return pl.pallas_call(
        _rwkv_kernel, grid=(B // nb, S // C),
        in_specs=[cblk(0), cblk(1), cblk(2),
                  pl.BlockSpec((nb, C, lw), lambda b, t: (b, t, (col0 + 3 * W) // lw)),
                  vec(W), vec(W), vec(W), vec(lw),
                  mat(wup_pad), mat(aup_pad), mat(gup),
                  vec(W), vec(W), vec(W), vec(W), vec(W), vec(W), vec(W)],
        out_specs=pl.BlockSpec((nb, C, W), lambda b, t: (b, t, 0)),
        out_shape=jax.ShapeDtypeStruct((B, S, W), BF16),
        scratch_shapes=[pltpu.VMEM((nb, 8, W), F32), pltpu.VMEM((nb, 8, W), F32), pltpu.VMEM((nb, 8, W), F32),
                        pltpu.VMEM((nb, 8, lw), F32),
                        pltpu.VMEM((nb, n_groups, GROUP_W, GROUP_W), F32)],
        compiler_params=_params("parallel", "arbitrary"), name="rwkv7",
    )(p_rwkv, p_rwkv, p_rwkv, p_rwkv,
      row(mu[:W]), row(mu[W:2 * W]), row(mu[2 * W:3 * W]), row(mu[3 * W:]),
      wup_pad, aup_pad, gup,
      row(w0), row(a0), row(k_k), row(k_a), row(r_k), row(ln_g), row(ln_b))


def _merge_kernel(yr_ref, yw_ref, pg_ref, x_ref, wr_ref, ww_ref, wo_ref, g_ref, b_ref, o_ref, *, alpha):
    tm, D = x_ref.shape
    half = tm // 2

    def branches(rs):
        return (jnp.dot(yr_ref[rs, :], wr_ref[...], preferred_element_type=F32),
                jnp.dot(yw_ref[rs, :], ww_ref[...], preferred_element_type=F32))

    def mix(rs, up):
        gate = jax.nn.sigmoid(pg_ref[rs, :].astype(F32))
        return (gate[:, :D] * up[0] + gate[:, D:] * up[1]).astype(BF16)

    def out_proj(m):
        return jnp.dot(m, wo_ref[...], preferred_element_type=F32)

    def finish(rs, h):
        o_ref[rs, :] = _ln(alpha * x_ref[rs, :] + h, g_ref[...], b_ref[...])

    rows_a, rows_b = slice(0, half), slice(half, tm)
    up_a = branches(rows_a)
    up_b = branches(rows_b)
    h_a = out_proj(mix(rows_a, up_a))
    h_b = out_proj(mix(rows_b, up_b))
    finish(rows_a, h_a)
    finish(rows_b, h_b)


def _merge(y_ret, y_rwkv, p_gate, gate_col0, x, w_ret_up, w_rwkv_up, w_out, ln_g, ln_b, alpha, tm=512):
    T, D = x.shape
    W = y_ret.shape[1]
    assert gate_col0 % (2 * D) == 0
    full = lambda a: pl.BlockSpec(a.shape, lambda i: (0, 0))
    rowblk = lambda n: pl.BlockSpec((tm, n), lambda i: (i, 0))
    return pl.pallas_call(
        functools.partial(_merge_kernel, alpha=alpha), grid=(T // tm,),
        in_specs=[rowblk(W), rowblk(W), pl.BlockSpec((tm, 2 * D), lambda i: (i, gate_col0 // (2 * D))), rowblk(D),
                  full(w_ret_up), full(w_rwkv_up), full(w_out),
                  pl.BlockSpec((1, D), lambda i: (0, 0)), pl.BlockSpec((1, D), lambda i: (0, 0))],
        out_specs=rowblk(D),
        out_shape=jax.ShapeDtypeStruct((T, D), F32),
        compiler_params=_params("parallel"), name="merge_out_ln",
    )(y_ret, y_rwkv, p_gate, x, w_ret_up, w_rwkv_up, w_out, ln_g.reshape(1, D), ln_b.reshape(1, D))


def _xattn_kernel(x_ref, kv_ref, wq_ref, wo_ref, g_ref, b_ref, rwt_ref, rb_ref, o_ref, cls_ref, *, alpha):
    tm, D = x_ref.shape[1:]
    hd = D // XA_HEADS
    half = tm // 2

    def project(rs):
        return jnp.dot(x_ref[0, rs, :].astype(BF16), wq_ref[...], preferred_element_type=F32)

    def scores(q):
        return [_dot_nt(q[:, h * hd:(h + 1) * hd], kv_ref[0, :, h * hd:(h + 1) * hd]) * (hd ** -0.5)
                for h in range(XA_HEADS)]

    def attend(s_heads):
        outs = []
        for h, s in enumerate(s_heads):
            e = jnp.exp(s - jnp.max(s, axis=-1, keepdims=True))
            probs = e / jnp.sum(e, axis=-1, keepdims=True)
            outs.append(jnp.dot(probs.astype(BF16), kv_ref[0, :, D + h * hd:D + (h + 1) * hd],
                                preferred_element_type=F32))
        return jnp.concatenate(outs, axis=1)

    def out_proj(o):
        return jnp.dot(o.astype(BF16), wo_ref[...], preferred_element_type=F32)

    def finish(rs, h_out):
        x2 = _ln(alpha * x_ref[0, rs, :] + h_out, g_ref[...], b_ref[...])
        comb, cls = _route(x2, rwt_ref[...], rb_ref[...])
        E = comb.shape[0]
        comb_pad = jnp.concatenate([comb, jnp.zeros((LANES - E, half), F32)], axis=0)
        o_ref[0, rs, :D] = x2
        o_ref[0, rs, D:] = comb_pad.T
        cls_ref[0, :, rs] = cls

    rows_a, rows_b = slice(0, half), slice(half, tm)
    s_a = scores(project(rows_a))
    q_b = project(rows_b)
    o_a = attend(s_a)
    s_b = scores(q_b)
    h_a = out_proj(o_a)
    o_b = attend(s_b)
    finish(rows_a, h_a)
    h_b = out_proj(o_b)
    finish(rows_b, h_b)


def _cross_attention(x, kv, wq, wo, ln_g, ln_b, router_w, router_bias, alpha, tm=512):
    B, S, D = x.shape
    M = kv.shape[1]
    E = router_w.shape[1]
    full = lambda a: pl.BlockSpec(a.shape, lambda b, i: (0, 0))
    n_t = S // tm
    return pl.pallas_call(
        functools.partial(_xattn_kernel, alpha=alpha), grid=(B, n_t),
        in_specs=[pl.BlockSpec((1, tm, D), lambda b, i: (b, i, 0)),
                  pl.BlockSpec((1, M, 2 * D), lambda b, i: (b, 0, 0)),
                  full(wq), full(wo),
                  pl.BlockSpec((1, D), lambda b, i: (0, 0)), pl.BlockSpec((1, D), lambda b, i: (0, 0)),
                  pl.BlockSpec((D, LANES), lambda b, i: (0, 0)), pl.BlockSpec((E, 1), lambda b, i: (0, 0))],
        out_specs=[pl.BlockSpec((1, tm, D + LANES), lambda b, i: (b, i, 0)),
                   pl.BlockSpec((1, 1, tm), lambda b, i: (b * n_t + i, 0, 0))],
        out_shape=[jax.ShapeDtypeStruct((B, S, D + LANES), F32),
                   jax.ShapeDtypeStruct((B * n_t, 1, tm), jnp.int32)],
        compiler_params=_params("parallel", "parallel"), name="xattn_ln_route",
    )(x, kv, wq, wo, ln_g.reshape(1, D), ln_b.reshape(1, D),
      jnp.pad(router_w, ((0, 0), (0, LANES - E))), router_bias.reshape(E, 1))


def _expert_pairs(per_group):
    todo = [(i, j) for i in range(per_group) for j in range(i + 1, per_group)]
    order = [todo.pop(0)]
    while todo:
        nxt = next((p for p in todo if set(p) & set(order[-1])), todo[0])
        todo.remove(nxt)
        order.append(nxt)
    return order


def _split_bf16(v):
    hi = v.astype(BF16)
    return hi, (v - hi.astype(F32)).astype(BF16)


def _route(x, w, bias):
    E = bias.shape[0]
    per_group = E // N_GROUPS
    x_hi, x_lo = _split_bf16(x)
    w_hi, w_lo = _split_bf16(w)
    logits = (jnp.dot(x_hi, w_hi, preferred_element_type=F32) + jnp.dot(x_hi, w_lo, preferred_element_type=F32)
              + jnp.dot(x_lo, w_hi, preferred_element_type=F32)).T[:E]
    aff = jax.nn.sigmoid(logits)
    choice = aff + bias
    rows = [choice[e:e + 1, :] for e in range(E)]
    scores = []
    for gidx in range(N_GROUPS):
        members = rows[gidx * per_group:(gidx + 1) * per_group]
        best = None
        for i in range(per_group):
            for j in range(i + 1, per_group):
                pair = members[i] + members[j]
                best = pair if best is None else jnp.maximum(best, pair)
        scores.append(best)
    top = scores[0]
    for s in scores[1:]:
        top = jnp.maximum(top, s)
    taken = jnp.zeros_like(top, dtype=jnp.bool_)
    in_best = []
    for s in scores:
        hit = jnp.logical_and(s == top, jnp.logical_not(taken))
        in_best.append(hit)
        taken = jnp.logical_or(taken, hit)
    sel_w = []
    cls = jnp.zeros(top.shape, jnp.int32)
    pairs = _expert_pairs(per_group)
    for gidx in range(N_GROUPS):
        members = rows[gidx * per_group:(gidx + 1) * per_group]
        chosen = []
        for i in range(per_group):
            rank = jnp.zeros_like(top)
            for j in range(per_group):
                if j == i:
                    continue
                ahead = (members[j] > members[i]) if j > i else (members[j] >= members[i])
                rank = rank + jnp.where(ahead, 1.0, 0.0)
            chosen.append(jnp.logical_and(in_best[gidx], rank < TOP_K))
            e = gidx * per_group + i
            sel_w.append(jnp.where(chosen[i], aff[e:e + 1, :], 0.0))
        for p, (i, j) in enumerate(pairs):
            cls = jnp.where(jnp.logical_and(chosen[i], chosen[j]), gidx * len(pairs) + p, cls)
    total = sel_w[0]
    for w in sel_w[1:]:
        total = total + w
    return jnp.concatenate(sel_w, axis=0) / total, cls


def _rank_kernel(cls_ref, rank_ref, cnt_ref, base_ref):
    @pl.when(pl.program_id(0) == 0)
    def _():
        base_ref[...] = jnp.zeros_like(base_ref)

    cls = cls_ref[0]
    tm = cls.shape[-1]
    n_cls = base_ref.shape[0]
    onehot = lax.broadcasted_iota(jnp.int32, (n_cls, tm), 0) == cls
    earlier = (lax.broadcasted_iota(jnp.int32, (tm, tm), 0)
               < lax.broadcasted_iota(jnp.int32, (tm, tm), 1))
    ones = jnp.where(onehot, 1.0, 0.0)
    before = jnp.dot(ones.astype(BF16), jnp.where(earlier, 1.0, 0.0).astype(BF16),
                     preferred_element_type=F32)
    base = base_ref[...]
    rank = jnp.sum(jnp.where(onehot, base + before, 0.0), axis=0, keepdims=True)
    rank_ref[0] = rank.astype(jnp.int32)
    total = base + jnp.sum(ones, axis=1, keepdims=True)
    base_ref[...] = total
    cnt_ref[...] = jnp.broadcast_to(total, cnt_ref.shape)


def _class_ranks(cls, n_cls_pad):
    n_t, _, tm = cls.shape
    return pl.pallas_call(
        _rank_kernel, grid=(n_t,),
        in_specs=[pl.BlockSpec((1, 1, tm), lambda i: (i, 0, 0))],
        out_specs=[pl.BlockSpec((1, 1, tm), lambda i: (i, 0, 0)),
                   pl.BlockSpec((n_cls_pad, LANES), lambda i: (0, 0))],
        out_shape=[jax.ShapeDtypeStruct((n_t, 1, tm), jnp.int32),
                   jax.ShapeDtypeStruct((n_cls_pad, LANES), F32)],
        scratch_shapes=[pltpu.VMEM((n_cls_pad, 1), F32)],
        compiler_params=_params("arbitrary"), name="moe_class_rank",
    )(cls)


def _row_copy(src, src_row, dst, dst_row, sem):
    return pltpu.make_async_copy(src.at[pl.ds(src_row, 1)], dst.at[pl.ds(dst_row, 1)], sem)


def _dispatch_kernel(dest_ref, x_ref, o_hbm, sem):
    tm = x_ref.shape[0]
    base = pl.program_id(0) * tm

    def start(r, carry):
        _row_copy(x_ref, r, o_hbm, dest_ref[base + r], sem).start()
        return carry

    def wait(r, carry):
        _row_copy(x_ref, r, o_hbm, dest_ref[base + r], sem).wait()
        return carry

    lax.fori_loop(0, tm, start, 0, unroll=8)
    lax.fori_loop(0, tm, wait, 0, unroll=8)


def _dispatch(xa, dest, tm=512):
    T, W = xa.shape
    return pl.pallas_call(
        _dispatch_kernel,
        grid_spec=pltpu.PrefetchScalarGridSpec(
            num_scalar_prefetch=1, grid=(T // tm,),
            in_specs=[pl.BlockSpec((tm, W), lambda i, dest: (i, 0))],
            out_specs=pl.BlockSpec(memory_space=pl.ANY),
            scratch_shapes=[pltpu.SemaphoreType.DMA(())]),
        out_shape=jax.ShapeDtypeStruct((T, W), xa.dtype),
        compiler_params=_params("arbitrary"), name="moe_dispatch",
    )(dest, xa)


def _undispatch_kernel(dest_ref, y_hbm, o_ref, sem):
    tm = o_ref.shape[0]
    base = pl.program_id(0) * tm

    def start(r, carry):
        _row_copy(y_hbm, dest_ref[base + r], o_ref, r, sem).start()
        return carry

    def wait(r, carry):
        _row_copy(y_hbm, dest_ref[base + r], o_ref, r, sem).wait()
        return carry

    lax.fori_loop(0, tm, start, 0, unroll=8)
    lax.fori_loop(0, tm, wait, 0, unroll=8)


def _undispatch(ys, dest, tm=512):
    T, D = ys.shape
    return pl.pallas_call(
        _undispatch_kernel,
        grid_spec=pltpu.PrefetchScalarGridSpec(
            num_scalar_prefetch=1, grid=(T // tm,),
            in_specs=[pl.BlockSpec(memory_space=pl.ANY)],
            out_specs=pl.BlockSpec((tm, D), lambda i, dest: (i, 0)),
            scratch_shapes=[pltpu.SemaphoreType.DMA(())]),
        out_shape=jax.ShapeDtypeStruct((T, D), ys.dtype),
        compiler_params=_params("arbitrary"), name="moe_undispatch",
    )(dest, ys)


STEP_FIRST, STEP_LAST, STEP_VALID = 1, 2, 4


def _moe_kernel(tile_ref, exp_ref, flag_ref, xs_ref, wg_ref, wu_ref, wd_ref, g_ref, b_ref, o_ref,
                xbf_ref, acc_ref, *, alpha):
    step = pl.program_id(0)
    flags = flag_ref[step]
    e = exp_ref[step]
    D = o_ref.shape[-1]

    @pl.when((flags & STEP_FIRST) != 0)
    def _():
        xbf_ref[...] = xs_ref[:, :D].astype(BF16)
        acc_ref[...] = jnp.zeros_like(acc_ref)

    @pl.when((flags & STEP_VALID) != 0)
    def _():
        half = xbf_ref.shape[0] // 2

        def up(rs):
            xb = xbf_ref[rs, :]
            return (jnp.dot(xb, wg_ref[0], preferred_element_type=F32),
                    jnp.dot(xb, wu_ref[0], preferred_element_type=F32))

        def act(hg, hu):
            return (hg * jax.nn.sigmoid(hg) * hu).astype(BF16)

        def down(h):
            return jnp.dot(h, wd_ref[0], preferred_element_type=F32)

        def accumulate(rs, y):
            comb = xs_ref[rs, D:]
            lane = lax.broadcasted_iota(jnp.int32, comb.shape, 1)
            w_e = jnp.sum(jnp.where(lane == e, comb, 0.0), axis=-1, keepdims=True)
            acc_ref[rs, :] += w_e * y

        rows_a, rows_b = slice(0, half), slice(half, 2 * half)
        up_a = up(rows_a)
        up_b = up(rows_b)
        y_a = down(act(*up_a))
        y_b = down(act(*up_b))
        accumulate(rows_a, y_a)
        accumulate(rows_b, y_b)

    @pl.when((flags & STEP_LAST) != 0)
    def _():
        o_ref[...] = _ln(alpha * xs_ref[:, :D] + acc_ref[...], g_ref[...], b_ref[...])


def _moe_steps(counts, n_experts, n_tiles, tm):
    n_cls = counts.shape[0]
    per_group = n_experts // N_GROUPS
    pairs = _expert_pairs(per_group)
    uses = [[0.0] * n_experts for _ in range(n_cls)]
    for c in range(n_cls):
        gidx, (i, j) = c // len(pairs), pairs[c % len(pairs)]
        uses[c][gidx * per_group + i] = 1.0
        uses[c][gidx * per_group + j] = 1.0
    uses = jnp.array(uses, F32)
    ends = jnp.cumsum(counts)
    starts = ends - counts
    lo = (jnp.arange(n_tiles, dtype=jnp.int32) * tm)[:, None]
    present = jnp.logical_and(starts[None, :] < lo + tm, ends[None, :] > lo)
    need = jnp.dot(present.astype(F32), uses) > 0
    odd = (jnp.arange(n_tiles, dtype=jnp.int32) % 2 == 1)[:, None]
    need = jnp.where(odd, need[:, ::-1], need).reshape(-1)
    n_steps = min(n_tiles * n_experts, TOP_K * (n_tiles + n_cls - 1))
    n_valid = jnp.sum(need.astype(jnp.int32))
    idx = jnp.nonzero(need, size=n_steps, fill_value=0)[0].astype(jnp.int32)
    k = jnp.arange(n_steps, dtype=jnp.int32)
    valid = k < n_valid
    idx = jnp.where(valid, idx, idx[n_valid - 1])
    tile, expert = idx // n_experts, idx % n_experts
    expert = jnp.where(tile % 2 == 1, n_experts - 1 - expert, expert)
    prev_tile = jnp.concatenate([jnp.full((1,), -1, jnp.int32), tile[:-1]])
    next_tile = jnp.concatenate([tile[1:], jnp.full((1,), -1, jnp.int32)])
    first = jnp.logical_and(valid, tile != prev_tile)
    last = jnp.logical_and(valid, jnp.logical_or(tile != next_tile, k == n_valid - 1))
    flags = (first * STEP_FIRST + last * STEP_LAST + valid * STEP_VALID).astype(jnp.int32)
    return tile, expert, flags


def _moe_experts(xs, steps, w_gate, w_up, w_down, ln_g, ln_b, alpha, tm):
    T, W = xs.shape
    D = W - LANES
    E, _, F = w_gate.shape
    tile, expert, flags = steps
    return pl.pallas_call(
        functools.partial(_moe_kernel, alpha=alpha),
        grid_spec=pltpu.PrefetchScalarGridSpec(
            num_scalar_prefetch=3, grid=(tile.shape[0],),
            in_specs=[pl.BlockSpec((tm, W), lambda s, t, e, f: (t[s], 0)),
                      pl.BlockSpec((1, D, F), lambda s, t, e, f: (e[s], 0, 0)),
                      pl.BlockSpec((1, D, F), lambda s, t, e, f: (e[s], 0, 0)),
                      pl.BlockSpec((1, F, D), lambda s, t, e, f: (e[s], 0, 0)),
                      pl.BlockSpec((1, D), lambda s, t, e, f: (0, 0)),
                      pl.BlockSpec((1, D), lambda s, t, e, f: (0, 0))],
            out_specs=pl.BlockSpec((tm, D), lambda s, t, e, f: (t[s], 0)),
            scratch_shapes=[pltpu.VMEM((tm, D), BF16), pltpu.VMEM((tm, D), F32)]),
        out_shape=jax.ShapeDtypeStruct((T, D), F32),
        compiler_params=_params("arbitrary"), name="moe_experts_ln",
    )(tile, expert, flags, xs, w_gate, w_up, w_down, ln_g.reshape(1, D), ln_b.reshape(1, D))


def _moe(xa, cls, w_gate, w_up, w_down, ln_g, ln_b, alpha, tm=512):
    T = xa.shape[0]
    E = w_gate.shape[0]
    n_cls = N_GROUPS * len(_expert_pairs(E // N_GROUPS))
    rank, cnt = _class_ranks(cls, -(-n_cls // 8) * 8)
    counts = cnt[:n_cls, 0].astype(jnp.int32)
    starts = jnp.cumsum(counts) - counts
    dest = (starts[cls.reshape(T)] + rank.reshape(T)).astype(jnp.int32)
    xs = _dispatch(xa, dest)
    ys = _moe_experts(xs, _moe_steps(counts, E, T // tm, tm), w_gate, w_up, w_down, ln_g, ln_b, alpha, tm)
    return _undispatch(ys, dest)


def _rotary_tables(S):
    half = RET_HD // 2
    inv_freq = ROPE_BASE ** (-jnp.arange(half, dtype=F32) / half)
    ang = jnp.arange(S).astype(F32)[:, None] * inv_freq[None, :]
    cos, sin = jnp.cos(ang), jnp.sin(ang)
    return jnp.concatenate([cos, cos], axis=1), jnp.concatenate([-sin, sin], axis=1)


def kernel(x, mem, ln_in_g, ln_in_b, router_w, router_bias, w_in, ret_gn_g, rwkv_mu, rwkv_w_up, rwkv_w0,
           rwkv_a_up, rwkv_a0, rwkv_g_up, rwkv_k_k, rwkv_k_a, rwkv_r_k, rwkv_ln_g, rwkv_ln_b,
           w_ret_up, w_rwkv_up, w_out, ln1_g, ln1_b, xa_wq, xa_wkv, xa_wo, ln2_g, ln2_b,
           moe_w_gate, moe_w_up, moe_w_down, ln3_g, ln3_b):
    B, S, D = x.shape
    T = B * S
    depth = w_in.shape[0]
    alpha = (2 * depth) ** 0.25
    ret_w = ret_gn_g.shape[-1]
    rwkv_w = rwkv_w0.shape[-1]
    ret_cols = 4 * ret_w
    rwkv_cols = rwkv_mu.shape[-1]
    n_proj = -(-w_in.shape[-1] // IN_PROJ_TN) * IN_PROJ_TN
    cos2, sin2 = _rotary_tables(S)
    mem2 = mem.reshape(B * mem.shape[1], D)

    xs = _layer_norm(x.reshape(T, D), ln_in_g, ln_in_b)
    for l in range(depth):
        w_l = w_in[l]
        w_cat = jnp.concatenate([w_l[:, :ret_cols], w_l[:, ret_cols + rwkv_cols:], w_l[:, ret_cols:ret_cols + rwkv_cols],
                                 jnp.zeros((D, n_proj - w_l.shape[1]), w_l.dtype)], axis=1).astype(BF16)
        p = _matmul(xs, w_cat, 1024, IN_PROJ_TN, BF16, "in_proj")
        p3 = p.reshape(B, S, n_proj)
        y_ret = _retention(p3, cos2, sin2, ret_gn_g[l])
        y_rwkv = _rwkv(p3, ret_cols + 2 * D, rwkv_mu[l], rwkv_w_up[l], rwkv_w0[l], rwkv_a_up[l],
                       rwkv_a0[l], rwkv_g_up[l], rwkv_k_k[l], rwkv_k_a[l], rwkv_r_k[l].reshape(-1),
                       rwkv_ln_g[l], rwkv_ln_b[l])
        xs = _merge(y_ret.reshape(T, ret_w), y_rwkv.reshape(T, rwkv_w), p, ret_cols, xs,
                    w_ret_up[l].astype(BF16), w_rwkv_up[l].astype(BF16), w_out[l].astype(BF16),
                    ln1_g[l], ln1_b[l], alpha)
        kv = _matmul(mem2, xa_wkv[l].astype(BF16), 512, 512, BF16, "xattn_kv")
        xa, cls = _cross_attention(xs.reshape(B, S, D), kv.reshape(B, -1, 2 * D), xa_wq[l].astype(BF16),
                                   xa_wo[l].astype(BF16), ln2_g[l], ln2_b[l], router_w, router_bias, alpha)
        xs = _moe(xa.reshape(T, D + LANES), cls, moe_w_gate[l].astype(BF16), moe_w_up[l].astype(BF16),
                  moe_w_down[l].astype(BF16), ln3_g[l], ln3_b[l], alpha)
    return xs.reshape(B, S, D)
```

```python
import functools
import math

import jax
import jax.numpy as jnp
from jax import lax
from jax.experimental import pallas as pl
from jax.experimental.pallas import tpu as pltpu

F32 = jnp.float32
BF16 = jnp.bfloat16

LANES = 128
SUBLANES = 8
VMEM_LIMIT = 56 * 1024 * 1024

CHUNK = 64
RET_HEADS = 4
RET_HD = 128
ROPE_BASE = 10000.0
RWKV_HD = 64
HD_SHIFT = 6
RWKV_GN_EPS = 64e-5
XA_HEADS = 4
N_GROUPS = 4
TOP_K = 2
LN_EPS = 1e-5
IN_PROJ_TN = 1024
HEADS_PER_GROUP = 4
GROUP_W = HEADS_PER_GROUP * RWKV_HD


def _params(*sem):
    return pltpu.CompilerParams(dimension_semantics=sem, vmem_limit_bytes=VMEM_LIMIT)


def _ln(v, g, b, eps=LN_EPS):
    mu = jnp.mean(v, axis=-1, keepdims=True)
    d = v - mu
    var = jnp.mean(d * d, axis=-1, keepdims=True)
    return d * lax.rsqrt(var + eps) * g + b


def _dot(a, b):
    return jnp.dot(a.astype(BF16), b.astype(BF16), preferred_element_type=F32)


def _dot_nt(a, b):
    return lax.dot_general(a.astype(BF16), b.astype(BF16), (((1,), (1,)), ((), ())),
                           preferred_element_type=F32)


def _dot_tn(a, b):
    return lax.dot_general(a.astype(BF16), b.astype(BF16), (((0,), (0,)), ((), ())),
                           preferred_element_type=F32)


def _ln_kernel(x_ref, g_ref, b_ref, o_ref):
    o_ref[...] = _ln(x_ref[...], g_ref[...], b_ref[...])


def _layer_norm(x, g, b, tm=512):
    T, D = x.shape
    return pl.pallas_call(
        _ln_kernel, grid=(T // tm,),
        in_specs=[pl.BlockSpec((tm, D), lambda i: (i, 0)),
                  pl.BlockSpec((1, D), lambda i: (0, 0)),
                  pl.BlockSpec((1, D), lambda i: (0, 0))],
        out_specs=pl.BlockSpec((tm, D), lambda i: (i, 0)),
        out_shape=jax.ShapeDtypeStruct((T, D), F32),
        compiler_params=_params("parallel"), name="ln_in",
    )(x, g.reshape(1, D), b.reshape(1, D))


def _mm_kernel(a_ref, w_ref, o_ref, abf_ref):
    @pl.when(pl.program_id(1) == 0)
    def _():
        abf_ref[...] = a_ref[...].astype(BF16)

    o_ref[...] = jnp.dot(abf_ref[...], w_ref[...], preferred_element_type=F32).astype(o_ref.dtype)


def _matmul(a, w, tm, tn, out_dtype, name):
    M, K = a.shape
    N = w.shape[1]
    return pl.pallas_call(
        _mm_kernel, grid=(M // tm, N // tn),
        in_specs=[pl.BlockSpec((tm, K), lambda i, j: (i, 0)),
                  pl.BlockSpec((K, tn), lambda i, j: (0, j))],
        out_specs=pl.BlockSpec((tm, tn), lambda i, j: (i, j)),
        out_shape=jax.ShapeDtypeStruct((M, N), out_dtype),
        scratch_shapes=[pltpu.VMEM((tm, K), BF16)],
        compiler_params=_params("parallel", "arbitrary"), name=name,
    )(a, w)


def _ret_kernel(q_ref, k_ref, v_ref, g_ref, cos_ref, sin_ref, gn_ref, o_ref, state_ref, *, n_chunks):
    @pl.when(pl.program_id(1) == 0)
    def _():
        state_ref[...] = jnp.zeros_like(state_ref)

    C, hd = CHUNK, RET_HD
    ti = lax.broadcasted_iota(jnp.int32, (C, C), 0)
    si = lax.broadcasted_iota(jnp.int32, (C, C), 1)
    dist = jnp.abs(ti - si).astype(F32)
    cpos = lax.broadcasted_iota(jnp.int32, (C, 1), 0).astype(F32)
    log_gamma = [math.log(1.0 - 2.0 ** (-5.0 - h)) for h in range(RET_HEADS)]
    inner_mask = [jnp.exp(lg * dist) for lg in log_gamma]
    k_decay = [jnp.exp(lg * (C - 1.0 - cpos)) for lg in log_gamma]
    q_decay = [jnp.exp(lg * (cpos + 1.0)) for lg in log_gamma]
    pairs = [(c, h) for c in range(n_chunks) for h in range(RET_HEADS)]
    rows = lambda c: slice(c * C, (c + 1) * C)
    cols = lambda h: slice(h * hd, (h + 1) * hd)

    def rotated(ref, c, h):
        t = ref[0, rows(c), cols(h)].astype(F32)
        return t * cos_ref[rows(c), :] + pltpu.roll(t, hd // 2, axis=1) * sin_ref[rows(c), :]

    qr = [rotated(q_ref, c, h) for c, h in pairs]
    kr = [rotated(k_ref, c, h) * (hd ** -0.5) for c, h in pairs]
    v = [v_ref[0, rows(c), cols(h)] for c, h in pairs]
    scores = [_dot_nt(a, b) * inner_mask[h] for a, b, (c, h) in zip(qr, kr, pairs)]
    inner = [_dot(s, t) for s, t in zip(scores, v)]
    kv = [_dot_tn(b * k_decay[h], t) for b, t, (c, h) in zip(kr, v, pairs)]
    states = []
    for h in range(RET_HEADS):
        state = state_ref[h]
        for c in range(n_chunks):
            states.append(((c, h), state))
            state = math.exp(log_gamma[h] * C) * state + kv[pairs.index((c, h))]
        state_ref[h] = state
    entering = dict(states)
    cross = [_dot(a * q_decay[h], entering[(c, h)]) for a, (c, h) in zip(qr, pairs)]
    for y_in, y_cr, (c, h) in zip(inner, cross, pairs):
        y = y_in + y_cr
        mu = jnp.mean(y, axis=-1, keepdims=True)
        d = y - mu
        var = jnp.mean(d * d, axis=-1, keepdims=True)
        yn = d * lax.rsqrt(var + LN_EPS) * gn_ref[:, cols(h)]
        g = g_ref[0, rows(c), cols(h)].astype(F32)
        o_ref[0, rows(c), cols(h)] = (g * jax.nn.sigmoid(g) * yn).astype(o_ref.dtype)


def _retention(p_ret, cos2, sin2, gn_g, ts=256):
    B, S, _ = p_ret.shape
    W = gn_g.shape[-1]
    blk = lambda c: pl.BlockSpec((1, ts, W), lambda b, n, c=c: (b, n, c))
    return pl.pallas_call(
        functools.partial(_ret_kernel, n_chunks=ts // CHUNK), grid=(B, S // ts),
        in_specs=[blk(0), blk(1), blk(2), blk(3),
                  pl.BlockSpec((ts, RET_HD), lambda b, n: (n, 0)),
                  pl.BlockSpec((ts, RET_HD), lambda b, n: (n, 0)),
                  pl.BlockSpec((1, W), lambda b, n: (0, 0))],
        out_specs=pl.BlockSpec((1, ts, W), lambda b, n: (b, n, 0)),
        out_shape=jax.ShapeDtypeStruct((B, S, W), BF16),
        scratch_shapes=[pltpu.VMEM((RET_HEADS, RET_HD, RET_HD), F32)],
        compiler_params=_params("parallel", "arbitrary"), name="retention",
    )(p_ret, p_ret, p_ret, p_ret, cos2, sin2, gn_g.reshape(1, W))


def _head_sum(x):
    R, W = x.shape
    lo = lax.broadcasted_iota(jnp.int32, (R, LANES), 1) < RWKV_HD
    outs = []
    for j in range(W // LANES):
        xs = x[:, j * LANES:(j + 1) * LANES]
        s_lo = jnp.sum(jnp.where(lo, xs, 0.0), axis=-1, keepdims=True)
        s_hi = jnp.sum(jnp.where(lo, 0.0, xs), axis=-1, keepdims=True)
        outs.append(jnp.where(lo, s_lo, s_hi))
    return jnp.concatenate(outs, axis=1)


def _block_diag(x, head_of_lane):
    parts = [jnp.where(head_of_lane == h, x, 0.0) for h in range(HEADS_PER_GROUP)]
    return jnp.concatenate(parts, axis=0).astype(BF16)


def _packed_mm(xs, ys, head_of_lane):
    bds = [_block_diag(y, head_of_lane) for y in ys]
    return [jnp.dot(x.astype(BF16), bd, preferred_element_type=F32) for x, bd in zip(xs, bds)]


def _unit_lower_inverse(a_list, t_idx, s_idx, head_of_lane):
    mm = functools.partial(_packed_mm, head_of_lane=head_of_lane)
    add = lambda xs, ys: [x + y for x, y in zip(xs, ys)]
    lower = s_idx < t_idx
    same4 = (t_idx >> 2) == (s_idx >> 2)
    same16 = (t_idx >> 4) == (s_idx >> 4)
    m1 = lower & same4
    m2 = lower & same16 & jnp.logical_not(same4)
    m3 = lower & jnp.logical_not(same16)
    eye = jnp.where(t_idx == s_idx, 1.0, 0.0)
    a1 = [jnp.where(m1, a, 0.0) for a in a_list]
    a1sq = mm(a1, a1)
    a1cu = mm(a1, a1sq)
    inv = [eye + p + q + r for p, q, r in zip(a1, a1sq, a1cu)]
    for m in (m2, m3):
        n = mm(inv, [jnp.where(m, a, 0.0) for a in a_list])
        x = add(inv, mm(mm(n, n), inv))
        inv = add(x, mm(n, x))
    return inv


def _rwkv_kernel(*refs):
    state_ref = refs[-1]

    @pl.when(pl.program_id(1) == 0)
    def _():
        for ref in refs[-5:]:
            ref[...] = jnp.zeros_like(ref)

    seqs = [_rwkv_prepare(bi, *refs) for bi in range(state_ref.shape[0])]
    _rwkv_chains(seqs, state_ref)
    for bi, seq in enumerate(seqs):
        _rwkv_finish(bi, seq, *refs)


def _rwkv_prepare(bi, pr_ref, pk_ref, pv_ref, pl_ref, mur_ref, muk_ref, muv_ref, mul_ref,
                  wup_ref, aup_ref, gup_ref, w0_ref, a0_ref, kk_ref, ka_ref, rk_ref, lng_ref, lnb_ref,
                  o_ref, carry_r, carry_k, carry_v, carry_l, state_ref):
    C = CHUNK

    def shifted(p_ref, mu_ref, carry):
        p = p_ref[bi].astype(F32)
        row = lax.broadcasted_iota(jnp.int32, p.shape, 0)
        prev = jnp.where(row == 0, carry[bi, 0:1, :], pltpu.roll(p, 1, axis=0))
        carry[bi, 0:1, :] = p[C - 1:C, :]
        return p + mu_ref[...] * (prev - p)

    r = shifted(pr_ref, mur_ref, carry_r)
    k = shifted(pk_ref, muk_ref, carry_k)
    v = shifted(pv_ref, muv_ref, carry_v)
    lo = shifted(pl_ref, mul_ref, carry_l)
    dwa = lo[:, :LANES]
    dg = lo[:, LANES:]

    z = w0_ref[...] + _dot(jnp.tanh(dwa), wup_ref[...])
    logw = -math.exp(-0.5) * jax.nn.sigmoid(z)
    a = jax.nn.sigmoid(a0_ref[...] + _dot(dwa, aup_ref[...]))
    g = _dot(jax.nn.sigmoid(dg), gup_ref[...])

    kk = k * kk_ref[...]
    kk = kk / jnp.maximum(jnp.sqrt(_head_sum(kk * kk)), 1e-12)
    k = k * (1.0 + (a - 1.0) * ka_ref[...])

    ti = lax.broadcasted_iota(jnp.int32, (C, C), 0)
    si = lax.broadcasted_iota(jnp.int32, (C, C), 1)
    tril = jnp.where(si <= ti, 1.0, 0.0).astype(BF16)
    lw_hi = logw.astype(BF16)
    lw_lo = (logw - lw_hi.astype(F32)).astype(BF16)
    cum = (jnp.dot(tril, lw_hi, preferred_element_type=F32)
           + jnp.dot(tril, lw_lo, preferred_element_type=F32))
    cum_last = cum[C - 1:C, :]
    e_neg = jnp.exp(-cum)
    abar = -kk * jnp.exp(cum - logw)
    rbar = r * jnp.exp(cum)
    kka = kk * a
    btil = kka * e_neg
    ktil = k * e_neg
    gamma_end = jnp.exp(cum_last)
    e_end = gamma_end * e_neg
    bend = kka * e_end
    kend = k * e_end
    return dict(r=r, k=k, v=v, g=g, abar=abar, rbar=rbar, btil=btil, ktil=ktil, bend=bend, kend=kend,
                gamma_end=gamma_end)


def _rwkv_chains(seqs, state_ref):
    C = CHUNK
    n_groups = state_ref.shape[1]
    t_idx = lax.broadcasted_iota(jnp.int32, (C, GROUP_W), 0)
    lane = lax.broadcasted_iota(jnp.int32, (C, GROUP_W), 1)
    s_idx = lane & (RWKV_HD - 1)
    head_of_lane = lane >> HD_SHIFT
    strict = s_idx < t_idx
    incl = s_idx <= t_idx
    bd_r = lax.broadcasted_iota(jnp.int32, (GROUP_W, GROUP_W), 0) >> HD_SHIFT
    bd_c = lax.broadcasted_iota(jnp.int32, (GROUP_W, GROUP_W), 1) >> HD_SHIFT
    on_diag = bd_r == bd_c

    chains = [(bi, gi) for bi in range(len(seqs)) for gi in range(n_groups)]
    part = lambda name: [seqs[bi][name][:, gi * GROUP_W:(gi + 1) * GROUP_W] for bi, gi in chains]
    mm = functools.partial(_packed_mm, head_of_lane=head_of_lane)
    v_g = part("v")
    lhs = [jnp.concatenate([p, q], axis=0) for p, q in zip(part("abar"), part("rbar"))]
    rhs = [jnp.concatenate([_block_diag(p, head_of_lane), _block_diag(q, head_of_lane)], axis=0)
           for p, q in zip(part("btil"), part("ktil"))]
    gram = [_dot_nt(p, q) for p, q in zip(lhs, rhs)]
    a_ab = [jnp.where(strict, gm[:C, :GROUP_W], 0.0) for gm in gram]
    a_ak = [jnp.where(strict, gm[:C, GROUP_W:], 0.0) for gm in gram]
    m_rb = [jnp.where(incl, gm[C:, :GROUP_W], 0.0) for gm in gram]
    m_rk = [jnp.where(incl, gm[C:, GROUP_W:], 0.0) for gm in gram]
    inv = _unit_lower_inverse(a_ab, t_idx, s_idx, head_of_lane)
    st = [state_ref[bi, gi] for bi, gi in chains]
    from_state = [_dot_nt(p, q) for p, q in zip(lhs, st)]
    local = mm([jnp.concatenate([p, q], axis=0) for p, q in zip(a_ak, m_rk)], v_g)
    u = mm(inv, [fs[:C] + lc[:C] for fs, lc in zip(from_state, local)])
    via_u = mm(m_rb, u)
    upd = [_dot_tn(jnp.concatenate([p, q], axis=0), jnp.concatenate([b, k], axis=0))
           for p, q, b, k in zip(u, v_g, part("bend"), part("kend"))]
    for (bi, gi), s_old, up, ge in zip(chains, st, upd, part("gamma_end")):
        state_ref[bi, gi] = s_old * ge + jnp.where(on_diag, up, 0.0)
    ys = [fs[C:] + p + lc[C:] for fs, p, lc in zip(from_state, via_u, local)]
    for bi, seq in enumerate(seqs):
        seq["y"] = jnp.concatenate(ys[bi * n_groups:(bi + 1) * n_groups], axis=1)


def _rwkv_finish(bi, seq, pr_ref, pk_ref, pv_ref, pl_ref, mur_ref, muk_ref, muv_ref, mul_ref,
                 wup_ref, aup_ref, gup_ref, w0_ref, a0_ref, kk_ref, ka_ref, rk_ref, lng_ref, lnb_ref,
                 o_ref, carry_r, carry_k, carry_v, carry_l, state_ref):
    y, r, k, v, g = seq["y"], seq["r"], seq["k"], seq["v"], seq["g"]
    inv_hd = 1.0 / RWKV_HD
    mu = _head_sum(y) * inv_hd
    d = y - mu
    var = _head_sum(d * d) * inv_hd
    yn = d * lax.rsqrt(var + RWKV_GN_EPS) * lng_ref[...] + lnb_ref[...]
    bonus = _head_sum(r * k * rk_ref[...]) * v
    o_ref[bi] = ((yn + bonus) * g).astype(o_ref.dtype)


def _rwkv(p_rwkv, col0, mu, w_up, w0, a_up, a0, g_up, k_k, k_a, r_k, ln_g, ln_b, nb=4):
    B, S, _ = p_rwkv.shape
    W = w0.shape[-1]
    C = CHUNK
    wup_pad = jnp.concatenate([w_up, jnp.zeros_like(a_up)], axis=0).astype(BF16)
    aup_pad = jnp.concatenate([jnp.zeros_like(w_up), a_up], axis=0).astype(BF16)
    row = lambda t: t.reshape(1, -1)
    vec = lambda n: pl.BlockSpec((1, n), lambda b, t: (0, 0))
    mat = lambda a: pl.BlockSpec(a.shape, lambda b, t: (0, 0))
    lw = mu.shape[-1] - 3 * W
    assert col0 % W == 0 and (col0 + 3 * W) % lw == 0 and B % nb == 0 and S % C == 0
    cblk = lambda c: pl.BlockSpec((nb, C, W), lambda b, t, c=c: (b, t, col0 // W + c))
    n_groups = W // GROUP_W
    gup = g_up.astype(BF16)
    return pl.pallas_call(
        _rwkv_kernel, grid=(B // nb, S // C),
        in_specs=[cblk(0), cblk(1), cblk(2),
                  pl.BlockSpec((nb, C, lw), lambda b, t: (b, t, (col0 + 3 * W) // lw)),
                  vec(W), vec(W), vec(W), vec(lw),
                  mat(wup_pad), mat(aup_pad), mat(gup),
                  vec(W), vec(W), vec(W), vec(W), vec(W), vec(W), vec(W)],
        out_specs=pl.BlockSpec((nb, C, W), lambda b, t: (b, t, 0)),
        out_shape=jax.ShapeDtypeStruct((B, S, W), BF16),
        scratch_shapes=[pltpu.VMEM((nb, 8, W), F32), pltpu.VMEM((nb, 8, W), F32), pltpu.VMEM((nb, 8, W), F32),
                        pltpu.VMEM((nb, 8, lw), F32),
                        pltpu.VMEM((nb, n_groups, GROUP_W, GROUP_W), F32)],
        compiler_params=_params("parallel", "arbitrary"), name="rwkv7",
    )(p_rwkv, p_rwkv, p_rwkv, p_rwkv,
      row(mu[:W]), row(mu[W:2 * W]), row(mu[2 * W:3 * W]), row(mu[3 * W:]),
      wup_pad, aup_pad, gup,
      row(w0), row(a0), row(k_k), row(k_a), row(r_k), row(ln_g), row(ln_b))


def _merge_kernel(yr_ref, yw_ref, x_ref, wg_ref, wr_ref, ww_ref, wo_ref, g_ref, b_ref, o_ref, *, alpha):
    tm, D = x_ref.shape
    half = tm // 2

    def branches(rs):
        return (jnp.dot(yr_ref[rs, :], wr_ref[...], preferred_element_type=F32),
                jnp.dot(yw_ref[rs, :], ww_ref[...], preferred_element_type=F32),
                jnp.dot(x_ref[rs, :].astype(BF16), wg_ref[...], preferred_element_type=F32))

    def mix(rs, up):
        gate = jax.nn.sigmoid(up[2])
        return (gate[:, :D] * up[0] + gate[:, D:] * up[1]).astype(BF16)

    def out_proj(m):
        return jnp.dot(m, wo_ref[...], preferred_element_type=F32)

    def finish(rs, h):
        o_ref[rs, :] = _ln(alpha * x_ref[rs, :] + h, g_ref[...], b_ref[...])

    rows_a, rows_b = slice(0, half), slice(half, tm)
    up_a = branches(rows_a)
    up_b = branches(rows_b)
    h_a = out_proj(mix(rows_a, up_a))
    h_b = out_proj(mix(rows_b, up_b))
    finish(rows_a, h_a)
    finish(rows_b, h_b)


def _merge(y_ret, y_rwkv, x, w_gate, w_ret_up, w_rwkv_up, w_out, ln_g, ln_b, alpha, tm=512):
    T, D = x.shape
    W = y_ret.shape[1]
    full = lambda a: pl.BlockSpec(a.shape, lambda i: (0, 0))
    rowblk = lambda n: pl.BlockSpec((tm, n), lambda i: (i, 0))
    return pl.pallas_call(
        functools.partial(_merge_kernel, alpha=alpha), grid=(T // tm,),
        in_specs=[rowblk(W), rowblk(W), rowblk(D),
                  full(w_gate), full(w_ret_up), full(w_rwkv_up), full(w_out),
                  pl.BlockSpec((1, D), lambda i: (0, 0)), pl.BlockSpec((1, D), lambda i: (0, 0))],
        out_specs=rowblk(D),
        out_shape=jax.ShapeDtypeStruct((T, D), F32),
        compiler_params=_params("parallel"), name="merge_out_ln",
    )(y_ret, y_rwkv, x, w_gate, w_ret_up, w_rwkv_up, w_out, ln_g.reshape(1, D), ln_b.reshape(1, D))


def _xattn_kernel(x_ref, kv_ref, wq_ref, wo_ref, g_ref, b_ref, rwt_ref, rb_ref, o_ref, cls_ref, *, alpha):
    tm, D = x_ref.shape[1:]
    hd = D // XA_HEADS
    half = tm // 2

    def project(rs):
        return jnp.dot(x_ref[0, rs, :].astype(BF16), wq_ref[...], preferred_element_type=F32)

    def scores(q):
        return [_dot_nt(q[:, h * hd:(h + 1) * hd], kv_ref[0, :, h * hd:(h + 1) * hd]) * (hd ** -0.5)
                for h in range(XA_HEADS)]

    def attend(s_heads):
        outs = []
        for h, s in enumerate(s_heads):
            e = jnp.exp(s - jnp.max(s, axis=-1, keepdims=True))
            probs = e / jnp.sum(e, axis=-1, keepdims=True)
            outs.append(jnp.dot(probs.astype(BF16), kv_ref[0, :, D + h * hd:D + (h + 1) * hd],
                                preferred_element_type=F32))
        return jnp.concatenate(outs, axis=1)

    def out_proj(o):
        return jnp.dot(o.astype(BF16), wo_ref[...], preferred_element_type=F32)

    def finish(rs, h_out):
        x2 = _ln(alpha * x_ref[0, rs, :] + h_out, g_ref[...], b_ref[...])
        comb, cls = _route(x2, rwt_ref[...], rb_ref[...])
        E = comb.shape[0]
        comb_pad = jnp.concatenate([comb, jnp.zeros((LANES - E, half), F32)], axis=0)
        o_ref[0, rs, :D] = x2
        o_ref[0, rs, D:] = comb_pad.T
        cls_ref[0, :, rs] = cls

    rows_a, rows_b = slice(0, half), slice(half, tm)
    s_a = scores(project(rows_a))
    q_b = project(rows_b)
    o_a = attend(s_a)
    s_b = scores(q_b)
    h_a = out_proj(o_a)
    o_b = attend(s_b)
    finish(rows_a, h_a)
    h_b = out_proj(o_b)
    finish(rows_b, h_b)


def _cross_attention(x, kv, wq, wo, ln_g, ln_b, router_w, router_bias, alpha, tm=512):
    B, S, D = x.shape
    M = kv.shape[1]
    E = router_w.shape[1]
    full = lambda a: pl.BlockSpec(a.shape, lambda b, i: (0, 0))
    n_t = S // tm
    return pl.pallas_call(
        functools.partial(_xattn_kernel, alpha=alpha), grid=(B, n_t),
        in_specs=[pl.BlockSpec((1, tm, D), lambda b, i: (b, i, 0)),
                  pl.BlockSpec((1, M, 2 * D), lambda b, i: (b, 0, 0)),
                  full(wq), full(wo),
                  pl.BlockSpec((1, D), lambda b, i: (0, 0)), pl.BlockSpec((1, D), lambda b, i: (0, 0)),
                  pl.BlockSpec((D, LANES), lambda b, i: (0, 0)), pl.BlockSpec((E, 1), lambda b, i: (0, 0))],
        out_specs=[pl.BlockSpec((1, tm, D + LANES), lambda b, i: (b, i, 0)),
                   pl.BlockSpec((1, 1, tm), lambda b, i: (b * n_t + i, 0, 0))],
        out_shape=[jax.ShapeDtypeStruct((B, S, D + LANES), F32),
                   jax.ShapeDtypeStruct((B * n_t, 1, tm), jnp.int32)],
        compiler_params=_params("parallel", "parallel"), name="xattn_ln_route",
    )(x, kv, wq, wo, ln_g.reshape(1, D), ln_b.reshape(1, D),
      jnp.pad(router_w, ((0, 0), (0, LANES - E))), router_bias.reshape(E, 1))


def _expert_pairs(per_group):
    todo = [(i, j) for i in range(per_group) for j in range(i + 1, per_group)]
    order = [todo.pop(0)]
    while todo:
        nxt = next((p for p in todo if set(p) & set(order[-1])), todo[0])
        todo.remove(nxt)
        order.append(nxt)
    return order


def _split_bf16(v):
    hi = v.astype(BF16)
    return hi, (v - hi.astype(F32)).astype(BF16)


def _route(x, w, bias):
    E = bias.shape[0]
    per_group = E // N_GROUPS
    x_hi, x_lo = _split_bf16(x)
    w_hi, w_lo = _split_bf16(w)
    logits = (jnp.dot(x_hi, w_hi, preferred_element_type=F32) + jnp.dot(x_hi, w_lo, preferred_element_type=F32)
              + jnp.dot(x_lo, w_hi, preferred_element_type=F32)).T[:E]
    aff = jax.nn.sigmoid(logits)
    choice = aff + bias
    rows = [choice[e:e + 1, :] for e in range(E)]
    scores = []
    for gidx in range(N_GROUPS):
        members = rows[gidx * per_group:(gidx + 1) * per_group]
        best = None
        for i in range(per_group):
            for j in range(i + 1, per_group):
                pair = members[i] + members[j]
                best = pair if best is None else jnp.maximum(best, pair)
        scores.append(best)
    top = scores[0]
    for s in scores[1:]:
        top = jnp.maximum(top, s)
    taken = jnp.zeros_like(top, dtype=jnp.bool_)
    in_best = []
    for s in scores:
        hit = jnp.logical_and(s == top, jnp.logical_not(taken))
        in_best.append(hit)
        taken = jnp.logical_or(taken, hit)
    sel_w = []
    cls = jnp.zeros(top.shape, jnp.int32)
    pairs = _expert_pairs(per_group)
    for gidx in range(N_GROUPS):
        members = rows[gidx * per_group:(gidx + 1) * per_group]
        chosen = []
        for i in range(per_group):
            rank = jnp.zeros_like(top)
            for j in range(per_group):
                if j == i:
                    continue
                ahead = (members[j] > members[i]) if j > i else (members[j] >= members[i])
                rank = rank + jnp.where(ahead, 1.0, 0.0)
            chosen.append(jnp.logical_and(in_best[gidx], rank < TOP_K))
            e = gidx * per_group + i
            sel_w.append(jnp.where(chosen[i], aff[e:e + 1, :], 0.0))
        for p, (i, j) in enumerate(pairs):
            cls = jnp.where(jnp.logical_and(chosen[i], chosen[j]), gidx * len(pairs) + p, cls)
    total = sel_w[0]
    for w in sel_w[1:]:
        total = total + w
    return jnp.concatenate(sel_w, axis=0) / total, cls


def _rank_kernel(cls_ref, rank_ref, cnt_ref, base_ref):
    @pl.when(pl.program_id(0) == 0)
    def _():
        base_ref[...] = jnp.zeros_like(base_ref)

    cls = cls_ref[0]
    tm = cls.shape[-1]
    n_cls = base_ref.shape[0]
    onehot = lax.broadcasted_iota(jnp.int32, (n_cls, tm), 0) == cls
    earlier = (lax.broadcasted_iota(jnp.int32, (tm, tm), 0)
               < lax.broadcasted_iota(jnp.int32, (tm, tm), 1))
    ones = jnp.where(onehot, 1.0, 0.0)
    before = jnp.dot(ones.astype(BF16), jnp.where(earlier, 1.0, 0.0).astype(BF16),
                     preferred_element_type=F32)
    base = base_ref[...]
    rank = jnp.sum(jnp.where(onehot, base + before, 0.0), axis=0, keepdims=True)
    rank_ref[0] = rank.astype(jnp.int32)
    total = base + jnp.sum(ones, axis=1, keepdims=True)
    base_ref[...] = total
    cnt_ref[...] = jnp.broadcast_to(total, cnt_ref.shape)


def _class_ranks(cls, n_cls_pad):
    n_t, _, tm = cls.shape
    return pl.pallas_call(
        _rank_kernel, grid=(n_t,),
        in_specs=[pl.BlockSpec((1, 1, tm), lambda i: (i, 0, 0))],
        out_specs=[pl.BlockSpec((1, 1, tm), lambda i: (i, 0, 0)),
                   pl.BlockSpec((n_cls_pad, LANES), lambda i: (0, 0))],
        out_shape=[jax.ShapeDtypeStruct((n_t, 1, tm), jnp.int32),
                   jax.ShapeDtypeStruct((n_cls_pad, LANES), F32)],
        scratch_shapes=[pltpu.VMEM((n_cls_pad, 1), F32)],
        compiler_params=_params("arbitrary"), name="moe_class_rank",
    )(cls)


def _row_copy(src, src_row, dst, dst_row, sem):
    return pltpu.make_async_copy(src.at[pl.ds(src_row, 1)], dst.at[pl.ds(dst_row, 1)], sem)


def _for_each_row(n_rows, fn):
    def group(g, carry):
        r0 = pl.multiple_of(g * SUBLANES, SUBLANES)
        for j in range(SUBLANES):
            fn(r0 + j)
        return carry

    lax.fori_loop(0, n_rows // SUBLANES, group, 0)


def _dispatch_kernel(dest_ref, x_ref, o_hbm, sem):
    tm = x_ref.shape[0]
    base = pl.program_id(0) * tm
    copy = lambda r: _row_copy(x_ref, r, o_hbm, dest_ref[base + r], sem)
    _for_each_row(tm, lambda r: copy(r).start())
    _for_each_row(tm, lambda r: copy(r).wait())


def _dispatch(xa, dest, tm=512):
    T, W = xa.shape
    return pl.pallas_call(
        _dispatch_kernel,
        grid_spec=pltpu.PrefetchScalarGridSpec(
            num_scalar_prefetch=1, grid=(T // tm,),
            in_specs=[pl.BlockSpec((tm, W), lambda i, dest: (i, 0))],
            out_specs=pl.BlockSpec(memory_space=pl.ANY),
            scratch_shapes=[pltpu.SemaphoreType.DMA(())]),
        out_shape=jax.ShapeDtypeStruct((T, W), xa.dtype),
        compiler_params=_params("arbitrary"), name="moe_dispatch",
    )(dest, xa)


def _undispatch_kernel(dest_ref, y_hbm, o_ref, sem):
    tm = o_ref.shape[0]
    base = pl.program_id(0) * tm
    copy = lambda r: _row_copy(y_hbm, dest_ref[base + r], o_ref, r, sem)
    _for_each_row(tm, lambda r: copy(r).start())
    _for_each_row(tm, lambda r: copy(r).wait())


def _undispatch(ys, dest, tm=512):
    T, D = ys.shape
    return pl.pallas_call(
        _undispatch_kernel,
        grid_spec=pltpu.PrefetchScalarGridSpec(
            num_scalar_prefetch=1, grid=(T // tm,),
            in_specs=[pl.BlockSpec(memory_space=pl.ANY)],
            out_specs=pl.BlockSpec((tm, D), lambda i, dest: (i, 0)),
            scratch_shapes=[pltpu.SemaphoreType.DMA(())]),
        out_shape=jax.ShapeDtypeStruct((T, D), ys.dtype),
        compiler_params=_params("arbitrary"), name="moe_undispatch",
    )(dest, ys)


STEP_FIRST, STEP_LAST, STEP_VALID = 1, 2, 4


def _moe_kernel(tile_ref, exp_ref, flag_ref, xs_ref, wg_ref, wu_ref, wd_ref, g_ref, b_ref, o_ref,
                xbf_ref, acc_ref, *, alpha):
    step = pl.program_id(0)
    flags = flag_ref[step]
    e = exp_ref[step]
    D = o_ref.shape[-1]

    @pl.when((flags & STEP_FIRST) != 0)
    def _():
        xbf_ref[...] = xs_ref[:, :D].astype(BF16)
        acc_ref[...] = jnp.zeros_like(acc_ref)

    @pl.when((flags & STEP_VALID) != 0)
    def _():
        half = xbf_ref.shape[0] // 2

        def up(rs):
            xb = xbf_ref[rs, :]
            return (jnp.dot(xb, wg_ref[0], preferred_element_type=F32),
                    jnp.dot(xb, wu_ref[0], preferred_element_type=F32))

        def act(hg, hu):
            return (hg * jax.nn.sigmoid(hg) * hu).astype(BF16)

        def down(h):
            return jnp.dot(h, wd_ref[0], preferred_element_type=F32)

        def accumulate(rs, y):
            comb = xs_ref[rs, D:]
            lane = lax.broadcasted_iota(jnp.int32, comb.shape, 1)
            w_e = jnp.sum(jnp.where(lane == e, comb, 0.0), axis=-1, keepdims=True)
            acc_ref[rs, :] += w_e * y

        rows_a, rows_b = slice(0, half), slice(half, 2 * half)
        up_a = up(rows_a)
        up_b = up(rows_b)
        y_a = down(act(*up_a))
        y_b = down(act(*up_b))
        accumulate(rows_a, y_a)
        accumulate(rows_b, y_b)

    @pl.when((flags & STEP_LAST) != 0)
    def _():
        o_ref[...] = _ln(alpha * xs_ref[:, :D] + acc_ref[...], g_ref[...], b_ref[...])


def _moe_steps(counts, n_experts, n_tiles, tm):
    n_cls = counts.shape[0]
    per_group = n_experts // N_GROUPS
    pairs = _expert_pairs(per_group)
    uses = [[0.0] * n_experts for _ in range(n_cls)]
    for c in range(n_cls):
        gidx, (i, j) = c // len(pairs), pairs[c % len(pairs)]
        uses[c][gidx * per_group + i] = 1.0
        uses[c][gidx * per_group + j] = 1.0
    uses = jnp.array(uses, F32)
    ends = jnp.cumsum(counts)
    starts = ends - counts
    lo = (jnp.arange(n_tiles, dtype=jnp.int32) * tm)[:, None]
    present = jnp.logical_and(starts[None, :] < lo + tm, ends[None, :] > lo)
    need = jnp.dot(present.astype(F32), uses) > 0
    odd = (jnp.arange(n_tiles, dtype=jnp.int32) % 2 == 1)[:, None]
    need = jnp.where(odd, need[:, ::-1], need).reshape(-1)
    n_steps = min(n_tiles * n_experts, TOP_K * (n_tiles + n_cls - 1))
    n_valid = jnp.sum(need.astype(jnp.int32))
    idx = jnp.nonzero(need, size=n_steps, fill_value=0)[0].astype(jnp.int32)
    k = jnp.arange(n_steps, dtype=jnp.int32)
    valid = k < n_valid
    idx = jnp.where(valid, idx, idx[n_valid - 1])
    tile, expert = idx // n_experts, idx % n_experts
    expert = jnp.where(tile % 2 == 1, n_experts - 1 - expert, expert)
    prev_tile = jnp.concatenate([jnp.full((1,), -1, jnp.int32), tile[:-1]])
    next_tile = jnp.concatenate([tile[1:], jnp.full((1,), -1, jnp.int32)])
    first = jnp.logical_and(valid, tile != prev_tile)
    last = jnp.logical_and(valid, jnp.logical_or(tile != next_tile, k == n_valid - 1))
    flags = (first * STEP_FIRST + last * STEP_LAST + valid * STEP_VALID).astype(jnp.int32)
    return tile, expert, flags


def _moe_experts(xs, steps, w_gate, w_up, w_down, ln_g, ln_b, alpha, tm):
    T, W = xs.shape
    D = W - LANES
    E, _, F = w_gate.shape
    tile, expert, flags = steps
    return pl.pallas_call(
        functools.partial(_moe_kernel, alpha=alpha),
        grid_spec=pltpu.PrefetchScalarGridSpec(
            num_scalar_prefetch=3, grid=(tile.shape[0],),
            in_specs=[pl.BlockSpec((tm, W), lambda s, t, e, f: (t[s], 0)),
                      pl.BlockSpec((1, D, F), lambda s, t, e, f: (e[s], 0, 0)),
                      pl.BlockSpec((1, D, F), lambda s, t, e, f: (e[s], 0, 0)),
                      pl.BlockSpec((1, F, D), lambda s, t, e, f: (e[s], 0, 0)),
                      pl.BlockSpec((1, D), lambda s, t, e, f: (0, 0)),
                      pl.BlockSpec((1, D), lambda s, t, e, f: (0, 0))],
            out_specs=pl.BlockSpec((tm, D), lambda s, t, e, f: (t[s], 0)),
            scratch_shapes=[pltpu.VMEM((tm, D), BF16), pltpu.VMEM((tm, D), F32)]),
        out_shape=jax.ShapeDtypeStruct((T, D), F32),
        compiler_params=_params("arbitrary"), name="moe_experts_ln",
    )(tile, expert, flags, xs, w_gate, w_up, w_down, ln_g.reshape(1, D), ln_b.reshape(1, D))


def _moe(xa, cls, w_gate, w_up, w_down, ln_g, ln_b, alpha, tm=512):
    T = xa.shape[0]
    E = w_gate.shape[0]
    n_cls = N_GROUPS * len(_expert_pairs(E // N_GROUPS))
    rank, cnt = _class_ranks(cls, -(-n_cls // 8) * 8)
    counts = cnt[:n_cls, 0].astype(jnp.int32)
    starts = jnp.cumsum(counts) - counts
    dest = (starts[cls.reshape(T)] + rank.reshape(T)).astype(jnp.int32)
    xs = _dispatch(xa, dest)
    ys = _moe_experts(xs, _moe_steps(counts, E, T // tm, tm), w_gate, w_up, w_down, ln_g, ln_b, alpha, tm)
    return _undispatch(ys, dest)


def _rotary_tables(S):
    half = RET_HD // 2
    inv_freq = ROPE_BASE ** (-jnp.arange(half, dtype=F32) / half)
    ang = jnp.arange(S).astype(F32)[:, None] * inv_freq[None, :]
    cos, sin = jnp.cos(ang), jnp.sin(ang)
    return jnp.concatenate([cos, cos], axis=1), jnp.concatenate([-sin, sin], axis=1)


def kernel(x, mem, ln_in_g, ln_in_b, router_w, router_bias, w_in, ret_gn_g, rwkv_mu, rwkv_w_up, rwkv_w0,
           rwkv_a_up, rwkv_a0, rwkv_g_up, rwkv_k_k, rwkv_k_a, rwkv_r_k, rwkv_ln_g, rwkv_ln_b,
           w_ret_up, w_rwkv_up, w_out, ln1_g, ln1_b, xa_wq, xa_wkv, xa_wo, ln2_g, ln2_b,
           moe_w_gate, moe_w_up, moe_w_down, ln3_g, ln3_b):
    B, S, D = x.shape
    T = B * S
    depth = w_in.shape[0]
    alpha = (2 * depth) ** 0.25
    ret_w = ret_gn_g.shape[-1]
    rwkv_w = rwkv_w0.shape[-1]
    ret_cols = 4 * ret_w
    rwkv_cols = rwkv_mu.shape[-1]
    n_proj = -(-(ret_cols + rwkv_cols) // IN_PROJ_TN) * IN_PROJ_TN
    cos2, sin2 = _rotary_tables(S)
    mem2 = mem.reshape(B * mem.shape[1], D)

    xs = _layer_norm(x.reshape(T, D), ln_in_g, ln_in_b)
    for l in range(depth):
        w_l = w_in[l]
        w_cat = jnp.concatenate([w_l[:, :ret_cols + rwkv_cols],
                                 jnp.zeros((D, n_proj - ret_cols - rwkv_cols), w_l.dtype)], axis=1).astype(BF16)
        p3 = _matmul(xs, w_cat, 1024, IN_PROJ_TN, BF16, "in_proj").reshape(B, S, n_proj)
        y_ret = _retention(p3, cos2, sin2, ret_gn_g[l])
        y_rwkv = _rwkv(p3, ret_cols, rwkv_mu[l], rwkv_w_up[l], rwkv_w0[l], rwkv_a_up[l],
                       rwkv_a0[l], rwkv_g_up[l], rwkv_k_k[l], rwkv_k_a[l], rwkv_r_k[l].reshape(-1),
                       rwkv_ln_g[l], rwkv_ln_b[l])
        xs = _merge(y_ret.reshape(T, ret_w), y_rwkv.reshape(T, rwkv_w), xs, w_l[:, ret_cols + rwkv_cols:].astype(BF16),
                    w_ret_up[l].astype(BF16), w_rwkv_up[l].astype(BF16), w_out[l].astype(BF16),
                    ln1_g[l], ln1_b[l], alpha)
        kv = _matmul(mem2, xa_wkv[l].astype(BF16), 512, 512, BF16, "xattn_kv")
        xa, cls = _cross_attention(xs.reshape(B, S, D), kv.reshape(B, -1, 2 * D), xa_wq[l].astype(BF16),
                                   xa_wo[l].astype(BF16), ln2_g[l], ln2_b[l], router_w, router_bias, alpha)
        xs = _moe(xa.reshape(T, D + LANES), cls, moe_w_gate[l].astype(BF16), moe_w_up[l].astype(BF16),
                  moe_w_down[l].astype(BF16), ln3_g[l], ln3_b[l], alpha)
    return xs.reshape(B, S, D)
```

```python
import functools
import math

import jax
import jax.numpy as jnp
from jax import lax
from jax.experimental import pallas as pl
from jax.experimental.pallas import tpu as pltpu

F32 = jnp.float32
BF16 = jnp.bfloat16

LANES = 128
SUBLANES = 8
VMEM_LIMIT = 56 * 1024 * 1024

CHUNK = 64
RET_HEADS = 4
RET_HD = 128
ROPE_BASE = 10000.0
RWKV_HD = 64
HD_SHIFT = 6
RWKV_GN_EPS = 64e-5
XA_HEADS = 4
XA_ROW_PARTS = 4
N_GROUPS = 4
TOP_K = 2
LN_EPS = 1e-5
IN_PROJ_TN = 2048
HEADS_PER_GROUP = 4
GROUP_W = HEADS_PER_GROUP * RWKV_HD


def _params(*sem):
    return pltpu.CompilerParams(dimension_semantics=sem, vmem_limit_bytes=VMEM_LIMIT)


def _ln(v, g, b, eps=LN_EPS):
    mu = jnp.mean(v, axis=-1, keepdims=True)
    d = v - mu
    var = jnp.mean(d * d, axis=-1, keepdims=True)
    return d * lax.rsqrt(var + eps) * g + b


def _dot(a, b):
    return jnp.dot(a.astype(BF16), b.astype(BF16), preferred_element_type=F32)


def _dot_nt(a, b):
    return lax.dot_general(a.astype(BF16), b.astype(BF16), (((1,), (1,)), ((), ())),
                           preferred_element_type=F32)


def _dot_tn(a, b):
    return lax.dot_general(a.astype(BF16), b.astype(BF16), (((0,), (0,)), ((), ())),
                           preferred_element_type=F32)


def _mm_kernel(a_ref, w_ref, o_ref, abf_ref):
    @pl.when(pl.program_id(1) == 0)
    def _():
        abf_ref[...] = a_ref[...].astype(BF16)

    o_ref[...] = jnp.dot(abf_ref[...], w_ref[...], preferred_element_type=F32).astype(o_ref.dtype)


def _matmul(a, w, tm, tn, out_dtype, name):
    M, K = a.shape
    N = w.shape[1]
    return pl.pallas_call(
        _mm_kernel, grid=(M // tm, N // tn),
        in_specs=[pl.BlockSpec((tm, K), lambda i, j: (i, 0)),
                  pl.BlockSpec((K, tn), lambda i, j: (0, j))],
        out_specs=pl.BlockSpec((tm, tn), lambda i, j: (i, j)),
        out_shape=jax.ShapeDtypeStruct((M, N), out_dtype),
        scratch_shapes=[pltpu.VMEM((tm, K), BF16)],
        compiler_params=_params("parallel", "arbitrary"), name=name,
    )(a, w)


def _ln_mm_kernel(a_ref, g_ref, b_ref, w_ref, n_ref, o_ref, abf_ref):
    @pl.when(pl.program_id(1) == 0)
    def _():
        normed = _ln(a_ref[...], g_ref[...], b_ref[...])
        n_ref[...] = normed
        abf_ref[...] = normed.astype(BF16)

    o_ref[...] = jnp.dot(abf_ref[...], w_ref[...], preferred_element_type=F32).astype(o_ref.dtype)


def _ln_matmul(a, g, b, w, tm, tn, out_dtype, name):
    M, K = a.shape
    N = w.shape[1]
    return pl.pallas_call(
        _ln_mm_kernel, grid=(M // tm, N // tn),
        in_specs=[pl.BlockSpec((tm, K), lambda i, j: (i, 0)),
                  pl.BlockSpec((1, K), lambda i, j: (0, 0)),
                  pl.BlockSpec((1, K), lambda i, j: (0, 0)),
                  pl.BlockSpec((K, tn), lambda i, j: (0, j))],
        out_specs=[pl.BlockSpec((tm, K), lambda i, j: (i, 0)),
                   pl.BlockSpec((tm, tn), lambda i, j: (i, j))],
        out_shape=[jax.ShapeDtypeStruct((M, K), F32), jax.ShapeDtypeStruct((M, N), out_dtype)],
        scratch_shapes=[pltpu.VMEM((tm, K), BF16)],
        compiler_params=_params("parallel", "arbitrary"), name=name,
    )(a, g.reshape(1, K), b.reshape(1, K), w)


def _ret_kernel(q_ref, k_ref, v_ref, g_ref, cos_ref, sin_ref, gn_ref, o_ref, state_ref, *, n_chunks):
    @pl.when(pl.program_id(1) == 0)
    def _():
        state_ref[...] = jnp.zeros_like(state_ref)

    C, hd = CHUNK, RET_HD
    ti = lax.broadcasted_iota(jnp.int32, (C, C), 0)
    si = lax.broadcasted_iota(jnp.int32, (C, C), 1)
    dist = jnp.abs(ti - si).astype(F32)
    cpos = lax.broadcasted_iota(jnp.int32, (C, 1), 0).astype(F32)
    log_gamma = [math.log(1.0 - 2.0 ** (-5.0 - h)) for h in range(RET_HEADS)]
    inner_mask = [jnp.exp(lg * dist) for lg in log_gamma]
    k_decay = [jnp.exp(lg * (C - 1.0 - cpos)) for lg in log_gamma]
    q_decay = [jnp.exp(lg * (cpos + 1.0)) for lg in log_gamma]
    pairs = [(c, h) for c in range(n_chunks) for h in range(RET_HEADS)]
    rows = lambda c: slice(c * C, (c + 1) * C)
    cols = lambda h: slice(h * hd, (h + 1) * hd)

    def rotated(ref, c, h):
        t = ref[0, rows(c), cols(h)].astype(F32)
        return t * cos_ref[rows(c), :] + pltpu.roll(t, hd // 2, axis=1) * sin_ref[rows(c), :]

    qr = [rotated(q_ref, c, h) for c, h in pairs]
    kr = [rotated(k_ref, c, h) * (hd ** -0.5) for c, h in pairs]
    v = [v_ref[0, rows(c), cols(h)] for c, h in pairs]
    scores = [_dot_nt(a, b) * inner_mask[h] for a, b, (c, h) in zip(qr, kr, pairs)]
    inner = [_dot(s, t) for s, t in zip(scores, v)]
    kv = [_dot_tn(b * k_decay[h], t) for b, t, (c, h) in zip(kr, v, pairs)]
    states = []
    for h in range(RET_HEADS):
        state = state_ref[h]
        for c in range(n_chunks):
            states.append(((c, h), state))
            state = math.exp(log_gamma[h] * C) * state + kv[pairs.index((c, h))]
        state_ref[h] = state
    entering = dict(states)
    cross = [_dot(a * q_decay[h], entering[(c, h)]) for a, (c, h) in zip(qr, pairs)]
    for y_in, y_cr, (c, h) in zip(inner, cross, pairs):
        y = y_in + y_cr
        mu = jnp.mean(y, axis=-1, keepdims=True)
        d = y - mu
        var = jnp.mean(d * d, axis=-1, keepdims=True)
        yn = d * lax.rsqrt(var + LN_EPS) * gn_ref[:, cols(h)]
        g = g_ref[0, rows(c), cols(h)].astype(F32)
        o_ref[0, rows(c), cols(h)] = (g * jax.nn.sigmoid(g) * yn).astype(o_ref.dtype)


def _retention(p_ret, cos2, sin2, gn_g, ts=256):
    B, S, _ = p_ret.shape
    W = gn_g.shape[-1]
    blk = lambda c: pl.BlockSpec((1, ts, W), lambda b, n, c=c: (b, n, c))
    return pl.pallas_call(
        functools.partial(_ret_kernel, n_chunks=ts // CHUNK), grid=(B, S // ts),
        in_specs=[blk(0), blk(1), blk(2), blk(3),
                  pl.BlockSpec((ts, RET_HD), lambda b, n: (n, 0)),
                  pl.BlockSpec((ts, RET_HD), lambda b, n: (n, 0)),
                  pl.BlockSpec((1, W), lambda b, n: (0, 0))],
        out_specs=pl.BlockSpec((1, ts, W), lambda b, n: (b, n, 0)),
        out_shape=jax.ShapeDtypeStruct((B, S, W), BF16),
        scratch_shapes=[pltpu.VMEM((RET_HEADS, RET_HD, RET_HD), F32)],
        compiler_params=_params("parallel", "arbitrary"), name="retention",
    )(p_ret, p_ret, p_ret, p_ret, cos2, sin2, gn_g.reshape(1, W))


def _head_sum(x):
    R, W = x.shape
    lo = lax.broadcasted_iota(jnp.int32, (R, LANES), 1) < RWKV_HD
    outs = []
    for j in range(W // LANES):
        xs = x[:, j * LANES:(j + 1) * LANES]
        s_lo = jnp.sum(jnp.where(lo, xs, 0.0), axis=-1, keepdims=True)
        s_hi = jnp.sum(jnp.where(lo, 0.0, xs), axis=-1, keepdims=True)
        outs.append(jnp.where(lo, s_lo, s_hi))
    return jnp.concatenate(outs, axis=1)


def _block_diag(x, head_of_lane):
    parts = [jnp.where(head_of_lane == h, x, 0.0) for h in range(HEADS_PER_GROUP)]
    return jnp.concatenate(parts, axis=0).astype(BF16)


def _packed_mm(xs, ys, head_of_lane):
    bds = [_block_diag(y, head_of_lane) for y in ys]
    return [jnp.dot(x.astype(BF16), bd, preferred_element_type=F32) for x, bd in zip(xs, bds)]


def _unit_lower_inverse(a_list, t_idx, s_idx, head_of_lane):
    mm = functools.partial(_packed_mm, head_of_lane=head_of_lane)
    add = lambda xs, ys: [x + y for x, y in zip(xs, ys)]
    lower = s_idx < t_idx
    same4 = (t_idx >> 2) == (s_idx >> 2)
    same16 = (t_idx >> 4) == (s_idx >> 4)
    m1 = lower & same4
    m2 = lower & same16 & jnp.logical_not(same4)
    m3 = lower & jnp.logical_not(same16)
    eye = jnp.where(t_idx == s_idx, 1.0, 0.0)
    a1 = [jnp.where(m1, a, 0.0) for a in a_list]
    a1sq = mm(a1, a1)
    a1cu = mm(a1, a1sq)
    inv = [eye + p + q + r for p, q, r in zip(a1, a1sq, a1cu)]
    for m in (m2, m3):
        n = mm(inv, [jnp.where(m, a, 0.0) for a in a_list])
        x = add(inv, mm(mm(n, n), inv))
        inv = add(x, mm(n, x))
    return inv


def _rwkv_kernel(*refs):
    state_ref = refs[-1]

    @pl.when(pl.program_id(1) == 0)
    def _():
        for ref in refs[-5:]:
            ref[...] = jnp.zeros_like(ref)

    seqs = [_rwkv_prepare(bi, *refs) for bi in range(state_ref.shape[0])]
    _rwkv_chains(seqs, state_ref)
    for bi, seq in enumerate(seqs):
        _rwkv_finish(bi, seq, *refs)


def _rwkv_prepare(bi, pr_ref, pk_ref, pv_ref, pl_ref, mur_ref, muk_ref, muv_ref, mul_ref,
                  wup_ref, aup_ref, gup_ref, w0_ref, a0_ref, kk_ref, ka_ref, rk_ref, lng_ref, lnb_ref,
                  o_ref, carry_r, carry_k, carry_v, carry_l, state_ref):
    C = CHUNK

    def shifted(p_ref, mu_ref, carry):
        p = p_ref[bi].astype(F32)
        row = lax.broadcasted_iota(jnp.int32, p.shape, 0)
        prev = jnp.where(row == 0, carry[bi, 0:1, :], pltpu.roll(p, 1, axis=0))
        carry[bi, 0:1, :] = p[C - 1:C, :]
        return p + mu_ref[...] * (prev - p)

    r = shifted(pr_ref, mur_ref, carry_r)
    k = shifted(pk_ref, muk_ref, carry_k)
    v = shifted(pv_ref, muv_ref, carry_v)
    lo = shifted(pl_ref, mul_ref, carry_l)
    dwa = lo[:, :LANES]
    dg = lo[:, LANES:]

    z = w0_ref[...] + _dot(jnp.tanh(dwa), wup_ref[...])
    logw = -math.exp(-0.5) * jax.nn.sigmoid(z)
    a = jax.nn.sigmoid(a0_ref[...] + _dot(dwa, aup_ref[...]))
    g = _dot(jax.nn.sigmoid(dg), gup_ref[...])

    kk = k * kk_ref[...]
    kk = kk / jnp.maximum(jnp.sqrt(_head_sum(kk * kk)), 1e-12)
    k = k * (1.0 + (a - 1.0) * ka_ref[...])

    ti = lax.broadcasted_iota(jnp.int32, (C, C), 0)
    si = lax.broadcasted_iota(jnp.int32, (C, C), 1)
    tril = jnp.where(si <= ti, 1.0, 0.0).astype(BF16)
    lw_hi = logw.astype(BF16)
    lw_lo = (logw - lw_hi.astype(F32)).astype(BF16)
    cum = (jnp.dot(tril, lw_hi, preferred_element_type=F32)
           + jnp.dot(tril, lw_lo, preferred_element_type=F32))
    cum_last = cum[C - 1:C, :]
    e_neg = jnp.exp(-cum)
    abar = -kk * jnp.exp(cum - logw)
    rbar = r * jnp.exp(cum)
    kka = kk * a
    btil = kka * e_neg
    ktil = k * e_neg
    gamma_end = jnp.exp(cum_last)
    e_end = gamma_end * e_neg
    bend = kka * e_end
    kend = k * e_end
    return dict(r=r, k=k, v=v, g=g, abar=abar, rbar=rbar, btil=btil, ktil=ktil, bend=bend, kend=kend,
                gamma_end=gamma_end)


def _rwkv_chains(seqs, state_ref):
    C = CHUNK
    n_groups = state_ref.shape[1]
    t_idx = lax.broadcasted_iota(jnp.int32, (C, GROUP_W), 0)
    lane = lax.broadcasted_iota(jnp.int32, (C, GROUP_W), 1)
    s_idx = lane & (RWKV_HD - 1)
    head_of_lane = lane >> HD_SHIFT
    strict = s_idx < t_idx
    incl = s_idx <= t_idx
    bd_r = lax.broadcasted_iota(jnp.int32, (GROUP_W, GROUP_W), 0) >> HD_SHIFT
    bd_c = lax.broadcasted_iota(jnp.int32, (GROUP_W, GROUP_W), 1) >> HD_SHIFT
    on_diag = bd_r == bd_c

    chains = [(bi, gi) for bi in range(len(seqs)) for gi in range(n_groups)]
    part = lambda name: [seqs[bi][name][:, gi * GROUP_W:(gi + 1) * GROUP_W] for bi, gi in chains]
    mm = functools.partial(_packed_mm, head_of_lane=head_of_lane)
    v_g = part("v")
    lhs = [jnp.concatenate([p, q], axis=0) for p, q in zip(part("abar"), part("rbar"))]
    rhs = [jnp.concatenate([_block_diag(p, head_of_lane), _block_diag(q, head_of_lane)], axis=0)
           for p, q in zip(part("btil"), part("ktil"))]
    gram = [_dot_nt(p, q) for p, q in zip(lhs, rhs)]
    a_ab = [jnp.where(strict, gm[:C, :GROUP_W], 0.0) for gm in gram]
    a_ak = [jnp.where(strict, gm[:C, GROUP_W:], 0.0) for gm in gram]
    m_rb = [jnp.where(incl, gm[C:, :GROUP_W], 0.0) for gm in gram]
    m_rk = [jnp.where(incl, gm[C:, GROUP_W:], 0.0) for gm in gram]
    inv = _unit_lower_inverse(a_ab, t_idx, s_idx, head_of_lane)
    st = [state_ref[bi, gi] for bi, gi in chains]
    from_state = [_dot_nt(p, q) for p, q in zip(lhs, st)]
    local = mm([jnp.concatenate([p, q], axis=0) for p, q in zip(a_ak, m_rk)], v_g)
    u = mm(inv, [fs[:C] + lc[:C] for fs, lc in zip(from_state, local)])
    via_u = mm(m_rb, u)
    upd = [_dot_tn(jnp.concatenate([p, q], axis=0), jnp.concatenate([b, k], axis=0))
           for p, q, b, k in zip(u, v_g, part("bend"), part("kend"))]
    for (bi, gi), s_old, up, ge in zip(chains, st, upd, part("gamma_end")):
        state_ref[bi, gi] = s_old * ge + jnp.where(on_diag, up, 0.0)
    ys = [fs[C:] + p + lc[C:] for fs, p, lc in zip(from_state, via_u, local)]
    for bi, seq in enumerate(seqs):
        seq["y"] = jnp.concatenate(ys[bi * n_groups:(bi + 1) * n_groups], axis=1)


def _rwkv_finish(bi, seq, pr_ref, pk_ref, pv_ref, pl_ref, mur_ref, muk_ref, muv_ref, mul_ref,
                 wup_ref, aup_ref, gup_ref, w0_ref, a0_ref, kk_ref, ka_ref, rk_ref, lng_ref, lnb_ref,
                 o_ref, carry_r, carry_k, carry_v, carry_l, state_ref):
    y, r, k, v, g = seq["y"], seq["r"], seq["k"], seq["v"], seq["g"]
    inv_hd = 1.0 / RWKV_HD
    mu = _head_sum(y) * inv_hd
    d = y - mu
    var = _head_sum(d * d) * inv_hd
    yn = d * lax.rsqrt(var + RWKV_GN_EPS) * lng_ref[...] + lnb_ref[...]
    bonus = _head_sum(r * k * rk_ref[...]) * v
    o_ref[bi] = ((yn + bonus) * g).astype(o_ref.dtype)


def _rwkv(p_rwkv, col0, mu, w_up, w0, a_up, a0, g_up, k_k, k_a, r_k, ln_g, ln_b, nb=4):
    B, S, _ = p_rwkv.shape
    W = w0.shape[-1]
    C = CHUNK
    wup_pad = jnp.concatenate([w_up, jnp.zeros_like(a_up)], axis=0).astype(BF16)
    aup_pad = jnp.concatenate([jnp.zeros_like(w_up), a_up], axis=0).astype(BF16)
    row = lambda t: t.reshape(1, -1)
    vec = lambda n: pl.BlockSpec((1, n), lambda b, t: (0, 0))
    mat = lambda a: pl.BlockSpec(a.shape, lambda b, t: (0, 0))
    lw = mu.shape[-1] - 3 * W
    assert col0 % W == 0 and (col0 + 3 * W) % lw == 0 and B % nb == 0 and S % C == 0
    cblk = lambda c: pl.BlockSpec((nb, C, W), lambda b, t, c=c: (b, t, col0 // W + c))
    n_groups = W // GROUP_W
    gup = g_up.astype(BF16)
    return pl.pallas_call(
        _rwkv_kernel, grid=(B // nb, S // C),
        in_specs=[cblk(0), cblk(1), cblk(2),
                  pl.BlockSpec((nb, C, lw), lambda b, t: (b, t, (col0 + 3 * W) // lw)),
                  vec(W), vec(W), vec(W), vec(lw),
                  mat(wup_pad), mat(aup_pad), mat(gup),
                  vec(W), vec(W), vec(W), vec(W), vec(W), vec(W), vec(W)],
        out_specs=pl.BlockSpec((nb, C, W), lambda b, t: (b, t, 0)),
        out_shape=jax.ShapeDtypeStruct((B, S, W), BF16),
        scratch_shapes=[pltpu.VMEM((nb, 8, W), F32), pltpu.VMEM((nb, 8, W), F32), pltpu.VMEM((nb, 8, W), F32),
                        pltpu.VMEM((nb, 8, lw), F32),
                        pltpu.VMEM((nb, n_groups, GROUP_W, GROUP_W), F32)],
        compiler_params=_params("parallel", "arbitrary"), name="rwkv7",
    )(p_rwkv, p_rwkv, p_rwkv, p_rwkv,
      row(mu[:W]), row(mu[W:2 * W]), row(mu[2 * W:3 * W]), row(mu[3 * W:]),
      wup_pad, aup_pad, gup,
      row(w0), row(a0), row(k_k), row(k_a), row(r_k), row(ln_g), row(ln_b))


def _merge_kernel(yr_ref, yw_ref, x_ref, wg_ref, wr_ref, ww_ref, wo_ref, g_ref, b_ref, o_ref, *, alpha):
    tm, D = x_ref.shape
    half = tm // 2

    def branches(rs):
        return (jnp.dot(yr_ref[rs, :], wr_ref[...], preferred_element_type=F32),
                jnp.dot(yw_ref[rs, :], ww_ref[...], preferred_element_type=F32),
                jnp.dot(x_ref[rs, :].astype(BF16), wg_ref[...], preferred_element_type=F32))

    def mix(rs, up):
        gate = jax.nn.sigmoid(up[2])
        return (gate[:, :D] * up[0] + gate[:, D:] * up[1]).astype(BF16)

    def out_proj(m):
        return jnp.dot(m, wo_ref[...], preferred_element_type=F32)

    def finish(rs, h):
        o_ref[rs, :] = _ln(alpha * x_ref[rs, :] + h, g_ref[...], b_ref[...])

    rows_a, rows_b = slice(0, half), slice(half, tm)
    up_a = branches(rows_a)
    up_b = branches(rows_b)
    h_a = out_proj(mix(rows_a, up_a))
    h_b = out_proj(mix(rows_b, up_b))
    finish(rows_a, h_a)
    finish(rows_b, h_b)


def _merge(y_ret, y_rwkv, x, w_gate, w_ret_up, w_rwkv_up, w_out, ln_g, ln_b, alpha, tm=512):
    T, D = x.shape
    W = y_ret.shape[1]
    full = lambda a: pl.BlockSpec(a.shape, lambda i: (0, 0))
    rowblk = lambda n: pl.BlockSpec((tm, n), lambda i: (i, 0))
    return pl.pallas_call(
        functools.partial(_merge_kernel, alpha=alpha), grid=(T // tm,),
        in_specs=[rowblk(W), rowblk(W), rowblk(D),
                  full(w_gate), full(w_ret_up), full(w_rwkv_up), full(w_out),
                  pl.BlockSpec((1, D), lambda i: (0, 0)), pl.BlockSpec((1, D), lambda i: (0, 0))],
        out_specs=rowblk(D),
        out_shape=jax.ShapeDtypeStruct((T, D), F32),
        compiler_params=_params("parallel"), name="merge_out_ln",
    )(y_ret, y_rwkv, x, w_gate, w_ret_up, w_rwkv_up, w_out, ln_g.reshape(1, D), ln_b.reshape(1, D))


def _xattn_kernel(x_ref, kv_ref, wq_ref, wo_ref, g_ref, b_ref, rwt_ref, rb_ref, o_ref, cls_ref, *, alpha):
    tm, D = x_ref.shape[1:]
    hd = D // XA_HEADS
    half = tm // XA_ROW_PARTS

    def project(rs):
        return jnp.dot(x_ref[0, rs, :].astype(BF16), wq_ref[...], preferred_element_type=F32)

    def scores(q):
        return [_dot_nt(q[:, h * hd:(h + 1) * hd], kv_ref[0, :, h * hd:(h + 1) * hd]) * (hd ** -0.5)
                for h in range(XA_HEADS)]

    def attend(s_heads):
        outs = []
        for h, s in enumerate(s_heads):
            e = jnp.exp(s - jnp.max(s, axis=-1, keepdims=True))
            probs = e / jnp.sum(e, axis=-1, keepdims=True)
            outs.append(jnp.dot(probs.astype(BF16), kv_ref[0, :, D + h * hd:D + (h + 1) * hd],
                                preferred_element_type=F32))
        return jnp.concatenate(outs, axis=1)

    def out_proj(o):
        return jnp.dot(o.astype(BF16), wo_ref[...], preferred_element_type=F32)

    def finish(rs, h_out):
        x2 = _ln(alpha * x_ref[0, rs, :] + h_out, g_ref[...], b_ref[...])
        comb, cls = _route(x2, rwt_ref[...], rb_ref[...])
        E = comb.shape[0]
        comb_pad = jnp.concatenate([comb, jnp.zeros((LANES - E, half), F32)], axis=0)
        o_ref[0, rs, :D] = x2
        o_ref[0, rs, D:] = comb_pad.T
        cls_ref[0, :, rs] = cls

    rows = [slice(i * half, (i + 1) * half) for i in range(tm // half)]
    s_cur, h_prev = scores(project(rows[0])), None
    for i, rs in enumerate(rows):
        q_next = project(rows[i + 1]) if i + 1 < len(rows) else None
        o = attend(s_cur)
        if q_next is not None:
            s_cur = scores(q_next)
        if h_prev is not None:
            finish(rows[i - 1], h_prev)
        h_prev = out_proj(o)
    finish(rows[-1], h_prev)


def _cross_attention(x, kv, wq, wo, ln_g, ln_b, router_w, router_bias, alpha, tm=1024):
    B, S, D = x.shape
    M = kv.shape[1]
    E = router_w.shape[1]
    full = lambda a: pl.BlockSpec(a.shape, lambda b, i: (0, 0))
    n_t = S // tm
    return pl.pallas_call(
        functools.partial(_xattn_kernel, alpha=alpha), grid=(B, n_t),
        in_specs=[pl.BlockSpec((1, tm, D), lambda b, i: (b, i, 0)),
                  pl.BlockSpec((1, M, 2 * D), lambda b, i: (b, 0, 0)),
                  full(wq), full(wo),
                  pl.BlockSpec((1, D), lambda b, i: (0, 0)), pl.BlockSpec((1, D), lambda b, i: (0, 0)),
                  pl.BlockSpec((D, LANES), lambda b, i: (0, 0)), pl.BlockSpec((E, 1), lambda b, i: (0, 0))],
        out_specs=[pl.BlockSpec((1, tm, D + LANES), lambda b, i: (b, i, 0)),
                   pl.BlockSpec((1, 1, tm), lambda b, i: (b * n_t + i, 0, 0))],
        out_shape=[jax.ShapeDtypeStruct((B, S, D + LANES), F32),
                   jax.ShapeDtypeStruct((B * n_t, 1, tm), jnp.int32)],
        compiler_params=_params("parallel", "parallel"), name="xattn_ln_route",
    )(x, kv, wq, wo, ln_g.reshape(1, D), ln_b.reshape(1, D),
      jnp.pad(router_w, ((0, 0), (0, LANES - E))), router_bias.reshape(E, 1))


def _expert_pairs(per_group):
    todo = [(i, j) for i in range(per_group) for j in range(i + 1, per_group)]
    order = [todo.pop(0)]
    while todo:
        nxt = next((p for p in todo if set(p) & set(order[-1])), todo[0])
        todo.remove(nxt)
        order.append(nxt)
    return order


def _split_bf16(v):
    hi = v.astype(BF16)
    return hi, (v - hi.astype(F32)).astype(BF16)


def _route(x, w, bias):
    E = bias.shape[0]
    per_group = E // N_GROUPS
    x_hi, x_lo = _split_bf16(x)
    w_hi, w_lo = _split_bf16(w)
    logits = (jnp.dot(x_hi, w_hi, preferred_element_type=F32) + jnp.dot(x_hi, w_lo, preferred_element_type=F32)
              + jnp.dot(x_lo, w_hi, preferred_element_type=F32)).T[:E]
    aff = jax.nn.sigmoid(logits)
    choice = aff + bias
    rows = [choice[e:e + 1, :] for e in range(E)]
    scores = []
    for gidx in range(N_GROUPS):
        members = rows[gidx * per_group:(gidx + 1) * per_group]
        best = None
        for i in range(per_group):
            for j in range(i + 1, per_group):
                pair = members[i] + members[j]
                best = pair if best is None else jnp.maximum(best, pair)
        scores.append(best)
    top = scores[0]
    for s in scores[1:]:
        top = jnp.maximum(top, s)
    taken = jnp.zeros_like(top, dtype=jnp.bool_)
    in_best = []
    for s in scores:
        hit = jnp.logical_and(s == top, jnp.logical_not(taken))
        in_best.append(hit)
        taken = jnp.logical_or(taken, hit)
    sel_w = []
    cls = jnp.zeros(top.shape, jnp.int32)
    pairs = _expert_pairs(per_group)
    for gidx in range(N_GROUPS):
        members = rows[gidx * per_group:(gidx + 1) * per_group]
        chosen = []
        for i in range(per_group):
            rank = jnp.zeros_like(top)
            for j in range(per_group):
                if j == i:
                    continue
                ahead = (members[j] > members[i]) if j > i else (members[j] >= members[i])
                rank = rank + jnp.where(ahead, 1.0, 0.0)
            chosen.append(jnp.logical_and(in_best[gidx], rank < TOP_K))
            e = gidx * per_group + i
            sel_w.append(jnp.where(chosen[i], aff[e:e + 1, :], 0.0))
        for p, (i, j) in enumerate(pairs):
            cls = jnp.where(jnp.logical_and(chosen[i], chosen[j]), gidx * len(pairs) + p, cls)
    total = sel_w[0]
    for w in sel_w[1:]:
        total = total + w
    return jnp.concatenate(sel_w, axis=0) / total, cls


def _rank_kernel(cls_ref, rank_ref, cnt_ref, base_ref):
    @pl.when(pl.program_id(0) == 0)
    def _():
        base_ref[...] = jnp.zeros_like(base_ref)

    cls = cls_ref[0]
    tm = cls.shape[-1]
    n_cls = base_ref.shape[0]
    onehot = lax.broadcasted_iota(jnp.int32, (n_cls, tm), 0) == cls
    earlier = (lax.broadcasted_iota(jnp.int32, (tm, tm), 0)
               < lax.broadcasted_iota(jnp.int32, (tm, tm), 1))
    ones = jnp.where(onehot, 1.0, 0.0)
    before = jnp.dot(ones.astype(BF16), jnp.where(earlier, 1.0, 0.0).astype(BF16),
                     preferred_element_type=F32)
    base = base_ref[...]
    rank = jnp.sum(jnp.where(onehot, base + before, 0.0), axis=0, keepdims=True)
    rank_ref[0] = rank.astype(jnp.int32)
    total = base + jnp.sum(ones, axis=1, keepdims=True)
    base_ref[...] = total
    cnt_ref[...] = jnp.broadcast_to(total, cnt_ref.shape)


def _class_ranks(cls, n_cls_pad):
    n_t, _, tm = cls.shape
    return pl.pallas_call(
        _rank_kernel, grid=(n_t,),
        in_specs=[pl.BlockSpec((1, 1, tm), lambda i: (i, 0, 0))],
        out_specs=[pl.BlockSpec((1, 1, tm), lambda i: (i, 0, 0)),
                   pl.BlockSpec((n_cls_pad, LANES), lambda i: (0, 0))],
        out_shape=[jax.ShapeDtypeStruct((n_t, 1, tm), jnp.int32),
                   jax.ShapeDtypeStruct((n_cls_pad, LANES), F32)],
        scratch_shapes=[pltpu.VMEM((n_cls_pad, 1), F32)],
        compiler_params=_params("arbitrary"), name="moe_class_rank",
    )(cls)


def _row_copy(src, src_row, dst, dst_row, sem):
    return pltpu.make_async_copy(src.at[pl.ds(src_row, 1)], dst.at[pl.ds(dst_row, 1)], sem)


def _for_each_row(n_rows, fn):
    def group(g, carry):
        r0 = pl.multiple_of(g * SUBLANES, SUBLANES)
        for j in range(SUBLANES):
            fn(r0 + j)
        return carry

    lax.fori_loop(0, n_rows // SUBLANES, group, 0)


def _dispatch_kernel(dest_ref, x_ref, o_hbm, sem):
    tm = x_ref.shape[0]
    base = pl.program_id(0) * tm
    copy = lambda r: _row_copy(x_ref, r, o_hbm, dest_ref[base + r], sem)
    _for_each_row(tm, lambda r: copy(r).start())
    _for_each_row(tm, lambda r: copy(r).wait())


def _dispatch(xa, dest, tm=512):
    T, W = xa.shape
    return pl.pallas_call(
        _dispatch_kernel,
        grid_spec=pltpu.PrefetchScalarGridSpec(
            num_scalar_prefetch=1, grid=(T // tm,),
            in_specs=[pl.BlockSpec((tm, W), lambda i, dest: (i, 0))],
            out_specs=pl.BlockSpec(memory_space=pl.ANY),
            scratch_shapes=[pltpu.SemaphoreType.DMA(())]),
        out_shape=jax.ShapeDtypeStruct((T, W), xa.dtype),
        compiler_params=_params("arbitrary"), name="moe_dispatch",
    )(dest, xa)


def _undispatch_kernel(dest_ref, y_hbm, o_ref, sem):
    tm = o_ref.shape[0]
    base = pl.program_id(0) * tm
    copy = lambda r: _row_copy(y_hbm, dest_ref[base + r], o_ref, r, sem)
    _for_each_row(tm, lambda r: copy(r).start())
    _for_each_row(tm, lambda r: copy(r).wait())


def _undispatch(ys, dest, tm=512):
    T, D = ys.shape
    return pl.pallas_call(
        _undispatch_kernel,
        grid_spec=pltpu.PrefetchScalarGridSpec(
            num_scalar_prefetch=1, grid=(T // tm,),
            in_specs=[pl.BlockSpec(memory_space=pl.ANY)],
            out_specs=pl.BlockSpec((tm, D), lambda i, dest: (i, 0)),
            scratch_shapes=[pltpu.SemaphoreType.DMA(())]),
        out_shape=jax.ShapeDtypeStruct((T, D), ys.dtype),
        compiler_params=_params("arbitrary"), name="moe_undispatch",
    )(dest, ys)


STEP_FIRST, STEP_LAST, STEP_VALID = 1, 2, 4


def _moe_kernel(tile_ref, exp_ref, flag_ref, xs_ref, wg_ref, wu_ref, wd_ref, g_ref, b_ref, o_ref,
                xbf_ref, acc_ref, *, alpha):
    step = pl.program_id(0)
    flags = flag_ref[step]
    e = exp_ref[step]
    D = o_ref.shape[-1]

    @pl.when((flags & STEP_FIRST) != 0)
    def _():
        xbf_ref[...] = xs_ref[:, :D].astype(BF16)
        acc_ref[...] = jnp.zeros_like(acc_ref)

    @pl.when((flags & STEP_VALID) != 0)
    def _():
        half = xbf_ref.shape[0] // 2

        def up(rs):
            xb = xbf_ref[rs, :]
            return (jnp.dot(xb, wg_ref[0], preferred_element_type=F32),
                    jnp.dot(xb, wu_ref[0], preferred_element_type=F32))

        def act(hg, hu):
            return (hg * jax.nn.sigmoid(hg) * hu).astype(BF16)

        def down(h):
            return jnp.dot(h, wd_ref[0], preferred_element_type=F32)

        def accumulate(rs, y):
            comb = xs_ref[rs, D:]
            lane = lax.broadcasted_iota(jnp.int32, comb.shape, 1)
            w_e = jnp.sum(jnp.where(lane == e, comb, 0.0), axis=-1, keepdims=True)
            acc_ref[rs, :] += w_e * y

        rows_a, rows_b = slice(0, half), slice(half, 2 * half)
        up_a = up(rows_a)
        up_b = up(rows_b)
        y_a = down(act(*up_a))
        y_b = down(act(*up_b))
        accumulate(rows_a, y_a)
        accumulate(rows_b, y_b)

    @pl.when((flags & STEP_LAST) != 0)
    def _():
        o_ref[...] = _ln(alpha * xs_ref[:, :D] + acc_ref[...], g_ref[...], b_ref[...])


def _moe_steps(counts, n_experts, n_tiles, tm):
    n_cls = counts.shape[0]
    per_group = n_experts // N_GROUPS
    pairs = _expert_pairs(per_group)
    uses = [[0.0] * n_experts for _ in range(n_cls)]
    for c in range(n_cls):
        gidx, (i, j) = c // len(pairs), pairs[c % len(pairs)]
        uses[c][gidx * per_group + i] = 1.0
        uses[c][gidx * per_group + j] = 1.0
    uses = jnp.array(uses, F32)
    ends = jnp.cumsum(counts)
    starts = ends - counts
    lo = (jnp.arange(n_tiles, dtype=jnp.int32) * tm)[:, None]
    present = jnp.logical_and(starts[None, :] < lo + tm, ends[None, :] > lo)
    need = jnp.dot(present.astype(F32), uses) > 0
    odd = (jnp.arange(n_tiles, dtype=jnp.int32) % 2 == 1)[:, None]
    need = jnp.where(odd, need[:, ::-1], need).reshape(-1)
    n_steps = min(n_tiles * n_experts, TOP_K * (n_tiles + n_cls - 1))
    n_valid = jnp.sum(need.astype(jnp.int32))
    idx = jnp.nonzero(need, size=n_steps, fill_value=0)[0].astype(jnp.int32)
    k = jnp.arange(n_steps, dtype=jnp.int32)
    valid = k < n_valid
    idx = jnp.where(valid, idx, idx[n_valid - 1])
    tile, expert = idx // n_experts, idx % n_experts
    expert = jnp.where(tile % 2 == 1, n_experts - 1 - expert, expert)
    prev_tile = jnp.concatenate([jnp.full((1,), -1, jnp.int32), tile[:-1]])
    next_tile = jnp.concatenate([tile[1:], jnp.full((1,), -1, jnp.int32)])
    first = jnp.logical_and(valid, tile != prev_tile)
    last = jnp.logical_and(valid, jnp.logical_or(tile != next_tile, k == n_valid - 1))
    flags = (first * STEP_FIRST + last * STEP_LAST + valid * STEP_VALID).astype(jnp.int32)
    return tile, expert, flags


def _moe_experts(xs, steps, w_gate, w_up, w_down, ln_g, ln_b, alpha, tm):
    T, W = xs.shape
    D = W - LANES
    E, _, F = w_gate.shape
    tile, expert, flags = steps
    return pl.pallas_call(
        functools.partial(_moe_kernel, alpha=alpha),
        grid_spec=pltpu.PrefetchScalarGridSpec(
            num_scalar_prefetch=3, grid=(tile.shape[0],),
            in_specs=[pl.BlockSpec((tm, W), lambda s, t, e, f: (t[s], 0)),
                      pl.BlockSpec((1, D, F), lambda s, t, e, f: (e[s], 0, 0)),
                      pl.BlockSpec((1, D, F), lambda s, t, e, f: (e[s], 0, 0)),
                      pl.BlockSpec((1, F, D), lambda s, t, e, f: (e[s], 0, 0)),
                      pl.BlockSpec((1, D), lambda s, t, e, f: (0, 0)),
                      pl.BlockSpec((1, D), lambda s, t, e, f: (0, 0))],
            out_specs=pl.BlockSpec((tm, D), lambda s, t, e, f: (t[s], 0)),
            scratch_shapes=[pltpu.VMEM((tm, D), BF16), pltpu.VMEM((tm, D), F32)]),
        out_shape=jax.ShapeDtypeStruct((T, D), F32),
        compiler_params=_params("arbitrary"), name="moe_experts_ln",
    )(tile, expert, flags, xs, w_gate, w_up, w_down, ln_g.reshape(1, D), ln_b.reshape(1, D))


def _moe(xa, cls, w_gate, w_up, w_down, ln_g, ln_b, alpha, tm=512):
    T = xa.shape[0]
    E = w_gate.shape[0]
    n_cls = N_GROUPS * len(_expert_pairs(E // N_GROUPS))
    rank, cnt = _class_ranks(cls, -(-n_cls // 8) * 8)
    counts = cnt[:n_cls, 0].astype(jnp.int32)
    starts = jnp.cumsum(counts) - counts
    dest = (starts[cls.reshape(T)] + rank.reshape(T)).astype(jnp.int32)
    xs = _dispatch(xa, dest)
    ys = _moe_experts(xs, _moe_steps(counts, E, T // tm, tm), w_gate, w_up, w_down, ln_g, ln_b, alpha, tm)
    return _undispatch(ys, dest)


def _rotary_tables(S):
    half = RET_HD // 2
    inv_freq = ROPE_BASE ** (-jnp.arange(half, dtype=F32) / half)
    ang = jnp.arange(S).astype(F32)[:, None] * inv_freq[None, :]
    cos, sin = jnp.cos(ang), jnp.sin(ang)
    return jnp.concatenate([cos, cos], axis=1), jnp.concatenate([-sin, sin], axis=1)


def kernel(x, mem, ln_in_g, ln_in_b, router_w, router_bias, w_in, ret_gn_g, rwkv_mu, rwkv_w_up, rwkv_w0,
           rwkv_a_up, rwkv_a0, rwkv_g_up, rwkv_k_k, rwkv_k_a, rwkv_r_k, rwkv_ln_g, rwkv_ln_b,
           w_ret_up, w_rwkv_up, w_out, ln1_g, ln1_b, xa_wq, xa_wkv, xa_wo, ln2_g, ln2_b,
           moe_w_gate, moe_w_up, moe_w_down, ln3_g, ln3_b):
    B, S, D = x.shape
    T = B * S
    depth = w_in.shape[0]
    alpha = (2 * depth) ** 0.25
    ret_w = ret_gn_g.shape[-1]
    rwkv_w = rwkv_w0.shape[-1]
    ret_cols = 4 * ret_w
    rwkv_cols = rwkv_mu.shape[-1]
    n_proj = -(-(ret_cols + rwkv_cols) // IN_PROJ_TN) * IN_PROJ_TN
    cos2, sin2 = _rotary_tables(S)
    mem2 = mem.reshape(B * mem.shape[1], D)

    xs = x.reshape(T, D)
    for l in range(depth):
        w_l = w_in[l]
        w_cat = jnp.concatenate([w_l[:, :ret_cols + rwkv_cols],
                                 jnp.zeros((D, n_proj - ret_cols - rwkv_cols), w_l.dtype)], axis=1).astype(BF16)
        if l == 0:
            xs, p = _ln_matmul(xs, ln_in_g, ln_in_b, w_cat, 1024, IN_PROJ_TN, BF16, "ln_in_proj")
        else:
            p = _matmul(xs, w_cat, 1024, IN_PROJ_TN, BF16, "in_proj")
        p3 = p.reshape(B, S, n_proj)
        y_ret = _retention(p3, cos2, sin2, ret_gn_g[l])
        y_rwkv = _rwkv(p3, ret_cols, rwkv_mu[l], rwkv_w_up[l], rwkv_w0[l], rwkv_a_up[l],
                       rwkv_a0[l], rwkv_g_up[l], rwkv_k_k[l], rwkv_k_a[l], rwkv_r_k[l].reshape(-1),
                       rwkv_ln_g[l], rwkv_ln_b[l])
        xs = _merge(y_ret.reshape(T, ret_w), y_rwkv.reshape(T, rwkv_w), xs, w_l[:, ret_cols + rwkv_cols:].astype(BF16),
                    w_ret_up[l].astype(BF16), w_rwkv_up[l].astype(BF16), w_out[l].astype(BF16),
                    ln1_g[l], ln1_b[l], alpha)
        kv = _matmul(mem2, xa_wkv[l].astype(BF16), 512, 512, BF16, "xattn_kv")
        xa, cls = _cross_attention(xs.reshape(B, S, D), kv.reshape(B, -1, 2 * D), xa_wq[l].astype(BF16),
                                   xa_wo[l].astype(BF16), ln2_g[l], ln2_b[l], router_w, router_bias, alpha)
        xs = _moe(xa.reshape(T, D + LANES), cls, moe_w_gate[l].astype(BF16), moe_w_up[l].astype(BF16),
                  moe_w_down[l].astype(BF16), ln3_g[l], ln3_b[l], alpha)
    return xs.reshape(B, S, D)
```

```python
import functools
import math

import jax
import jax.numpy as jnp
from jax import lax
from jax.experimental import pallas as pl
from jax.experimental.pallas import tpu as pltpu

F32 = jnp.float32
BF16 = jnp.bfloat16

LANES = 128
SUBLANES = 8
VMEM_LIMIT = 56 * 1024 * 1024

CHUNK = 64
RET_HEADS = 4
RET_HD = 128
ROPE_BASE = 10000.0
RWKV_HD = 64
HD_SHIFT = 6
RWKV_GN_EPS = 64e-5
XA_HEADS = 4
XA_ROW_PARTS = 4
N_GROUPS = 4
TOP_K = 2
LN_EPS = 1e-5
IN_PROJ_TN = 2048
IN_PROJ_ROW_PARTS = 2
HEADS_PER_GROUP = 4
GROUP_W = HEADS_PER_GROUP * RWKV_HD


def _params(*sem):
    return pltpu.CompilerParams(dimension_semantics=sem, vmem_limit_bytes=VMEM_LIMIT)


def _ln(v, g, b, eps=LN_EPS):
    mu = jnp.mean(v, axis=-1, keepdims=True)
    d = v - mu
    var = jnp.mean(d * d, axis=-1, keepdims=True)
    return d * lax.rsqrt(var + eps) * g + b


def _dot(a, b):
    return jnp.dot(a.astype(BF16), b.astype(BF16), preferred_element_type=F32)


def _dot_nt(a, b):
    return lax.dot_general(a.astype(BF16), b.astype(BF16), (((1,), (1,)), ((), ())),
                           preferred_element_type=F32)


def _dot_tn(a, b):
    return lax.dot_general(a.astype(BF16), b.astype(BF16), (((0,), (0,)), ((), ())),
                           preferred_element_type=F32)


def _mm_kernel(a_ref, w_ref, o_ref, abf_ref):
    @pl.when(pl.program_id(1) == 0)
    def _():
        abf_ref[...] = a_ref[...].astype(BF16)

    o_ref[...] = jnp.dot(abf_ref[...], w_ref[...], preferred_element_type=F32).astype(o_ref.dtype)


def _matmul(a, w, tm, tn, out_dtype, name):
    M, K = a.shape
    N = w.shape[1]
    return pl.pallas_call(
        _mm_kernel, grid=(M // tm, N // tn),
        in_specs=[pl.BlockSpec((tm, K), lambda i, j: (i, 0)),
                  pl.BlockSpec((K, tn), lambda i, j: (0, j))],
        out_specs=pl.BlockSpec((tm, tn), lambda i, j: (i, j)),
        out_shape=jax.ShapeDtypeStruct((M, N), out_dtype),
        scratch_shapes=[pltpu.VMEM((tm, K), BF16)],
        compiler_params=_params("parallel", "arbitrary"), name=name,
    )(a, w)


def _in_proj_kernel(*refs, with_ln):
    if with_ln:
        a_ref, g_ref, b_ref, w_ref, o_ref, wbf_ref = refs
    else:
        a_ref, w_ref, o_ref, wbf_ref = refs

    @pl.when(pl.program_id(1) == 0)
    def _():
        wbf_ref[...] = w_ref[0].astype(BF16)

    tm = a_ref.shape[0]
    part = tm // IN_PROJ_ROW_PARTS

    def prepared(rs):
        a = a_ref[rs, :]
        if with_ln:
            a = _ln(a, g_ref[...], b_ref[...])
        return a.astype(BF16)

    rows = [slice(i * part, (i + 1) * part) for i in range(IN_PROJ_ROW_PARTS)]
    cur = prepared(rows[0])
    for i, rs in enumerate(rows):
        nxt = prepared(rows[i + 1]) if i + 1 < len(rows) else None
        o_ref[rs, :] = jnp.dot(cur, wbf_ref[...], preferred_element_type=F32).astype(o_ref.dtype)
        cur = nxt


def _in_proj(a, w_in, layer, n_cols, ln=None, tm=1024):
    M, K = a.shape
    tn = IN_PROJ_TN
    assert n_cols % tn == 0 and n_cols <= w_in.shape[-1] and M % tm == 0
    with_ln = ln is not None
    vec = pl.BlockSpec((1, K), lambda j, i: (0, 0))
    in_specs = [pl.BlockSpec((tm, K), lambda j, i: (i, 0))]
    in_specs += [vec, vec] if with_ln else []
    in_specs += [pl.BlockSpec((1, K, tn), lambda j, i: (layer, 0, j), pipeline_mode=pl.Buffered(1))]
    args = (a,) + ((ln[0].reshape(1, K), ln[1].reshape(1, K)) if with_ln else ()) + (w_in,)
    return pl.pallas_call(
        functools.partial(_in_proj_kernel, with_ln=with_ln), grid=(n_cols // tn, M // tm),
        in_specs=in_specs, out_specs=pl.BlockSpec((tm, tn), lambda j, i: (i, j)),
        out_shape=jax.ShapeDtypeStruct((M, n_cols), BF16),
        scratch_shapes=[pltpu.VMEM((K, tn), BF16)],
        compiler_params=_params("arbitrary", "arbitrary"), name="ln_in_proj" if with_ln else "in_proj",
    )(*args)


def _ret_kernel(q_ref, k_ref, v_ref, g_ref, cos_ref, sin_ref, gn_ref, o_ref, state_ref, *, n_chunks):
    @pl.when(pl.program_id(1) == 0)
    def _():
        state_ref[...] = jnp.zeros_like(state_ref)

    C, hd = CHUNK, RET_HD
    ti = lax.broadcasted_iota(jnp.int32, (C, C), 0)
    si = lax.broadcasted_iota(jnp.int32, (C, C), 1)
    dist = jnp.abs(ti - si).astype(F32)
    cpos = lax.broadcasted_iota(jnp.int32, (C, 1), 0).astype(F32)
    log_gamma = [math.log(1.0 - 2.0 ** (-5.0 - h)) for h in range(RET_HEADS)]
    inner_mask = [jnp.exp(lg * dist) for lg in log_gamma]
    k_decay = [jnp.exp(lg * (C - 1.0 - cpos)) for lg in log_gamma]
    q_decay = [jnp.exp(lg * (cpos + 1.0)) for lg in log_gamma]
    pairs = [(c, h) for c in range(n_chunks) for h in range(RET_HEADS)]
    rows = lambda c: slice(c * C, (c + 1) * C)
    cols = lambda h: slice(h * hd, (h + 1) * hd)

    def rotated(ref, c, h):
        t = ref[0, rows(c), cols(h)].astype(F32)
        return t * cos_ref[rows(c), :] + pltpu.roll(t, hd // 2, axis=1) * sin_ref[rows(c), :]

    qr = [rotated(q_ref, c, h) for c, h in pairs]
    kr = [rotated(k_ref, c, h) * (hd ** -0.5) for c, h in pairs]
    v = [v_ref[0, rows(c), cols(h)] for c, h in pairs]
    scores = [_dot_nt(a, b) * inner_mask[h] for a, b, (c, h) in zip(qr, kr, pairs)]
    inner = [_dot(s, t) for s, t in zip(scores, v)]
    kv = [_dot_tn(b * k_decay[h], t) for b, t, (c, h) in zip(kr, v, pairs)]
    states = []
    for h in range(RET_HEADS):
        state = state_ref[h]
        for c in range(n_chunks):
            states.append(((c, h), state))
            state = math.exp(log_gamma[h] * C) * state + kv[pairs.index((c, h))]
        state_ref[h] = state
    entering = dict(states)
    cross = [_dot(a * q_decay[h], entering[(c, h)]) for a, (c, h) in zip(qr, pairs)]
    for y_in, y_cr, (c, h) in zip(inner, cross, pairs):
        y = y_in + y_cr
        mu = jnp.mean(y, axis=-1, keepdims=True)
        d = y - mu
        var = jnp.mean(d * d, axis=-1, keepdims=True)
        yn = d * lax.rsqrt(var + LN_EPS) * gn_ref[:, cols(h)]
        g = g_ref[0, rows(c), cols(h)].astype(F32)
        o_ref[0, rows(c), cols(h)] = (g * jax.nn.sigmoid(g) * yn).astype(o_ref.dtype)


def _retention(p_ret, cos2, sin2, gn_g, ts=256):
    B, S, _ = p_ret.shape
    W = gn_g.shape[-1]
    blk = lambda c: pl.BlockSpec((1, ts, W), lambda b, n, c=c: (b, n, c))
    return pl.pallas_call(
        functools.partial(_ret_kernel, n_chunks=ts // CHUNK), grid=(B, S // ts),
        in_specs=[blk(0), blk(1), blk(2), blk(3),
                  pl.BlockSpec((ts, RET_HD), lambda b, n: (n, 0)),
                  pl.BlockSpec((ts, RET_HD), lambda b, n: (n, 0)),
                  pl.BlockSpec((1, W), lambda b, n: (0, 0))],
        out_specs=pl.BlockSpec((1, ts, W), lambda b, n: (b, n, 0)),
        out_shape=jax.ShapeDtypeStruct((B, S, W), BF16),
        scratch_shapes=[pltpu.VMEM((RET_HEADS, RET_HD, RET_HD), F32)],
        compiler_params=_params("parallel", "arbitrary"), name="retention",
    )(p_ret, p_ret, p_ret, p_ret, cos2, sin2, gn_g.reshape(1, W))


def _head_sum(x):
    R, W = x.shape
    lo = lax.broadcasted_iota(jnp.int32, (R, LANES), 1) < RWKV_HD
    outs = []
    for j in range(W // LANES):
        xs = x[:, j * LANES:(j + 1) * LANES]
        s_lo = jnp.sum(jnp.where(lo, xs, 0.0), axis=-1, keepdims=True)
        s_hi = jnp.sum(jnp.where(lo, 0.0, xs), axis=-1, keepdims=True)
        outs.append(jnp.where(lo, s_lo, s_hi))
    return jnp.concatenate(outs, axis=1)


def _block_diag(x, head_of_lane):
    parts = [jnp.where(head_of_lane == h, x, 0.0) for h in range(HEADS_PER_GROUP)]
    return jnp.concatenate(parts, axis=0).astype(BF16)


def _packed_mm(xs, ys, head_of_lane):
    bds = [_block_diag(y, head_of_lane) for y in ys]
    return [jnp.dot(x.astype(BF16), bd, preferred_element_type=F32) for x, bd in zip(xs, bds)]


def _unit_lower_inverse(a_list, t_idx, s_idx, head_of_lane):
    mm = functools.partial(_packed_mm, head_of_lane=head_of_lane)
    add = lambda xs, ys: [x + y for x, y in zip(xs, ys)]
    lower = s_idx < t_idx
    same4 = (t_idx >> 2) == (s_idx >> 2)
    same16 = (t_idx >> 4) == (s_idx >> 4)
    m1 = lower & same4
    m2 = lower & same16 & jnp.logical_not(same4)
    m3 = lower & jnp.logical_not(same16)
    eye = jnp.where(t_idx == s_idx, 1.0, 0.0)
    a1 = [jnp.where(m1, a, 0.0) for a in a_list]
    a1sq = mm(a1, a1)
    a1cu = mm(a1, a1sq)
    inv = [eye + p + q + r for p, q, r in zip(a1, a1sq, a1cu)]
    for m in (m2, m3):
        n = mm(inv, [jnp.where(m, a, 0.0) for a in a_list])
        x = add(inv, mm(mm(n, n), inv))
        inv = add(x, mm(n, x))
    return inv


def _rwkv_kernel(*refs):
    state_ref = refs[-1]

    @pl.when(pl.program_id(1) == 0)
    def _():
        for ref in refs[-5:]:
            ref[...] = jnp.zeros_like(ref)

    seqs = [_rwkv_prepare(bi, *refs) for bi in range(state_ref.shape[0])]
    _rwkv_chains(seqs, state_ref)
    for bi, seq in enumerate(seqs):
        _rwkv_finish(bi, seq, *refs)


def _rwkv_prepare(bi, pr_ref, pk_ref, pv_ref, pl_ref, mur_ref, muk_ref, muv_ref, mul_ref,
                  wup_ref, aup_ref, gup_ref, w0_ref, a0_ref, kk_ref, ka_ref, rk_ref, lng_ref, lnb_ref,
                  o_ref, carry_r, carry_k, carry_v, carry_l, state_ref):
    C = CHUNK

    def shifted(p_ref, mu_ref, carry):
        p = p_ref[bi].astype(F32)
        row = lax.broadcasted_iota(jnp.int32, p.shape, 0)
        prev = jnp.where(row == 0, carry[bi, 0:1, :], pltpu.roll(p, 1, axis=0))
        carry[bi, 0:1, :] = p[C - 1:C, :]
        return p + mu_ref[...] * (prev - p)

    r = shifted(pr_ref, mur_ref, carry_r)
    k = shifted(pk_ref, muk_ref, carry_k)
    v = shifted(pv_ref, muv_ref, carry_v)
    lo = shifted(pl_ref, mul_ref, carry_l)
    dwa = lo[:, :LANES]
    dg = lo[:, LANES:]

    z = w0_ref[...] + _dot(jnp.tanh(dwa), wup_ref[...])
    logw = -math.exp(-0.5) * jax.nn.sigmoid(z)
    a = jax.nn.sigmoid(a0_ref[...] + _dot(dwa, aup_ref[...]))
    g = _dot(jax.nn.sigmoid(dg), gup_ref[...])

    kk = k * kk_ref[...]
    kk = kk / jnp.maximum(jnp.sqrt(_head_sum(kk * kk)), 1e-12)
    k = k * (1.0 + (a - 1.0) * ka_ref[...])

    ti = lax.broadcasted_iota(jnp.int32, (C, C), 0)
    si = lax.broadcasted_iota(jnp.int32, (C, C), 1)
    tril = jnp.where(si <= ti, 1.0, 0.0).astype(BF16)
    lw_hi = logw.astype(BF16)
    lw_lo = (logw - lw_hi.astype(F32)).astype(BF16)
    cum = (jnp.dot(tril, lw_hi, preferred_element_type=F32)
           + jnp.dot(tril, lw_lo, preferred_element_type=F32))
    cum_last = cum[C - 1:C, :]
    e_neg = jnp.exp(-cum)
    abar = -kk * jnp.exp(cum - logw)
    rbar = r * jnp.exp(cum)
    kka = kk * a
    btil = kka * e_neg
    ktil = k * e_neg
    gamma_end = jnp.exp(cum_last)
    e_end = gamma_end * e_neg
    bend = kka * e_end
    kend = k * e_end
    return dict(r=r, k=k, v=v, g=g, abar=abar, rbar=rbar, btil=btil, ktil=ktil, bend=bend, kend=kend,
                gamma_end=gamma_end)


def _rwkv_chains(seqs, state_ref):
    C = CHUNK
    n_groups = state_ref.shape[1]
    t_idx = lax.broadcasted_iota(jnp.int32, (C, GROUP_W), 0)
    lane = lax.broadcasted_iota(jnp.int32, (C, GROUP_W), 1)
    s_idx = lane & (RWKV_HD - 1)
    head_of_lane = lane >> HD_SHIFT
    strict = s_idx < t_idx
    incl = s_idx <= t_idx
    bd_r = lax.broadcasted_iota(jnp.int32, (GROUP_W, GROUP_W), 0) >> HD_SHIFT
    bd_c = lax.broadcasted_iota(jnp.int32, (GROUP_W, GROUP_W), 1) >> HD_SHIFT
    on_diag = bd_r == bd_c

    chains = [(bi, gi) for bi in range(len(seqs)) for gi in range(n_groups)]
    part = lambda name: [seqs[bi][name][:, gi * GROUP_W:(gi + 1) * GROUP_W] for bi, gi in chains]
    mm = functools.partial(_packed_mm, head_of_lane=head_of_lane)
    v_g = part("v")
    lhs = [jnp.concatenate([p, q], axis=0) for p, q in zip(part("abar"), part("rbar"))]
    rhs = [jnp.concatenate([_block_diag(p, head_of_lane), _block_diag(q, head_of_lane)], axis=0)
           for p, q in zip(part("btil"), part("ktil"))]
    gram = [_dot_nt(p, q) for p, q in zip(lhs, rhs)]
    a_ab = [jnp.where(strict, gm[:C, :GROUP_W], 0.0) for gm in gram]
    a_ak = [jnp.where(strict, gm[:C, GROUP_W:], 0.0) for gm in gram]
    m_rb = [jnp.where(incl, gm[C:, :GROUP_W], 0.0) for gm in gram]
    m_rk = [jnp.where(incl, gm[C:, GROUP_W:], 0.0) for gm in gram]
    inv = _unit_lower_inverse(a_ab, t_idx, s_idx, head_of_lane)
    st = [state_ref[bi, gi] for bi, gi in chains]
    from_state = [_dot_nt(p, q) for p, q in zip(lhs, st)]
    local = mm([jnp.concatenate([p, q], axis=0) for p, q in zip(a_ak, m_rk)], v_g)
    u = mm(inv, [fs[:C] + lc[:C] for fs, lc in zip(from_state, local)])
    via_u = mm(m_rb, u)
    upd = [_dot_tn(jnp.concatenate([p, q], axis=0), jnp.concatenate([b, k], axis=0))
           for p, q, b, k in zip(u, v_g, part("bend"), part("kend"))]
    for (bi, gi), s_old, up, ge in zip(chains, st, upd, part("gamma_end")):
        state_ref[bi, gi] = s_old * ge + jnp.where(on_diag, up, 0.0)
    ys = [fs[C:] + p + lc[C:] for fs, p, lc in zip(from_state, via_u, local)]
    for bi, seq in enumerate(seqs):
        seq["y"] = jnp.concatenate(ys[bi * n_groups:(bi + 1) * n_groups], axis=1)


def _rwkv_finish(bi, seq, pr_ref, pk_ref, pv_ref, pl_ref, mur_ref, muk_ref, muv_ref, mul_ref,
                 wup_ref, aup_ref, gup_ref, w0_ref, a0_ref, kk_ref, ka_ref, rk_ref, lng_ref, lnb_ref,
                 o_ref, carry_r, carry_k, carry_v, carry_l, state_ref):
    y, r, k, v, g = seq["y"], seq["r"], seq["k"], seq["v"], seq["g"]
    inv_hd = 1.0 / RWKV_HD
    mu = _head_sum(y) * inv_hd
    d = y - mu
    var = _head_sum(d * d) * inv_hd
    yn = d * lax.rsqrt(var + RWKV_GN_EPS) * lng_ref[...] + lnb_ref[...]
    bonus = _head_sum(r * k * rk_ref[...]) * v
    o_ref[bi] = ((yn + bonus) * g).astype(o_ref.dtype)


def _rwkv(p_rwkv, col0, mu, w_up, w0, a_up, a0, g_up, k_k, k_a, r_k, ln_g, ln_b, nb=4):
    B, S, _ = p_rwkv.shape
    W = w0.shape[-1]
    C = CHUNK
    wup_pad = jnp.concatenate([w_up, jnp.zeros_like(a_up)], axis=0).astype(BF16)
    aup_pad = jnp.concatenate([jnp.zeros_like(w_up), a_up], axis=0).astype(BF16)
    row = lambda t: t.reshape(1, -1)
    vec = lambda n: pl.BlockSpec((1, n), lambda b, t: (0, 0))
    mat = lambda a: pl.BlockSpec(a.shape, lambda b, t: (0, 0))
    lw = mu.shape[-1] - 3 * W
    assert col0 % W == 0 and (col0 + 3 * W) % lw == 0 and B % nb == 0 and S % C == 0
    cblk = lambda c: pl.BlockSpec((nb, C, W), lambda b, t, c=c: (b, t, col0 // W + c))
    n_groups = W // GROUP_W
    gup = g_up.astype(BF16)
    return pl.pallas_call(
        _rwkv_kernel, grid=(B // nb, S // C),
        in_specs=[cblk(0), cblk(1), cblk(2),
                  pl.BlockSpec((nb, C, lw), lambda b, t: (b, t, (col0 + 3 * W) // lw)),
                  vec(W), vec(W), vec(W), vec(lw),
                  mat(wup_pad), mat(aup_pad), mat(gup),
                  vec(W), vec(W), vec(W), vec(W), vec(W), vec(W), vec(W)],
        out_specs=pl.BlockSpec((nb, C, W), lambda b, t: (b, t, 0)),
        out_shape=jax.ShapeDtypeStruct((B, S, W), BF16),
        scratch_shapes=[pltpu.VMEM((nb, 8, W), F32), pltpu.VMEM((nb, 8, W), F32), pltpu.VMEM((nb, 8, W), F32),
                        pltpu.VMEM((nb, 8, lw), F32),
                        pltpu.VMEM((nb, n_groups, GROUP_W, GROUP_W), F32)],
        compiler_params=_params("parallel", "arbitrary"), name="rwkv7",
    )(p_rwkv, p_rwkv, p_rwkv, p_rwkv,
      row(mu[:W]), row(mu[W:2 * W]), row(mu[2 * W:3 * W]), row(mu[3 * W:]),
      wup_pad, aup_pad, gup,
      row(w0), row(a0), row(k_k), row(k_a), row(r_k), row(ln_g), row(ln_b))


def _merge_kernel(*refs, alpha, with_ln):
    if with_ln:
        yr_ref, yw_ref, x_ref, pg_ref, pb_ref, wg_ref, wr_ref, ww_ref, wo_ref, g_ref, b_ref, o_ref = refs
    else:
        yr_ref, yw_ref, x_ref, wg_ref, wr_ref, ww_ref, wo_ref, g_ref, b_ref, o_ref = refs
    tm, D = x_ref.shape
    half = tm // 2

    def block_input(rs):
        x = x_ref[rs, :]
        return _ln(x, pg_ref[...], pb_ref[...]) if with_ln else x

    def branches(rs):
        return (jnp.dot(yr_ref[rs, :], wr_ref[...], preferred_element_type=F32),
                jnp.dot(yw_ref[rs, :], ww_ref[...], preferred_element_type=F32),
                jnp.dot(block_input(rs).astype(BF16), wg_ref[...], preferred_element_type=F32))

    def mix(rs, up):
        gate = jax.nn.sigmoid(up[2])
        return (gate[:, :D] * up[0] + gate[:, D:] * up[1]).astype(BF16)

    def out_proj(m):
        return jnp.dot(m, wo_ref[...], preferred_element_type=F32)

    def finish(rs, h):
        o_ref[rs, :] = _ln(alpha * block_input(rs) + h, g_ref[...], b_ref[...])

    rows_a, rows_b = slice(0, half), slice(half, tm)
    up_a = branches(rows_a)
    up_b = branches(rows_b)
    h_a = out_proj(mix(rows_a, up_a))
    h_b = out_proj(mix(rows_b, up_b))
    finish(rows_a, h_a)
    finish(rows_b, h_b)


def _merge(y_ret, y_rwkv, x, pre_ln, w_gate, w_ret_up, w_rwkv_up, w_out, ln_g, ln_b, alpha, tm=512):
    T, D = x.shape
    W = y_ret.shape[1]
    full = lambda a: pl.BlockSpec(a.shape, lambda i: (0, 0))
    rowblk = lambda n: pl.BlockSpec((tm, n), lambda i: (i, 0))
    vec = pl.BlockSpec((1, D), lambda i: (0, 0))
    with_ln = pre_ln is not None
    pre = (pre_ln[0].reshape(1, D), pre_ln[1].reshape(1, D)) if with_ln else ()
    return pl.pallas_call(
        functools.partial(_merge_kernel, alpha=alpha, with_ln=with_ln), grid=(T // tm,),
        in_specs=[rowblk(W), rowblk(W), rowblk(D)] + [vec] * len(pre)
        + [full(w_gate), full(w_ret_up), full(w_rwkv_up), full(w_out), vec, vec],
        out_specs=rowblk(D),
        out_shape=jax.ShapeDtypeStruct((T, D), F32),
        compiler_params=_params("parallel"), name="merge_out_ln",
    )(y_ret, y_rwkv, x, *pre, w_gate, w_ret_up, w_rwkv_up, w_out, ln_g.reshape(1, D), ln_b.reshape(1, D))


def _xattn_kernel(x_ref, kv_ref, wq_ref, wo_ref, g_ref, b_ref, rwt_ref, rb_ref, o_ref, cls_ref, *, alpha):
    tm, D = x_ref.shape[1:]
    hd = D // XA_HEADS
    half = tm // XA_ROW_PARTS

    def project(rs):
        return jnp.dot(x_ref[0, rs, :].astype(BF16), wq_ref[...], preferred_element_type=F32)

    def scores(q):
        return [_dot_nt(q[:, h * hd:(h + 1) * hd], kv_ref[0, :, h * hd:(h + 1) * hd]) * (hd ** -0.5)
                for h in range(XA_HEADS)]

    def attend(s_heads):
        outs = []
        for h, s in enumerate(s_heads):
            e = jnp.exp(s - jnp.max(s, axis=-1, keepdims=True))
            probs = e / jnp.sum(e, axis=-1, keepdims=True)
            outs.append(jnp.dot(probs.astype(BF16), kv_ref[0, :, D + h * hd:D + (h + 1) * hd],
                                preferred_element_type=F32))
        return jnp.concatenate(outs, axis=1)

    def out_proj(o):
        return jnp.dot(o.astype(BF16), wo_ref[...], preferred_element_type=F32)

    def finish(rs, h_out):
        x2 = _ln(alpha * x_ref[0, rs, :] + h_out, g_ref[...], b_ref[...])
        comb, cls = _route(x2, rwt_ref[...], rb_ref[...])
        E = comb.shape[0]
        comb_pad = jnp.concatenate([comb, jnp.zeros((LANES - E, half), F32)], axis=0)
        o_ref[0, rs, :D] = x2
        o_ref[0, rs, D:] = comb_pad.T
        cls_ref[0, :, rs] = cls

    rows = [slice(i * half, (i + 1) * half) for i in range(tm // half)]
    s_cur, h_prev = scores(project(rows[0])), None
    for i, rs in enumerate(rows):
        q_next = project(rows[i + 1]) if i + 1 < len(rows) else None
        o = attend(s_cur)
        if q_next is not None:
            s_cur = scores(q_next)
        if h_prev is not None:
            finish(rows[i - 1], h_prev)
        h_prev = out_proj(o)
    finish(rows[-1], h_prev)


def _cross_attention(x, kv, wq, wo, ln_g, ln_b, router_w, router_bias, alpha, tm=1024):
    B, S, D = x.shape
    M = kv.shape[1]
    E = router_w.shape[1]
    full = lambda a: pl.BlockSpec(a.shape, lambda b, i: (0, 0))
    n_t = S // tm
    return pl.pallas_call(
        functools.partial(_xattn_kernel, alpha=alpha), grid=(B, n_t),
        in_specs=[pl.BlockSpec((1, tm, D), lambda b, i: (b, i, 0)),
                  pl.BlockSpec((1, M, 2 * D), lambda b, i: (b, 0, 0)),
                  full(wq), full(wo),
                  pl.BlockSpec((1, D), lambda b, i: (0, 0)), pl.BlockSpec((1, D), lambda b, i: (0, 0)),
                  pl.BlockSpec((D, LANES), lambda b, i: (0, 0)), pl.BlockSpec((E, 1), lambda b, i: (0, 0))],
        out_specs=[pl.BlockSpec((1, tm, D + LANES), lambda b, i: (b, i, 0)),
                   pl.BlockSpec((1, 1, tm), lambda b, i: (b * n_t + i, 0, 0))],
        out_shape=[jax.ShapeDtypeStruct((B, S, D + LANES), F32),
                   jax.ShapeDtypeStruct((B * n_t, 1, tm), jnp.int32)],
        compiler_params=_params("parallel", "parallel"), name="xattn_ln_route",
    )(x, kv, wq, wo, ln_g.reshape(1, D), ln_b.reshape(1, D),
      jnp.pad(router_w, ((0, 0), (0, LANES - E))), router_bias.reshape(E, 1))


def _expert_pairs(per_group):
    todo = [(i, j) for i in range(per_group) for j in range(i + 1, per_group)]
    order = [todo.pop(0)]
    while todo:
        nxt = next((p for p in todo if set(p) & set(order[-1])), todo[0])
        todo.remove(nxt)
        order.append(nxt)
    return order


def _split_bf16(v):
    hi = v.astype(BF16)
    return hi, (v - hi.astype(F32)).astype(BF16)


def _route(x, w, bias):
    E = bias.shape[0]
    per_group = E // N_GROUPS
    x_hi, x_lo = _split_bf16(x)
    w_hi, w_lo = _split_bf16(w)
    logits = (jnp.dot(x_hi, w_hi, preferred_element_type=F32) + jnp.dot(x_hi, w_lo, preferred_element_type=F32)
              + jnp.dot(x_lo, w_hi, preferred_element_type=F32)).T[:E]
    aff = jax.nn.sigmoid(logits)
    choice = aff + bias
    rows = [choice[e:e + 1, :] for e in range(E)]
    scores = []
    for gidx in range(N_GROUPS):
        members = rows[gidx * per_group:(gidx + 1) * per_group]
        best = None
        for i in range(per_group):
            for j in range(i + 1, per_group):
                pair = members[i] + members[j]
                best = pair if best is None else jnp.maximum(best, pair)
        scores.append(best)
    top = scores[0]
    for s in scores[1:]:
        top = jnp.maximum(top, s)
    taken = jnp.zeros_like(top, dtype=jnp.bool_)
    in_best = []
    for s in scores:
        hit = jnp.logical_and(s == top, jnp.logical_not(taken))
        in_best.append(hit)
        taken = jnp.logical_or(taken, hit)
    sel_w = []
    cls = jnp.zeros(top.shape, jnp.int32)
    pairs = _expert_pairs(per_group)
    for gidx in range(N_GROUPS):
        members = rows[gidx * per_group:(gidx + 1) * per_group]
        chosen = []
        for i in range(per_group):
            rank = jnp.zeros_like(top)
            for j in range(per_group):
                if j == i:
                    continue
                ahead = (members[j] > members[i]) if j > i else (members[j] >= members[i])
                rank = rank + jnp.where(ahead, 1.0, 0.0)
            chosen.append(jnp.logical_and(in_best[gidx], rank < TOP_K))
            e = gidx * per_group + i
            sel_w.append(jnp.where(chosen[i], aff[e:e + 1, :], 0.0))
        for p, (i, j) in enumerate(pairs):
            cls = jnp.where(jnp.logical_and(chosen[i], chosen[j]), gidx * len(pairs) + p, cls)
    total = sel_w[0]
    for w in sel_w[1:]:
        total = total + w
    return jnp.concatenate(sel_w, axis=0) / total, cls


def _rank_kernel(cls_ref, rank_ref, cnt_ref, base_ref):
    @pl.when(pl.program_id(0) == 0)
    def _():
        base_ref[...] = jnp.zeros_like(base_ref)

    cls = cls_ref[0]
    tm = cls.shape[-1]
    n_cls = base_ref.shape[0]
    onehot = lax.broadcasted_iota(jnp.int32, (n_cls, tm), 0) == cls
    earlier = (lax.broadcasted_iota(jnp.int32, (tm, tm), 0)
               < lax.broadcasted_iota(jnp.int32, (tm, tm), 1))
    ones = jnp.where(onehot, 1.0, 0.0)
    before = jnp.dot(ones.astype(BF16), jnp.where(earlier, 1.0, 0.0).astype(BF16),
                     preferred_element_type=F32)
    base = base_ref[...]
    rank = jnp.sum(jnp.where(onehot, base + before, 0.0), axis=0, keepdims=True)
    rank_ref[0] = rank.astype(jnp.int32)
    total = base + jnp.sum(ones, axis=1, keepdims=True)
    base_ref[...] = total
    cnt_ref[...] = jnp.broadcast_to(total, cnt_ref.shape)


def _class_ranks(cls, n_cls_pad):
    n_t, _, tm = cls.shape
    return pl.pallas_call(
        _rank_kernel, grid=(n_t,),
        in_specs=[pl.BlockSpec((1, 1, tm), lambda i: (i, 0, 0))],
        out_specs=[pl.BlockSpec((1, 1, tm), lambda i: (i, 0, 0)),
                   pl.BlockSpec((n_cls_pad, LANES), lambda i: (0, 0))],
        out_shape=[jax.ShapeDtypeStruct((n_t, 1, tm), jnp.int32),
                   jax.ShapeDtypeStruct((n_cls_pad, LANES), F32)],
        scratch_shapes=[pltpu.VMEM((n_cls_pad, 1), F32)],
        compiler_params=_params("arbitrary"), name="moe_class_rank",
    )(cls)


def _row_copy(src, src_row, dst, dst_row, sem):
    return pltpu.make_async_copy(src.at[pl.ds(src_row, 1)], dst.at[pl.ds(dst_row, 1)], sem)


def _for_each_row(n_rows, fn):
    def group(g, carry):
        r0 = pl.multiple_of(g * SUBLANES, SUBLANES)
        for j in range(SUBLANES):
            fn(r0 + j)
        return carry

    lax.fori_loop(0, n_rows // SUBLANES, group, 0)


def _dispatch_kernel(dest_ref, x_ref, o_hbm, sem):
    tm = x_ref.shape[0]
    base = pl.program_id(0) * tm
    copy = lambda r: _row_copy(x_ref, r, o_hbm, dest_ref[base + r], sem)
    _for_each_row(tm, lambda r: copy(r).start())
    _for_each_row(tm, lambda r: copy(r).wait())


def _dispatch(xa, dest, tm=512):
    T, W = xa.shape
    return pl.pallas_call(
        _dispatch_kernel,
        grid_spec=pltpu.PrefetchScalarGridSpec(
            num_scalar_prefetch=1, grid=(T // tm,),
            in_specs=[pl.BlockSpec((tm, W), lambda i, dest: (i, 0))],
            out_specs=pl.BlockSpec(memory_space=pl.ANY),
            scratch_shapes=[pltpu.SemaphoreType.DMA(())]),
        out_shape=jax.ShapeDtypeStruct((T, W), xa.dtype),
        compiler_params=_params("arbitrary"), name="moe_dispatch",
    )(dest, xa)


def _undispatch_kernel(dest_ref, y_hbm, o_ref, sem):
    tm = o_ref.shape[0]
    base = pl.program_id(0) * tm
    copy = lambda r: _row_copy(y_hbm, dest_ref[base + r], o_ref, r, sem)
    _for_each_row(tm, lambda r: copy(r).start())
    _for_each_row(tm, lambda r: copy(r).wait())


def _undispatch(ys, dest, tm=512):
    T, D = ys.shape
    return pl.pallas_call(
        _undispatch_kernel,
        grid_spec=pltpu.PrefetchScalarGridSpec(
            num_scalar_prefetch=1, grid=(T // tm,),
            in_specs=[pl.BlockSpec(memory_space=pl.ANY)],
            out_specs=pl.BlockSpec((tm, D), lambda i, dest: (i, 0)),
            scratch_shapes=[pltpu.SemaphoreType.DMA(())]),
        out_shape=jax.ShapeDtypeStruct((T, D), ys.dtype),
        compiler_params=_params("arbitrary"), name="moe_undispatch",
    )(dest, ys)


STEP_FIRST, STEP_LAST, STEP_VALID, STEP_NEW_EXPERT = 1, 2, 4, 8


def _moe_kernel(tile_ref, exp_ref, flag_ref, xs_ref, wg_ref, wu_ref, wd_ref, g_ref, b_ref, o_ref,
                xbf_ref, acc_ref, wgb_ref, wub_ref, wdb_ref, *, alpha):
    step = pl.program_id(0)
    flags = flag_ref[step]
    e = exp_ref[step]
    D = o_ref.shape[-1]

    @pl.when((flags & STEP_NEW_EXPERT) != 0)
    def _():
        wgb_ref[...] = wg_ref[0, 0].astype(BF16)
        wub_ref[...] = wu_ref[0, 0].astype(BF16)
        wdb_ref[...] = wd_ref[0, 0].astype(BF16)

    @pl.when((flags & STEP_FIRST) != 0)
    def _():
        xbf_ref[...] = xs_ref[:, :D].astype(BF16)
        acc_ref[...] = jnp.zeros_like(acc_ref)

    @pl.when((flags & STEP_VALID) != 0)
    def _():
        half = xbf_ref.shape[0] // 2

        def up(rs):
            xb = xbf_ref[rs, :]
            return (jnp.dot(xb, wgb_ref[...], preferred_element_type=F32),
                    jnp.dot(xb, wub_ref[...], preferred_element_type=F32))

        def act(hg, hu):
            return (hg * jax.nn.sigmoid(hg) * hu).astype(BF16)

        def down(h):
            return jnp.dot(h, wdb_ref[...], preferred_element_type=F32)

        def accumulate(rs, y):
            comb = xs_ref[rs, D:]
            lane = lax.broadcasted_iota(jnp.int32, comb.shape, 1)
            w_e = jnp.sum(jnp.where(lane == e, comb, 0.0), axis=-1, keepdims=True)
            acc_ref[rs, :] += w_e * y

        rows_a, rows_b = slice(0, half), slice(half, 2 * half)
        up_a = up(rows_a)
        up_b = up(rows_b)
        y_a = down(act(*up_a))
        y_b = down(act(*up_b))
        accumulate(rows_a, y_a)
        accumulate(rows_b, y_b)

    @pl.when((flags & STEP_LAST) != 0)
    def _():
        o_ref[...] = _ln(alpha * xs_ref[:, :D] + acc_ref[...], g_ref[...], b_ref[...])


def _moe_steps(counts, n_experts, n_tiles, tm):
    n_cls = counts.shape[0]
    per_group = n_experts // N_GROUPS
    pairs = _expert_pairs(per_group)
    uses = [[0.0] * n_experts for _ in range(n_cls)]
    for c in range(n_cls):
        gidx, (i, j) = c // len(pairs), pairs[c % len(pairs)]
        uses[c][gidx * per_group + i] = 1.0
        uses[c][gidx * per_group + j] = 1.0
    uses = jnp.array(uses, F32)
    ends = jnp.cumsum(counts)
    starts = ends - counts
    lo = (jnp.arange(n_tiles, dtype=jnp.int32) * tm)[:, None]
    present = jnp.logical_and(starts[None, :] < lo + tm, ends[None, :] > lo)
    need = jnp.dot(present.astype(F32), uses) > 0
    odd = (jnp.arange(n_tiles, dtype=jnp.int32) % 2 == 1)[:, None]
    need = jnp.where(odd, need[:, ::-1], need).reshape(-1)
    n_steps = min(n_tiles * n_experts, TOP_K * (n_tiles + n_cls - 1))
    n_valid = jnp.sum(need.astype(jnp.int32))
    idx = jnp.nonzero(need, size=n_steps, fill_value=0)[0].astype(jnp.int32)
    k = jnp.arange(n_steps, dtype=jnp.int32)
    valid = k < n_valid
    idx = jnp.where(valid, idx, idx[n_valid - 1])
    tile, expert = idx // n_experts, idx % n_experts
    expert = jnp.where(tile % 2 == 1, n_experts - 1 - expert, expert)
    prev_tile = jnp.concatenate([jnp.full((1,), -1, jnp.int32), tile[:-1]])
    next_tile = jnp.concatenate([tile[1:], jnp.full((1,), -1, jnp.int32)])
    first = jnp.logical_and(valid, tile != prev_tile)
    last = jnp.logical_and(valid, jnp.logical_or(tile != next_tile, k == n_valid - 1))
    prev_expert = jnp.concatenate([jnp.full((1,), -1, jnp.int32), expert[:-1]])
    new_expert = jnp.logical_and(valid, expert != prev_expert)
    flags = (first * STEP_FIRST + last * STEP_LAST + valid * STEP_VALID
             + new_expert * STEP_NEW_EXPERT).astype(jnp.int32)
    return tile, expert, flags


def _moe_experts(xs, steps, layer, w_gate, w_up, w_down, ln_g, ln_b, alpha, tm):
    T, W = xs.shape
    D = W - LANES
    F = w_gate.shape[-1]
    tile, expert, flags = steps
    return pl.pallas_call(
        functools.partial(_moe_kernel, alpha=alpha),
        grid_spec=pltpu.PrefetchScalarGridSpec(
            num_scalar_prefetch=3, grid=(tile.shape[0],),
            in_specs=[pl.BlockSpec((tm, W), lambda s, t, e, f: (t[s], 0)),
                      pl.BlockSpec((1, 1, D, F), lambda s, t, e, f: (layer, e[s], 0, 0)),
                      pl.BlockSpec((1, 1, D, F), lambda s, t, e, f: (layer, e[s], 0, 0)),
                      pl.BlockSpec((1, 1, F, D), lambda s, t, e, f: (layer, e[s], 0, 0)),
                      pl.BlockSpec((1, D), lambda s, t, e, f: (0, 0)),
                      pl.BlockSpec((1, D), lambda s, t, e, f: (0, 0))],
            out_specs=pl.BlockSpec((tm, D), lambda s, t, e, f: (t[s], 0)),
            scratch_shapes=[pltpu.VMEM((tm, D), BF16), pltpu.VMEM((tm, D), F32),
                            pltpu.VMEM((D, F), BF16), pltpu.VMEM((D, F), BF16), pltpu.VMEM((F, D), BF16)]),
        out_shape=jax.ShapeDtypeStruct((T, D), F32),
        compiler_params=_params("arbitrary"), name="moe_experts_ln",
    )(tile, expert, flags, xs, w_gate, w_up, w_down, ln_g.reshape(1, D), ln_b.reshape(1, D))


def _moe(xa, cls, layer, w_gate, w_up, w_down, ln_g, ln_b, alpha, tm=512):
    T = xa.shape[0]
    E = w_gate.shape[1]
    n_cls = N_GROUPS * len(_expert_pairs(E // N_GROUPS))
    rank, cnt = _class_ranks(cls, -(-n_cls // 8) * 8)
    counts = cnt[:n_cls, 0].astype(jnp.int32)
    starts = jnp.cumsum(counts) - counts
    dest = (starts[cls.reshape(T)] + rank.reshape(T)).astype(jnp.int32)
    xs = _dispatch(xa, dest)
    ys = _moe_experts(xs, _moe_steps(counts, E, T // tm, tm), layer, w_gate, w_up, w_down, ln_g, ln_b, alpha, tm)
    return _undispatch(ys, dest)


def _rotary_tables(S):
    half = RET_HD // 2
    inv_freq = ROPE_BASE ** (-jnp.arange(half, dtype=F32) / half)
    ang = jnp.arange(S).astype(F32)[:, None] * inv_freq[None, :]
    cos, sin = jnp.cos(ang), jnp.sin(ang)
    return jnp.concatenate([cos, cos], axis=1), jnp.concatenate([-sin, sin], axis=1)


def kernel(x, mem, ln_in_g, ln_in_b, router_w, router_bias, w_in, ret_gn_g, rwkv_mu, rwkv_w_up, rwkv_w0,
           rwkv_a_up, rwkv_a0, rwkv_g_up, rwkv_k_k, rwkv_k_a, rwkv_r_k, rwkv_ln_g, rwkv_ln_b,
           w_ret_up, w_rwkv_up, w_out, ln1_g, ln1_b, xa_wq, xa_wkv, xa_wo, ln2_g, ln2_b,
           moe_w_gate, moe_w_up, moe_w_down, ln3_g, ln3_b):
    B, S, D = x.shape
    T = B * S
    depth = w_in.shape[0]
    alpha = (2 * depth) ** 0.25
    ret_w = ret_gn_g.shape[-1]
    rwkv_w = rwkv_w0.shape[-1]
    ret_cols = 4 * ret_w
    rwkv_cols = rwkv_mu.shape[-1]
    n_proj = -(-(ret_cols + rwkv_cols) // IN_PROJ_TN) * IN_PROJ_TN
    cos2, sin2 = _rotary_tables(S)
    mem2 = mem.reshape(B * mem.shape[1], D)

    xs = x.reshape(T, D)
    for l in range(depth):
        w_l = w_in[l]
        entry_ln = (ln_in_g, ln_in_b) if l == 0 else None
        p3 = _in_proj(xs, w_in, l, n_proj, ln=entry_ln).reshape(B, S, n_proj)
        y_ret = _retention(p3, cos2, sin2, ret_gn_g[l])
        y_rwkv = _rwkv(p3, ret_cols, rwkv_mu[l], rwkv_w_up[l], rwkv_w0[l], rwkv_a_up[l],
                       rwkv_a0[l], rwkv_g_up[l], rwkv_k_k[l], rwkv_k_a[l], rwkv_r_k[l].reshape(-1),
                       rwkv_ln_g[l], rwkv_ln_b[l])
        xs = _merge(y_ret.reshape(T, ret_w), y_rwkv.reshape(T, rwkv_w), xs, entry_ln,
                    w_l[:, ret_cols + rwkv_cols:].astype(BF16),
                    w_ret_up[l].astype(BF16), w_rwkv_up[l].astype(BF16), w_out[l].astype(BF16),
                    ln1_g[l], ln1_b[l], alpha)
        kv = _matmul(mem2, xa_wkv[l].astype(BF16), 512, 512, BF16, "xattn_kv")
        xa, cls = _cross_attention(xs.reshape(B, S, D), kv.reshape(B, -1, 2 * D), xa_wq[l].astype(BF16),
                                   xa_wo[l].astype(BF16), ln2_g[l], ln2_b[l], router_w, router_bias, alpha)
        xs = _moe(xa.reshape(T, D + LANES), cls, l, moe_w_gate, moe_w_up, moe_w_down, ln3_g[l], ln3_b[l], alpha)
    return xs.reshape(B, S, D)
```

```python
import functools
import math

import jax
import jax.numpy as jnp
from jax import lax
from jax.experimental import pallas as pl
from jax.experimental.pallas import tpu as pltpu

F32 = jnp.float32
BF16 = jnp.bfloat16

LANES = 128
SUBLANES = 8
VMEM_LIMIT = 56 * 1024 * 1024

CHUNK = 64
RET_HEADS = 4
RET_HD = 128
ROPE_BASE = 10000.0
RWKV_HD = 64
HD_SHIFT = 6
RWKV_GN_EPS = 64e-5
XA_HEADS = 4
XA_ROW_PARTS = 4
N_GROUPS = 4
TOP_K = 2
LN_EPS = 1e-5
IN_PROJ_TN = 2048
IN_PROJ_ROW_PARTS = 2
HEADS_PER_GROUP = 4
GROUP_W = HEADS_PER_GROUP * RWKV_HD


def _params(*sem):
    return pltpu.CompilerParams(dimension_semantics=sem, vmem_limit_bytes=VMEM_LIMIT)


def _ln(v, g, b, eps=LN_EPS):
    mu = jnp.mean(v, axis=-1, keepdims=True)
    d = v - mu
    var = jnp.mean(d * d, axis=-1, keepdims=True)
    return d * lax.rsqrt(var + eps) * g + b


def _dot(a, b):
    return jnp.dot(a.astype(BF16), b.astype(BF16), preferred_element_type=F32)


def _dot_nt(a, b):
    return lax.dot_general(a.astype(BF16), b.astype(BF16), (((1,), (1,)), ((), ())),
                           preferred_element_type=F32)


def _dot_tn(a, b):
    return lax.dot_general(a.astype(BF16), b.astype(BF16), (((0,), (0,)), ((), ())),
                           preferred_element_type=F32)


def _mm_kernel(a_ref, w_ref, o_ref, abf_ref):
    @pl.when(pl.program_id(1) == 0)
    def _():
        abf_ref[...] = a_ref[...].astype(BF16)

    o_ref[...] = jnp.dot(abf_ref[...], w_ref[...], preferred_element_type=F32).astype(o_ref.dtype)


def _matmul(a, w, tm, tn, out_dtype, name):
    M, K = a.shape
    N = w.shape[1]
    return pl.pallas_call(
        _mm_kernel, grid=(M // tm, N // tn),
        in_specs=[pl.BlockSpec((tm, K), lambda i, j: (i, 0)),
                  pl.BlockSpec((K, tn), lambda i, j: (0, j))],
        out_specs=pl.BlockSpec((tm, tn), lambda i, j: (i, j)),
        out_shape=jax.ShapeDtypeStruct((M, N), out_dtype),
        scratch_shapes=[pltpu.VMEM((tm, K), BF16)],
        compiler_params=_params("parallel", "arbitrary"), name=name,
    )(a, w)


def _in_proj_kernel(*refs, with_ln):
    if with_ln:
        a_ref, g_ref, b_ref, w_ref, o_ref, wbf_ref = refs
    else:
        a_ref, w_ref, o_ref, wbf_ref = refs

    @pl.when(pl.program_id(1) == 0)
    def _():
        wbf_ref[...] = w_ref[0].astype(BF16)

    tm = a_ref.shape[0]
    part = tm // IN_PROJ_ROW_PARTS

    def prepared(rs):
        a = a_ref[rs, :]
        if with_ln:
            a = _ln(a, g_ref[...], b_ref[...])
        return a.astype(BF16)

    rows = [slice(i * part, (i + 1) * part) for i in range(IN_PROJ_ROW_PARTS)]
    cur = prepared(rows[0])
    for i, rs in enumerate(rows):
        nxt = prepared(rows[i + 1]) if i + 1 < len(rows) else None
        o_ref[rs, :] = jnp.dot(cur, wbf_ref[...], preferred_element_type=F32).astype(o_ref.dtype)
        cur = nxt


def _in_proj(a, w_in, layer, n_cols, ln=None, tm=1024):
    M, K = a.shape
    tn = IN_PROJ_TN
    assert n_cols % tn == 0 and n_cols <= w_in.shape[-1] and M % tm == 0
    with_ln = ln is not None
    vec = pl.BlockSpec((1, K), lambda j, i: (0, 0))
    in_specs = [pl.BlockSpec((tm, K), lambda j, i: (i, 0))]
    in_specs += [vec, vec] if with_ln else []
    in_specs += [pl.BlockSpec((1, K, tn), lambda j, i: (layer, 0, j), pipeline_mode=pl.Buffered(1))]
    args = (a,) + ((ln[0].reshape(1, K), ln[1].reshape(1, K)) if with_ln else ()) + (w_in,)
    return pl.pallas_call(
        functools.partial(_in_proj_kernel, with_ln=with_ln), grid=(n_cols // tn, M // tm),
        in_specs=in_specs, out_specs=pl.BlockSpec((tm, tn), lambda j, i: (i, j)),
        out_shape=jax.ShapeDtypeStruct((M, n_cols), BF16),
        scratch_shapes=[pltpu.VMEM((K, tn), BF16)],
        compiler_params=_params("arbitrary", "arbitrary"), name="ln_in_proj" if with_ln else "in_proj",
    )(*args)


def _ret_kernel(q_ref, k_ref, v_ref, g_ref, cos_ref, sin_ref, gn_ref, o_ref, state_ref, *, n_chunks):
    @pl.when(pl.program_id(1) == 0)
    def _():
        state_ref[...] = jnp.zeros_like(state_ref)

    C, hd = CHUNK, RET_HD
    ti = lax.broadcasted_iota(jnp.int32, (C, C), 0)
    si = lax.broadcasted_iota(jnp.int32, (C, C), 1)
    dist = jnp.abs(ti - si).astype(F32)
    cpos = lax.broadcasted_iota(jnp.int32, (C, 1), 0).astype(F32)
    log_gamma = [math.log(1.0 - 2.0 ** (-5.0 - h)) for h in range(RET_HEADS)]
    inner_mask = [jnp.exp(lg * dist) for lg in log_gamma]
    k_decay = [jnp.exp(lg * (C - 1.0 - cpos)) for lg in log_gamma]
    q_decay = [jnp.exp(lg * (cpos + 1.0)) for lg in log_gamma]
    pairs = [(c, h) for c in range(n_chunks) for h in range(RET_HEADS)]
    rows = lambda c: slice(c * C, (c + 1) * C)
    cols = lambda h: slice(h * hd, (h + 1) * hd)

    def rotated(ref, c, h):
        t = ref[0, rows(c), cols(h)].astype(F32)
        return t * cos_ref[rows(c), :] + pltpu.roll(t, hd // 2, axis=1) * sin_ref[rows(c), :]

    qr = [rotated(q_ref, c, h) for c, h in pairs]
    kr = [rotated(k_ref, c, h) * (hd ** -0.5) for c, h in pairs]
    v = [v_ref[0, rows(c), cols(h)] for c, h in pairs]
    scores = [_dot_nt(a, b) * inner_mask[h] for a, b, (c, h) in zip(qr, kr, pairs)]
    inner = [_dot(s, t) for s, t in zip(scores, v)]
    kv = [_dot_tn(b * k_decay[h], t) for b, t, (c, h) in zip(kr, v, pairs)]
    states = []
    for h in range(RET_HEADS):
        state = state_ref[h]
        for c in range(n_chunks):
            states.append(((c, h), state))
            state = math.exp(log_gamma[h] * C) * state + kv[pairs.index((c, h))]
        state_ref[h] = state
    entering = dict(states)
    cross = [_dot(a * q_decay[h], entering[(c, h)]) for a, (c, h) in zip(qr, pairs)]
    for y_in, y_cr, (c, h) in zip(inner, cross, pairs):
        y = y_in + y_cr
        mu = jnp.mean(y, axis=-1, keepdims=True)
        d = y - mu
        var = jnp.mean(d * d, axis=-1, keepdims=True)
        yn = d * lax.rsqrt(var + LN_EPS) * gn_ref[:, cols(h)]
        g = g_ref[0, rows(c), cols(h)].astype(F32)
        o_ref[0, rows(c), cols(h)] = (g * jax.nn.sigmoid(g) * yn).astype(o_ref.dtype)


def _retention(p_ret, cos2, sin2, gn_g, ts=256):
    B, S, _ = p_ret.shape
    W = gn_g.shape[-1]
    blk = lambda c: pl.BlockSpec((1, ts, W), lambda b, n, c=c: (b, n, c))
    return pl.pallas_call(
        functools.partial(_ret_kernel, n_chunks=ts // CHUNK), grid=(B, S // ts),
        in_specs=[blk(0), blk(1), blk(2), blk(3),
                  pl.BlockSpec((ts, RET_HD), lambda b, n: (n, 0)),
                  pl.BlockSpec((ts, RET_HD), lambda b, n: (n, 0)),
                  pl.BlockSpec((1, W), lambda b, n: (0, 0))],
        out_specs=pl.BlockSpec((1, ts, W), lambda b, n: (b, n, 0)),
        out_shape=jax.ShapeDtypeStruct((B, S, W), BF16),
        scratch_shapes=[pltpu.VMEM((RET_HEADS, RET_HD, RET_HD), F32)],
        compiler_params=_params("parallel", "arbitrary"), name="retention",
    )(p_ret, p_ret, p_ret, p_ret, cos2, sin2, gn_g.reshape(1, W))


def _head_sum(x):
    R, W = x.shape
    lo = lax.broadcasted_iota(jnp.int32, (R, LANES), 1) < RWKV_HD
    outs = []
    for j in range(W // LANES):
        xs = x[:, j * LANES:(j + 1) * LANES]
        s_lo = jnp.sum(jnp.where(lo, xs, 0.0), axis=-1, keepdims=True)
        s_hi = jnp.sum(jnp.where(lo, 0.0, xs), axis=-1, keepdims=True)
        outs.append(jnp.where(lo, s_lo, s_hi))
    return jnp.concatenate(outs, axis=1)


def _block_diag(x, head_of_lane):
    parts = [jnp.where(head_of_lane == h, x, 0.0) for h in range(HEADS_PER_GROUP)]
    return jnp.concatenate(parts, axis=0).astype(BF16)


def _packed_mm(xs, ys, head_of_lane):
    bds = [_block_diag(y, head_of_lane) for y in ys]
    return [jnp.dot(x.astype(BF16), bd, preferred_element_type=F32) for x, bd in zip(xs, bds)]


def _unit_lower_inverse(a_list, t_idx, s_idx, head_of_lane):
    mm = functools.partial(_packed_mm, head_of_lane=head_of_lane)
    add = lambda xs, ys: [x + y for x, y in zip(xs, ys)]
    lower = s_idx < t_idx
    same4 = (t_idx >> 2) == (s_idx >> 2)
    same16 = (t_idx >> 4) == (s_idx >> 4)
    m1 = lower & same4
    m2 = lower & same16 & jnp.logical_not(same4)
    m3 = lower & jnp.logical_not(same16)
    eye = jnp.where(t_idx == s_idx, 1.0, 0.0)
    a1 = [jnp.where(m1, a, 0.0) for a in a_list]
    a1sq = mm(a1, a1)
    a1cu = mm(a1, a1sq)
    inv = [eye + p + q + r for p, q, r in zip(a1, a1sq, a1cu)]
    for m in (m2, m3):
        n = mm(inv, [jnp.where(m, a, 0.0) for a in a_list])
        x = add(inv, mm(mm(n, n), inv))
        inv = add(x, mm(n, x))
    return inv


def _rwkv_kernel(*refs):
    state_ref = refs[-1]

    @pl.when(pl.program_id(1) == 0)
    def _():
        for ref in refs[-5:]:
            ref[...] = jnp.zeros_like(ref)

    seqs = [_rwkv_prepare(bi, *refs) for bi in range(state_ref.shape[0])]
    _rwkv_chains(seqs, state_ref)
    for bi, seq in enumerate(seqs):
        _rwkv_finish(bi, seq, *refs)


def _rwkv_prepare(bi, pr_ref, pk_ref, pv_ref, pl_ref, mur_ref, muk_ref, muv_ref, mul_ref,
                  wup_ref, aup_ref, gup_ref, w0_ref, a0_ref, kk_ref, ka_ref, rk_ref, lng_ref, lnb_ref,
                  o_ref, carry_r, carry_k, carry_v, carry_l, state_ref):
    C = CHUNK

    def shifted(p_ref, mu_ref, carry):
        p = p_ref[bi].astype(F32)
        row = lax.broadcasted_iota(jnp.int32, p.shape, 0)
        prev = jnp.where(row == 0, carry[bi, 0:1, :], pltpu.roll(p, 1, axis=0))
        carry[bi, 0:1, :] = p[C - 1:C, :]
        return p + mu_ref[...] * (prev - p)

    r = shifted(pr_ref, mur_ref, carry_r)
    k = shifted(pk_ref, muk_ref, carry_k)
    v = shifted(pv_ref, muv_ref, carry_v)
    lo = shifted(pl_ref, mul_ref, carry_l)
    dwa = lo[:, :LANES]
    dg = lo[:, LANES:]

    z = w0_ref[...] + _dot(jnp.tanh(dwa), wup_ref[...])
    logw = -math.exp(-0.5) * jax.nn.sigmoid(z)
    a = jax.nn.sigmoid(a0_ref[...] + _dot(dwa, aup_ref[...]))
    g = _dot(jax.nn.sigmoid(dg), gup_ref[...])

    kk = k * kk_ref[...]
    kk = kk * jnp.minimum(lax.rsqrt(_head_sum(kk * kk)), 1e12)
    k = k * (1.0 + (a - 1.0) * ka_ref[...])

    ti = lax.broadcasted_iota(jnp.int32, (C, C), 0)
    si = lax.broadcasted_iota(jnp.int32, (C, C), 1)
    tril = jnp.where(si <= ti, 1.0, 0.0).astype(BF16)
    lw_hi = logw.astype(BF16)
    lw_lo = (logw - lw_hi.astype(F32)).astype(BF16)
    cum = (jnp.dot(tril, lw_hi, preferred_element_type=F32)
           + jnp.dot(tril, lw_lo, preferred_element_type=F32))
    cum_last = cum[C - 1:C, :]
    e_neg = jnp.exp(-cum)
    abar = -kk * jnp.exp(cum - logw)
    rbar = r * jnp.exp(cum)
    kka = kk * a
    btil = kka * e_neg
    ktil = k * e_neg
    gamma_end = jnp.exp(cum_last)
    e_end = gamma_end * e_neg
    bend = kka * e_end
    kend = k * e_end
    return dict(r=r, k=k, v=v, g=g, abar=abar, rbar=rbar, btil=btil, ktil=ktil, bend=bend, kend=kend,
                gamma_end=gamma_end)


def _rwkv_chains(seqs, state_ref):
    C = CHUNK
    n_groups = state_ref.shape[1]
    t_idx = lax.broadcasted_iota(jnp.int32, (C, GROUP_W), 0)
    lane = lax.broadcasted_iota(jnp.int32, (C, GROUP_W), 1)
    s_idx = lane & (RWKV_HD - 1)
    head_of_lane = lane >> HD_SHIFT
    strict = s_idx < t_idx
    incl = s_idx <= t_idx
    bd_r = lax.broadcasted_iota(jnp.int32, (GROUP_W, GROUP_W), 0) >> HD_SHIFT
    bd_c = lax.broadcasted_iota(jnp.int32, (GROUP_W, GROUP_W), 1) >> HD_SHIFT
    on_diag = bd_r == bd_c

    chains = [(bi, gi) for bi in range(len(seqs)) for gi in range(n_groups)]
    part = lambda name: [seqs[bi][name][:, gi * GROUP_W:(gi + 1) * GROUP_W] for bi, gi in chains]
    mm = functools.partial(_packed_mm, head_of_lane=head_of_lane)
    v_g = part("v")
    lhs = [jnp.concatenate([p, q], axis=0) for p, q in zip(part("abar"), part("rbar"))]
    rhs = [jnp.concatenate([_block_diag(p, head_of_lane), _block_diag(q, head_of_lane)], axis=0)
           for p, q in zip(part("btil"), part("ktil"))]
    gram = [_dot_nt(p, q) for p, q in zip(lhs, rhs)]
    a_ab = [jnp.where(strict, gm[:C, :GROUP_W], 0.0) for gm in gram]
    a_ak = [jnp.where(strict, gm[:C, GROUP_W:], 0.0) for gm in gram]
    m_rb = [jnp.where(incl, gm[C:, :GROUP_W], 0.0) for gm in gram]
    m_rk = [jnp.where(incl, gm[C:, GROUP_W:], 0.0) for gm in gram]
    inv = _unit_lower_inverse(a_ab, t_idx, s_idx, head_of_lane)
    st = [state_ref[bi, gi] for bi, gi in chains]
    from_state = [_dot_nt(p, q) for p, q in zip(lhs, st)]
    local = mm([jnp.concatenate([p, q], axis=0) for p, q in zip(a_ak, m_rk)], v_g)
    u = mm(inv, [fs[:C] + lc[:C] for fs, lc in zip(from_state, local)])
    via_u = mm(m_rb, u)
    upd = [_dot_tn(jnp.concatenate([p, q], axis=0), jnp.concatenate([b, k], axis=0))
           for p, q, b, k in zip(u, v_g, part("bend"), part("kend"))]
    for (bi, gi), s_old, up, ge in zip(chains, st, upd, part("gamma_end")):
        state_ref[bi, gi] = s_old * ge + jnp.where(on_diag, up, 0.0)
    ys = [fs[C:] + p + lc[C:] for fs, p, lc in zip(from_state, via_u, local)]
    for bi, seq in enumerate(seqs):
        seq["y"] = jnp.concatenate(ys[bi * n_groups:(bi + 1) * n_groups], axis=1)


def _rwkv_finish(bi, seq, pr_ref, pk_ref, pv_ref, pl_ref, mur_ref, muk_ref, muv_ref, mul_ref,
                 wup_ref, aup_ref, gup_ref, w0_ref, a0_ref, kk_ref, ka_ref, rk_ref, lng_ref, lnb_ref,
                 o_ref, carry_r, carry_k, carry_v, carry_l, state_ref):
    y, r, k, v, g = seq["y"], seq["r"], seq["k"], seq["v"], seq["g"]
    inv_hd = 1.0 / RWKV_HD
    mu = _head_sum(y) * inv_hd
    d = y - mu
    var = _head_sum(d * d) * inv_hd
    yn = d * lax.rsqrt(var + RWKV_GN_EPS) * lng_ref[...] + lnb_ref[...]
    bonus = _head_sum(r * k * rk_ref[...]) * v
    o_ref[bi] = ((yn + bonus) * g).astype(o_ref.dtype)


def _rwkv(p_rwkv, col0, mu, w_up, w0, a_up, a0, g_up, k_k, k_a, r_k, ln_g, ln_b, nb=4):
    B, S, _ = p_rwkv.shape
    W = w0.shape[-1]
    C = CHUNK
    wup_pad = jnp.concatenate([w_up, jnp.zeros_like(a_up)], axis=0).astype(BF16)
    aup_pad = jnp.concatenate([jnp.zeros_like(w_up), a_up], axis=0).astype(BF16)
    row = lambda t: t.reshape(1, -1)
    vec = lambda n: pl.BlockSpec((1, n), lambda b, t: (0, 0))
    mat = lambda a: pl.BlockSpec(a.shape, lambda b, t: (0, 0))
    lw = mu.shape[-1] - 3 * W
    assert col0 % W == 0 and (col0 + 3 * W) % lw == 0 and B % nb == 0 and S % C == 0
    cblk = lambda c: pl.BlockSpec((nb, C, W), lambda b, t, c=c: (b, t, col0 // W + c))
    n_groups = W // GROUP_W
    gup = g_up.astype(BF16)
    return pl.pallas_call(
        _rwkv_kernel, grid=(B // nb, S // C),
        in_specs=[cblk(0), cblk(1), cblk(2),
                  pl.BlockSpec((nb, C, lw), lambda b, t: (b, t, (col0 + 3 * W) // lw)),
                  vec(W), vec(W), vec(W), vec(lw),
                  mat(wup_pad), mat(aup_pad), mat(gup),
                  vec(W), vec(W), vec(W), vec(W), vec(W), vec(W), vec(W)],
        out_specs=pl.BlockSpec((nb, C, W), lambda b, t: (b, t, 0)),
        out_shape=jax.ShapeDtypeStruct((B, S, W), BF16),
        scratch_shapes=[pltpu.VMEM((nb, 8, W), F32), pltpu.VMEM((nb, 8, W), F32), pltpu.VMEM((nb, 8, W), F32),
                        pltpu.VMEM((nb, 8, lw), F32),
                        pltpu.VMEM((nb, n_groups, GROUP_W, GROUP_W), F32)],
        compiler_params=_params("parallel", "arbitrary"), name="rwkv7",
    )(p_rwkv, p_rwkv, p_rwkv, p_rwkv,
      row(mu[:W]), row(mu[W:2 * W]), row(mu[2 * W:3 * W]), row(mu[3 * W:]),
      wup_pad, aup_pad, gup,
      row(w0), row(a0), row(k_k), row(k_a), row(r_k), row(ln_g), row(ln_b))


def _merge_kernel(*refs, alpha, with_ln):
    if with_ln:
        yr_ref, yw_ref, x_ref, pg_ref, pb_ref, wg_ref, wr_ref, ww_ref, wo_ref, g_ref, b_ref, o_ref = refs
    else:
        yr_ref, yw_ref, x_ref, wg_ref, wr_ref, ww_ref, wo_ref, g_ref, b_ref, o_ref = refs
    tm, D = x_ref.shape
    half = tm // 2

    def block_input(rs):
        x = x_ref[rs, :]
        return _ln(x, pg_ref[...], pb_ref[...]) if with_ln else x

    def branches(rs):
        return (jnp.dot(yr_ref[rs, :], wr_ref[...], preferred_element_type=F32),
                jnp.dot(yw_ref[rs, :], ww_ref[...], preferred_element_type=F32),
                jnp.dot(block_input(rs).astype(BF16), wg_ref[...], preferred_element_type=F32))

    def mix(rs, up):
        gate = jax.nn.sigmoid(up[2])
        return (gate[:, :D] * up[0] + gate[:, D:] * up[1]).astype(BF16)

    def out_proj(m):
        return jnp.dot(m, wo_ref[...], preferred_element_type=F32)

    def finish(rs, h):
        o_ref[rs, :] = _ln(alpha * block_input(rs) + h, g_ref[...], b_ref[...])

    rows_a, rows_b = slice(0, half), slice(half, tm)
    up_a = branches(rows_a)
    up_b = branches(rows_b)
    h_a = out_proj(mix(rows_a, up_a))
    h_b = out_proj(mix(rows_b, up_b))
    finish(rows_a, h_a)
    finish(rows_b, h_b)


def _merge(y_ret, y_rwkv, x, pre_ln, w_gate, w_ret_up, w_rwkv_up, w_out, ln_g, ln_b, alpha, tm=512):
    T, D = x.shape
    W = y_ret.shape[1]
    full = lambda a: pl.BlockSpec(a.shape, lambda i: (0, 0))
    rowblk = lambda n: pl.BlockSpec((tm, n), lambda i: (i, 0))
    vec = pl.BlockSpec((1, D), lambda i: (0, 0))
    with_ln = pre_ln is not None
    pre = (pre_ln[0].reshape(1, D), pre_ln[1].reshape(1, D)) if with_ln else ()
    return pl.pallas_call(
        functools.partial(_merge_kernel, alpha=alpha, with_ln=with_ln), grid=(T // tm,),
        in_specs=[rowblk(W), rowblk(W), rowblk(D)] + [vec] * len(pre)
        + [full(w_gate), full(w_ret_up), full(w_rwkv_up), full(w_out), vec, vec],
        out_specs=rowblk(D),
        out_shape=jax.ShapeDtypeStruct((T, D), F32),
        compiler_params=_params("parallel"), name="merge_out_ln",
    )(y_ret, y_rwkv, x, *pre, w_gate, w_ret_up, w_rwkv_up, w_out, ln_g.reshape(1, D), ln_b.reshape(1, D))


def _xattn_kernel(x_ref, kv_ref, wq_ref, wo_ref, g_ref, b_ref, rwt_ref, rb_ref, o_ref, cls_ref, *, alpha):
    tm, D = x_ref.shape[1:]
    hd = D // XA_HEADS
    half = tm // XA_ROW_PARTS

    def project(rs):
        return jnp.dot(x_ref[0, rs, :].astype(BF16), wq_ref[...], preferred_element_type=F32)

    def scores(q):
        return [_dot_nt(q[:, h * hd:(h + 1) * hd], kv_ref[0, :, h * hd:(h + 1) * hd]) * (hd ** -0.5)
                for h in range(XA_HEADS)]

    def attend(s_heads):
        outs = []
        for h, s in enumerate(s_heads):
            e = jnp.exp(s - jnp.max(s, axis=-1, keepdims=True))
            probs = e / jnp.sum(e, axis=-1, keepdims=True)
            outs.append(jnp.dot(probs.astype(BF16), kv_ref[0, :, D + h * hd:D + (h + 1) * hd],
                                preferred_element_type=F32))
        return jnp.concatenate(outs, axis=1)

    def out_proj(o):
        return jnp.dot(o.astype(BF16), wo_ref[...], preferred_element_type=F32)

    def finish(rs, h_out):
        x2 = _ln(alpha * x_ref[0, rs, :] + h_out, g_ref[...], b_ref[...])
        comb, cls = _route(x2, rwt_ref[...], rb_ref[...])
        E = comb.shape[0]
        comb_pad = jnp.concatenate([comb, jnp.zeros((LANES - E, half), F32)], axis=0)
        o_ref[0, rs, :D] = x2
        o_ref[0, rs, D:] = comb_pad.T
        cls_ref[0, :, rs] = cls

    rows = [slice(i * half, (i + 1) * half) for i in range(tm // half)]
    s_cur, h_prev = scores(project(rows[0])), None
    for i, rs in enumerate(rows):
        q_next = project(rows[i + 1]) if i + 1 < len(rows) else None
        o = attend(s_cur)
        if q_next is not None:
            s_cur = scores(q_next)
        if h_prev is not None:
            finish(rows[i - 1], h_prev)
        h_prev = out_proj(o)
    finish(rows[-1], h_prev)


def _cross_attention(x, kv, wq, wo, ln_g, ln_b, router_w, router_bias, alpha, tm=1024):
    B, S, D = x.shape
    M = kv.shape[1]
    E = router_w.shape[1]
    full = lambda a: pl.BlockSpec(a.shape, lambda b, i: (0, 0))
    n_t = S // tm
    return pl.pallas_call(
        functools.partial(_xattn_kernel, alpha=alpha), grid=(B, n_t),
        in_specs=[pl.BlockSpec((1, tm, D), lambda b, i: (b, i, 0)),
                  pl.BlockSpec((1, M, 2 * D), lambda b, i: (b, 0, 0)),
                  full(wq), full(wo),
                  pl.BlockSpec((1, D), lambda b, i: (0, 0)), pl.BlockSpec((1, D), lambda b, i: (0, 0)),
                  pl.BlockSpec((D, LANES), lambda b, i: (0, 0)), pl.BlockSpec((E, 1), lambda b, i: (0, 0))],
        out_specs=[pl.BlockSpec((1, tm, D + LANES), lambda b, i: (b, i, 0)),
                   pl.BlockSpec((1, 1, tm), lambda b, i: (b * n_t + i, 0, 0))],
        out_shape=[jax.ShapeDtypeStruct((B, S, D + LANES), F32),
                   jax.ShapeDtypeStruct((B * n_t, 1, tm), jnp.int32)],
        compiler_params=_params("parallel", "parallel"), name="xattn_ln_route",
    )(x, kv, wq, wo, ln_g.reshape(1, D), ln_b.reshape(1, D),
      jnp.pad(router_w, ((0, 0), (0, LANES - E))), router_bias.reshape(E, 1))


def _expert_pairs(per_group):
    todo = [(i, j) for i in range(per_group) for j in range(i + 1, per_group)]
    order = [todo.pop(0)]
    while todo:
        nxt = next((p for p in todo if set(p) & set(order[-1])), todo[0])
        todo.remove(nxt)
        order.append(nxt)
    return order


def _split_bf16(v):
    hi = v.astype(BF16)
    return hi, (v - hi.astype(F32)).astype(BF16)


def _route(x, w, bias):
    E = bias.shape[0]
    per_group = E // N_GROUPS
    x_hi, x_lo = _split_bf16(x)
    w_hi, w_lo = _split_bf16(w)
    logits = (jnp.dot(x_hi, w_hi, preferred_element_type=F32) + jnp.dot(x_hi, w_lo, preferred_element_type=F32)
              + jnp.dot(x_lo, w_hi, preferred_element_type=F32)).T[:E]
    aff = jax.nn.sigmoid(logits)
    choice = aff + bias
    rows = [choice[e:e + 1, :] for e in range(E)]
    scores = []
    for gidx in range(N_GROUPS):
        members = rows[gidx * per_group:(gidx + 1) * per_group]
        best = None
        for i in range(per_group):
            for j in range(i + 1, per_group):
                pair = members[i] + members[j]
                best = pair if best is None else jnp.maximum(best, pair)
        scores.append(best)
    top = scores[0]
    for s in scores[1:]:
        top = jnp.maximum(top, s)
    taken = jnp.zeros_like(top, dtype=jnp.bool_)
    in_best = []
    for s in scores:
        hit = jnp.logical_and(s == top, jnp.logical_not(taken))
        in_best.append(hit)
        taken = jnp.logical_or(taken, hit)
    sel_w = []
    cls = jnp.zeros(top.shape, jnp.int32)
    pairs = _expert_pairs(per_group)
    for gidx in range(N_GROUPS):
        members = rows[gidx * per_group:(gidx + 1) * per_group]
        chosen = []
        for i in range(per_group):
            rank = jnp.zeros_like(top)
            for j in range(per_group):
                if j == i:
                    continue
                ahead = (members[j] > members[i]) if j > i else (members[j] >= members[i])
                rank = rank + jnp.where(ahead, 1.0, 0.0)
            chosen.append(jnp.logical_and(in_best[gidx], rank < TOP_K))
            e = gidx * per_group + i
            sel_w.append(jnp.where(chosen[i], aff[e:e + 1, :], 0.0))
        for p, (i, j) in enumerate(pairs):
            cls = jnp.where(jnp.logical_and(chosen[i], chosen[j]), gidx * len(pairs) + p, cls)
    total = sel_w[0]
    for w in sel_w[1:]:
        total = total + w
    return jnp.concatenate(sel_w, axis=0) / total, cls


def _rank_kernel(cls_ref, rank_ref, cnt_ref, base_ref):
    @pl.when(pl.program_id(0) == 0)
    def _():
        base_ref[...] = jnp.zeros_like(base_ref)

    cls = cls_ref[0]
    tm = cls.shape[-1]
    n_cls = base_ref.shape[0]
    onehot = lax.broadcasted_iota(jnp.int32, (n_cls, tm), 0) == cls
    earlier = (lax.broadcasted_iota(jnp.int32, (tm, tm), 0)
               < lax.broadcasted_iota(jnp.int32, (tm, tm), 1))
    ones = jnp.where(onehot, 1.0, 0.0)
    before = jnp.dot(ones.astype(BF16), jnp.where(earlier, 1.0, 0.0).astype(BF16),
                     preferred_element_type=F32)
    base = base_ref[...]
    rank = jnp.sum(jnp.where(onehot, base + before, 0.0), axis=0, keepdims=True)
    rank_ref[0] = rank.astype(jnp.int32)
    total = base + jnp.sum(ones, axis=1, keepdims=True)
    base_ref[...] = total
    cnt_ref[...] = jnp.broadcast_to(total, cnt_ref.shape)


def _class_ranks(cls, n_cls_pad):
    n_t, _, tm = cls.shape
    return pl.pallas_call(
        _rank_kernel, grid=(n_t,),
        in_specs=[pl.BlockSpec((1, 1, tm), lambda i: (i, 0, 0))],
        out_specs=[pl.BlockSpec((1, 1, tm), lambda i: (i, 0, 0)),
                   pl.BlockSpec((n_cls_pad, LANES), lambda i: (0, 0))],
        out_shape=[jax.ShapeDtypeStruct((n_t, 1, tm), jnp.int32),
                   jax.ShapeDtypeStruct((n_cls_pad, LANES), F32)],
        scratch_shapes=[pltpu.VMEM((n_cls_pad, 1), F32)],
        compiler_params=_params("arbitrary"), name="moe_class_rank",
    )(cls)


def _row_copy(src, src_row, dst, dst_row, sem):
    return pltpu.make_async_copy(src.at[pl.ds(src_row, 1)], dst.at[pl.ds(dst_row, 1)], sem)


def _for_each_row(n_rows, fn):
    def group(g, carry):
        r0 = pl.multiple_of(g * SUBLANES, SUBLANES)
        for j in range(SUBLANES):
            fn(r0 + j)
        return carry

    lax.fori_loop(0, n_rows // SUBLANES, group, 0)


def _dispatch_kernel(dest_ref, x_ref, o_hbm, sem):
    tm = x_ref.shape[0]
    base = pl.program_id(0) * tm
    copy = lambda r: _row_copy(x_ref, r, o_hbm, dest_ref[base + r], sem)
    _for_each_row(tm, lambda r: copy(r).start())
    _for_each_row(tm, lambda r: copy(r).wait())


def _dispatch(xa, dest, tm=2048):
    T, W = xa.shape
    tm = min(tm, T)
    return pl.pallas_call(
        _dispatch_kernel,
        grid_spec=pltpu.PrefetchScalarGridSpec(
            num_scalar_prefetch=1, grid=(T // tm,),
            in_specs=[pl.BlockSpec((tm, W), lambda i, dest: (i, 0))],
            out_specs=pl.BlockSpec(memory_space=pl.ANY),
            scratch_shapes=[pltpu.SemaphoreType.DMA(())]),
        out_shape=jax.ShapeDtypeStruct((T, W), xa.dtype),
        compiler_params=_params("arbitrary"), name="moe_dispatch",
    )(dest, xa)


def _undispatch_kernel(dest_ref, y_hbm, o_ref, sem):
    tm = o_ref.shape[0]
    base = pl.program_id(0) * tm
    copy = lambda r: _row_copy(y_hbm, dest_ref[base + r], o_ref, r, sem)
    _for_each_row(tm, lambda r: copy(r).start())
    _for_each_row(tm, lambda r: copy(r).wait())


def _undispatch(ys, dest, tm=2048):
    T, D = ys.shape
    tm = min(tm, T)
    return pl.pallas_call(
        _undispatch_kernel,
        grid_spec=pltpu.PrefetchScalarGridSpec(
            num_scalar_prefetch=1, grid=(T // tm,),
            in_specs=[pl.BlockSpec(memory_space=pl.ANY)],
            out_specs=pl.BlockSpec((tm, D), lambda i, dest: (i, 0)),
            scratch_shapes=[pltpu.SemaphoreType.DMA(())]),
        out_shape=jax.ShapeDtypeStruct((T, D), ys.dtype),
        compiler_params=_params("arbitrary"), name="moe_undispatch",
    )(dest, ys)


STEP_FIRST, STEP_LAST, STEP_VALID, STEP_NEW_EXPERT = 1, 2, 4, 8


def _moe_kernel(tile_ref, exp_ref, flag_ref, xs_ref, wg_ref, wu_ref, wd_ref, g_ref, b_ref, o_ref,
                xbf_ref, acc_ref, wgb_ref, wub_ref, wdb_ref, *, alpha):
    step = pl.program_id(0)
    flags = flag_ref[step]
    e = exp_ref[step]
    D = o_ref.shape[-1]

    @pl.when((flags & STEP_NEW_EXPERT) != 0)
    def _():
        wgb_ref[...] = wg_ref[0, 0].astype(BF16)
        wub_ref[...] = wu_ref[0, 0].astype(BF16)

    @pl.when((flags & STEP_FIRST) != 0)
    def _():
        xbf_ref[...] = xs_ref[:, :D].astype(BF16)
        acc_ref[...] = jnp.zeros_like(acc_ref)

    @pl.when((flags & STEP_VALID) != 0)
    def _():
        half = xbf_ref.shape[0] // 2

        def up(rs):
            xb = xbf_ref[rs, :]
            return (jnp.dot(xb, wgb_ref[...], preferred_element_type=F32),
                    jnp.dot(xb, wub_ref[...], preferred_element_type=F32))

        def act(hg, hu):
            return (hg * jax.nn.sigmoid(hg) * hu).astype(BF16)

        def down(h):
            return jnp.dot(h, wdb_ref[...], preferred_element_type=F32)

        def accumulate(rs, y):
            comb = xs_ref[rs, D:]
            lane = lax.broadcasted_iota(jnp.int32, comb.shape, 1)
            w_e = jnp.sum(jnp.where(lane == e, comb, 0.0), axis=-1, keepdims=True)
            acc_ref[rs, :] += w_e * y

        rows_a, rows_b = slice(0, half), slice(half, 2 * half)
        up_a = up(rows_a)
        up_b = up(rows_b)
        wdb_ref[...] = wd_ref[0, 0].astype(BF16)
        y_a = down(act(*up_a))
        y_b = down(act(*up_b))
        accumulate(rows_a, y_a)
        accumulate(rows_b, y_b)

    @pl.when((flags & STEP_LAST) != 0)
    def _():
        o_ref[...] = _ln(alpha * xs_ref[:, :D] + acc_ref[...], g_ref[...], b_ref[...])


def _moe_steps(counts, n_experts, n_tiles, tm):
    n_cls = counts.shape[0]
    per_group = n_experts // N_GROUPS
    pairs = _expert_pairs(per_group)
    uses = [[0.0] * n_experts for _ in range(n_cls)]
    for c in range(n_cls):
        gidx, (i, j) = c // len(pairs), pairs[c % len(pairs)]
        uses[c][gidx * per_group + i] = 1.0
        uses[c][gidx * per_group + j] = 1.0
    uses = jnp.array(uses, F32)
    ends = jnp.cumsum(counts)
    starts = ends - counts
    lo = (jnp.arange(n_tiles, dtype=jnp.int32) * tm)[:, None]
    present = jnp.logical_and(starts[None, :] < lo + tm, ends[None, :] > lo)
    need = jnp.dot(present.astype(F32), uses) > 0
    odd = (jnp.arange(n_tiles, dtype=jnp.int32) % 2 == 1)[:, None]
    need = jnp.where(odd, need[:, ::-1], need).reshape(-1)
    n_steps = min(n_tiles * n_experts, TOP_K * (n_tiles + n_cls - 1))
    n_valid = jnp.sum(need.astype(jnp.int32))
    idx = jnp.nonzero(need, size=n_steps, fill_value=0)[0].astype(jnp.int32)
    k = jnp.arange(n_steps, dtype=jnp.int32)
    valid = k < n_valid
    idx = jnp.where(valid, idx, idx[n_valid - 1])
    tile, expert = idx // n_experts, idx % n_experts
    expert = jnp.where(tile % 2 == 1, n_experts - 1 - expert, expert)
    prev_tile = jnp.concatenate([jnp.full((1,), -1, jnp.int32), tile[:-1]])
    next_tile = jnp.concatenate([tile[1:], jnp.full((1,), -1, jnp.int32)])
    first = jnp.logical_and(valid, tile != prev_tile)
    last = jnp.logical_and(valid, jnp.logical_or(tile != next_tile, k == n_valid - 1))
    prev_expert = jnp.concatenate([jnp.full((1,), -1, jnp.int32), expert[:-1]])
    new_expert = jnp.logical_and(valid, expert != prev_expert)
    flags = (first * STEP_FIRST + last * STEP_LAST + valid * STEP_VALID
             + new_expert * STEP_NEW_EXPERT).astype(jnp.int32)
    return tile, expert, flags


def _moe_experts(xs, steps, layer, w_gate, w_up, w_down, ln_g, ln_b, alpha, tm):
    T, W = xs.shape
    D = W - LANES
    F = w_gate.shape[-1]
    tile, expert, flags = steps
    return pl.pallas_call(
        functools.partial(_moe_kernel, alpha=alpha),
        grid_spec=pltpu.PrefetchScalarGridSpec(
            num_scalar_prefetch=3, grid=(tile.shape[0],),
            in_specs=[pl.BlockSpec((tm, W), lambda s, t, e, f: (t[s], 0)),
                      pl.BlockSpec((1, 1, D, F), lambda s, t, e, f: (layer, e[s], 0, 0)),
                      pl.BlockSpec((1, 1, D, F), lambda s, t, e, f: (layer, e[s], 0, 0)),
                      pl.BlockSpec((1, 1, F, D), lambda s, t, e, f: (layer, e[s], 0, 0)),
                      pl.BlockSpec((1, D), lambda s, t, e, f: (0, 0)),
                      pl.BlockSpec((1, D), lambda s, t, e, f: (0, 0))],
            out_specs=pl.BlockSpec((tm, D), lambda s, t, e, f: (t[s], 0)),
            scratch_shapes=[pltpu.VMEM((tm, D), BF16), pltpu.VMEM((tm, D), F32),
                            pltpu.VMEM((D, F), BF16), pltpu.VMEM((D, F), BF16), pltpu.VMEM((F, D), BF16)]),
        out_shape=jax.ShapeDtypeStruct((T, D), F32),
        compiler_params=_params("arbitrary"), name="moe_experts_ln",
    )(tile, expert, flags, xs, w_gate, w_up, w_down, ln_g.reshape(1, D), ln_b.reshape(1, D))


def _moe(xa, cls, layer, w_gate, w_up, w_down, ln_g, ln_b, alpha, tm=512):
    T = xa.shape[0]
    E = w_gate.shape[1]
    n_cls = N_GROUPS * len(_expert_pairs(E // N_GROUPS))
    rank, cnt = _class_ranks(cls, -(-n_cls // 8) * 8)
    counts = cnt[:n_cls, 0].astype(jnp.int32)
    starts = jnp.cumsum(counts) - counts
    dest = (starts[cls.reshape(T)] + rank.reshape(T)).astype(jnp.int32)
    xs = _dispatch(xa, dest)
    ys = _moe_experts(xs, _moe_steps(counts, E, T // tm, tm), layer, w_gate, w_up, w_down, ln_g, ln_b, alpha, tm)
    return _undispatch(ys, dest)


def _rotary_tables(S):
    half = RET_HD // 2
    inv_freq = ROPE_BASE ** (-jnp.arange(half, dtype=F32) / half)
    ang = jnp.arange(S).astype(F32)[:, None] * inv_freq[None, :]
    cos, sin = jnp.cos(ang), jnp.sin(ang)
    return jnp.concatenate([cos, cos], axis=1), jnp.concatenate([-sin, sin], axis=1)


def kernel(x, mem, ln_in_g, ln_in_b, router_w, router_bias, w_in, ret_gn_g, rwkv_mu, rwkv_w_up, rwkv_w0,
           rwkv_a_up, rwkv_a0, rwkv_g_up, rwkv_k_k, rwkv_k_a, rwkv_r_k, rwkv_ln_g, rwkv_ln_b,
           w_ret_up, w_rwkv_up, w_out, ln1_g, ln1_b, xa_wq, xa_wkv, xa_wo, ln2_g, ln2_b,
           moe_w_gate, moe_w_up, moe_w_down, ln3_g, ln3_b):
    B, S, D = x.shape
    T = B * S
    depth = w_in.shape[0]
    alpha = (2 * depth) ** 0.25
    ret_w = ret_gn_g.shape[-1]
    rwkv_w = rwkv_w0.shape[-1]
    ret_cols = 4 * ret_w
    rwkv_cols = rwkv_mu.shape[-1]
    n_proj = -(-(ret_cols + rwkv_cols) // IN_PROJ_TN) * IN_PROJ_TN
    cos2, sin2 = _rotary_tables(S)
    mem2 = mem.reshape(B * mem.shape[1], D)

    xs = x.reshape(T, D)
    for l in range(depth):
        w_l = w_in[l]
        entry_ln = (ln_in_g, ln_in_b) if l == 0 else None
        p3 = _in_proj(xs, w_in, l, n_proj, ln=entry_ln).reshape(B, S, n_proj)
        y_ret = _retention(p3, cos2, sin2, ret_gn_g[l])
        y_rwkv = _rwkv(p3, ret_cols, rwkv_mu[l], rwkv_w_up[l], rwkv_w0[l], rwkv_a_up[l],
                       rwkv_a0[l], rwkv_g_up[l], rwkv_k_k[l], rwkv_k_a[l], rwkv_r_k[l].reshape(-1),
                       rwkv_ln_g[l], rwkv_ln_b[l])
        xs = _merge(y_ret.reshape(T, ret_w), y_rwkv.reshape(T, rwkv_w), xs, entry_ln,
                    w_l[:, ret_cols + rwkv_cols:].astype(BF16),
                    w_ret_up[l].astype(BF16), w_rwkv_up[l].astype(BF16), w_out[l].astype(BF16),
                    ln1_g[l], ln1_b[l], alpha)
        kv = _matmul(mem2, xa_wkv[l].astype(BF16), 512, 512, BF16, "xattn_kv")
        xa, cls = _cross_attention(xs.reshape(B, S, D), kv.reshape(B, -1, 2 * D), xa_wq[l].astype(BF16),
                                   xa_wo[l].astype(BF16), ln2_g[l], ln2_b[l], router_w, router_bias, alpha)
        xs = _moe(xa.reshape(T, D + LANES), cls, l, moe_w_gate, moe_w_up, moe_w_down, ln3_g[l], ln3_b[l], alpha)
    return xs.reshape(B, S, D)
```

```python
import functools
import math

import jax
import jax.numpy as jnp
from jax import lax
from jax.experimental import pallas as pl
from jax.experimental.pallas import tpu as pltpu

F32 = jnp.float32
BF16 = jnp.bfloat16

LANES = 128
SUBLANES = 8
VMEM_LIMIT = 56 * 1024 * 1024

CHUNK = 64
RET_HEADS = 4
RET_HD = 128
ROPE_BASE = 10000.0
RWKV_HD = 64
HD_SHIFT = 6
RWKV_GN_EPS = 64e-5
XA_HEADS = 4
XA_ROW_PARTS = 4
N_GROUPS = 4
TOP_K = 2
LN_EPS = 1e-5
IN_PROJ_TN = 2048
IN_PROJ_ROW_PARTS = 2
HEADS_PER_GROUP = 4
GROUP_W = HEADS_PER_GROUP * RWKV_HD


def _params(*sem):
    return pltpu.CompilerParams(dimension_semantics=sem, vmem_limit_bytes=VMEM_LIMIT)


def _ln(v, g, b, eps=LN_EPS):
    mu = jnp.mean(v, axis=-1, keepdims=True)
    d = v - mu
    var = jnp.mean(d * d, axis=-1, keepdims=True)
    return d * lax.rsqrt(var + eps) * g + b


def _dot(a, b):
    return jnp.dot(a.astype(BF16), b.astype(BF16), preferred_element_type=F32)


def _dot_nt(a, b):
    return lax.dot_general(a.astype(BF16), b.astype(BF16), (((1,), (1,)), ((), ())),
                           preferred_element_type=F32)


def _dot_tn(a, b):
    return lax.dot_general(a.astype(BF16), b.astype(BF16), (((0,), (0,)), ((), ())),
                           preferred_element_type=F32)


def _in_proj_kernel(*refs, with_ln):
    if with_ln:
        a_ref, g_ref, b_ref, w_ref, o_ref, wbf_ref = refs
    else:
        a_ref, w_ref, o_ref, wbf_ref = refs

    @pl.when(pl.program_id(1) == 0)
    def _():
        wbf_ref[...] = w_ref[0].astype(BF16)

    tm = a_ref.shape[0]
    part = tm // IN_PROJ_ROW_PARTS

    def prepared(rs):
        a = a_ref[rs, :]
        if with_ln:
            a = _ln(a, g_ref[...], b_ref[...])
        return a.astype(BF16)

    rows = [slice(i * part, (i + 1) * part) for i in range(IN_PROJ_ROW_PARTS)]
    cur = prepared(rows[0])
    for i, rs in enumerate(rows):
        nxt = prepared(rows[i + 1]) if i + 1 < len(rows) else None
        o_ref[rs, :] = jnp.dot(cur, wbf_ref[...], preferred_element_type=F32).astype(o_ref.dtype)
        cur = nxt


def _in_proj(a, w_in, layer, n_cols, ln=None, tm=1024):
    M, K = a.shape
    tn = IN_PROJ_TN
    assert n_cols % tn == 0 and n_cols <= w_in.shape[-1] and M % tm == 0
    with_ln = ln is not None
    vec = pl.BlockSpec((1, K), lambda j, i: (0, 0))
    in_specs = [pl.BlockSpec((tm, K), lambda j, i: (i, 0))]
    in_specs += [vec, vec] if with_ln else []
    in_specs += [pl.BlockSpec((1, K, tn), lambda j, i: (layer, 0, j), pipeline_mode=pl.Buffered(1))]
    args = (a,) + ((ln[0].reshape(1, K), ln[1].reshape(1, K)) if with_ln else ()) + (w_in,)
    return pl.pallas_call(
        functools.partial(_in_proj_kernel, with_ln=with_ln), grid=(n_cols // tn, M // tm),
        in_specs=in_specs, out_specs=pl.BlockSpec((tm, tn), lambda j, i: (i, j)),
        out_shape=jax.ShapeDtypeStruct((M, n_cols), BF16),
        scratch_shapes=[pltpu.VMEM((K, tn), BF16)],
        compiler_params=_params("arbitrary", "arbitrary"), name="ln_in_proj" if with_ln else "in_proj",
    )(*args)


def _ret_kernel(q_ref, k_ref, v_ref, g_ref, cos_ref, sin_ref, gn_ref, o_ref, state_ref, *, n_chunks):
    @pl.when(pl.program_id(1) == 0)
    def _():
        state_ref[...] = jnp.zeros_like(state_ref)

    C, hd = CHUNK, RET_HD
    ti = lax.broadcasted_iota(jnp.int32, (C, C), 0)
    si = lax.broadcasted_iota(jnp.int32, (C, C), 1)
    dist = jnp.abs(ti - si).astype(F32)
    cpos = lax.broadcasted_iota(jnp.int32, (C, 1), 0).astype(F32)
    log_gamma = [math.log(1.0 - 2.0 ** (-5.0 - h)) for h in range(RET_HEADS)]
    inner_mask = [jnp.exp(lg * dist) for lg in log_gamma]
    k_decay = [jnp.exp(lg * (C - 1.0 - cpos)) for lg in log_gamma]
    q_decay = [jnp.exp(lg * (cpos + 1.0)) for lg in log_gamma]
    pairs = [(c, h) for c in range(n_chunks) for h in range(RET_HEADS)]
    rows = lambda c: slice(c * C, (c + 1) * C)
    cols = lambda h: slice(h * hd, (h + 1) * hd)

    def rotated(ref, c, h):
        t = ref[0, rows(c), cols(h)].astype(F32)
        return t * cos_ref[rows(c), :] + pltpu.roll(t, hd // 2, axis=1) * sin_ref[rows(c), :]

    qr = [rotated(q_ref, c, h) for c, h in pairs]
    kr = [rotated(k_ref, c, h) * (hd ** -0.5) for c, h in pairs]
    v = [v_ref[0, rows(c), cols(h)] for c, h in pairs]
    scores = [_dot_nt(a, b) * inner_mask[h] for a, b, (c, h) in zip(qr, kr, pairs)]
    inner = [_dot(s, t) for s, t in zip(scores, v)]
    kv = [_dot_tn(b * k_decay[h], t) for b, t, (c, h) in zip(kr, v, pairs)]
    states = []
    for h in range(RET_HEADS):
        state = state_ref[h]
        for c in range(n_chunks):
            states.append(((c, h), state))
            state = math.exp(log_gamma[h] * C) * state + kv[pairs.index((c, h))]
        state_ref[h] = state
    entering = dict(states)
    cross = [_dot(a * q_decay[h], entering[(c, h)]) for a, (c, h) in zip(qr, pairs)]
    for y_in, y_cr, (c, h) in zip(inner, cross, pairs):
        y = y_in + y_cr
        mu = jnp.mean(y, axis=-1, keepdims=True)
        d = y - mu
        var = jnp.mean(d * d, axis=-1, keepdims=True)
        yn = d * lax.rsqrt(var + LN_EPS) * gn_ref[:, cols(h)]
        g = g_ref[0, rows(c), cols(h)].astype(F32)
        o_ref[0, rows(c), cols(h)] = (g * jax.nn.sigmoid(g) * yn).astype(o_ref.dtype)


def _retention(p_ret, cos2, sin2, gn_g, ts=256):
    B, S, _ = p_ret.shape
    W = gn_g.shape[-1]
    blk = lambda c: pl.BlockSpec((1, ts, W), lambda b, n, c=c: (b, n, c))
    return pl.pallas_call(
        functools.partial(_ret_kernel, n_chunks=ts // CHUNK), grid=(B, S // ts),
        in_specs=[blk(0), blk(1), blk(2), blk(3),
                  pl.BlockSpec((ts, RET_HD), lambda b, n: (n, 0)),
                  pl.BlockSpec((ts, RET_HD), lambda b, n: (n, 0)),
                  pl.BlockSpec((1, W), lambda b, n: (0, 0))],
        out_specs=pl.BlockSpec((1, ts, W), lambda b, n: (b, n, 0)),
        out_shape=jax.ShapeDtypeStruct((B, S, W), BF16),
        scratch_shapes=[pltpu.VMEM((RET_HEADS, RET_HD, RET_HD), F32)],
        compiler_params=_params("parallel", "arbitrary"), name="retention",
    )(p_ret, p_ret, p_ret, p_ret, cos2, sin2, gn_g.reshape(1, W))


def _head_sum(x):
    R, W = x.shape
    lo = lax.broadcasted_iota(jnp.int32, (R, LANES), 1) < RWKV_HD
    outs = []
    for j in range(W // LANES):
        xs = x[:, j * LANES:(j + 1) * LANES]
        s_lo = jnp.sum(jnp.where(lo, xs, 0.0), axis=-1, keepdims=True)
        s_hi = jnp.sum(jnp.where(lo, 0.0, xs), axis=-1, keepdims=True)
        outs.append(jnp.where(lo, s_lo, s_hi))
    return jnp.concatenate(outs, axis=1)


def _block_diag(x, head_of_lane):
    parts = [jnp.where(head_of_lane == h, x, 0.0) for h in range(HEADS_PER_GROUP)]
    return jnp.concatenate(parts, axis=0).astype(BF16)


def _packed_mm(xs, ys, head_of_lane):
    bds = [_block_diag(y, head_of_lane) for y in ys]
    return [jnp.dot(x.astype(BF16), bd, preferred_element_type=F32) for x, bd in zip(xs, bds)]


def _unit_lower_inverse(a_list, t_idx, s_idx, head_of_lane):
    mm = functools.partial(_packed_mm, head_of_lane=head_of_lane)
    add = lambda xs, ys: [x + y for x, y in zip(xs, ys)]
    lower = s_idx < t_idx
    same4 = (t_idx >> 2) == (s_idx >> 2)
    same16 = (t_idx >> 4) == (s_idx >> 4)
    m1 = lower & same4
    m2 = lower & same16 & jnp.logical_not(same4)
    m3 = lower & jnp.logical_not(same16)
    eye = jnp.where(t_idx == s_idx, 1.0, 0.0)
    a1 = [jnp.where(m1, a, 0.0) for a in a_list]
    a1sq = mm(a1, a1)
    a1cu = mm(a1, a1sq)
    inv = [eye + p + q + r for p, q, r in zip(a1, a1sq, a1cu)]
    for m in (m2, m3):
        n = mm(inv, [jnp.where(m, a, 0.0) for a in a_list])
        x = add(inv, mm(mm(n, n), inv))
        inv = add(x, mm(n, x))
    return inv


def _rwkv_kernel(*refs):
    state_ref = refs[-1]

    @pl.when(pl.program_id(1) == 0)
    def _():
        for ref in refs[-5:]:
            ref[...] = jnp.zeros_like(ref)

    seqs = [_rwkv_prepare(bi, *refs) for bi in range(state_ref.shape[0])]
    _rwkv_chains(seqs, state_ref)
    for bi, seq in enumerate(seqs):
        _rwkv_finish(bi, seq, *refs)


def _rwkv_prepare(bi, pr_ref, pk_ref, pv_ref, pl_ref, mur_ref, muk_ref, muv_ref, mul_ref,
                  wup_ref, aup_ref, gup_ref, w0_ref, a0_ref, kk_ref, ka_ref, rk_ref, lng_ref, lnb_ref,
                  o_ref, carry_r, carry_k, carry_v, carry_l, state_ref):
    C = CHUNK

    def shifted(p_ref, mu_ref, carry):
        p = p_ref[bi].astype(F32)
        row = lax.broadcasted_iota(jnp.int32, p.shape, 0)
        prev = jnp.where(row == 0, carry[bi, 0:1, :], pltpu.roll(p, 1, axis=0))
        carry[bi, 0:1, :] = p[C - 1:C, :]
        return p + mu_ref[...] * (prev - p)

    r = shifted(pr_ref, mur_ref, carry_r)
    k = shifted(pk_ref, muk_ref, carry_k)
    v = shifted(pv_ref, muv_ref, carry_v)
    lo = shifted(pl_ref, mul_ref, carry_l)
    dwa = lo[:, :LANES]
    dg = lo[:, LANES:]

    z = w0_ref[...] + _dot(jnp.tanh(dwa), wup_ref[...])
    logw = -math.exp(-0.5) * jax.nn.sigmoid(z)
    a = jax.nn.sigmoid(a0_ref[...] + _dot(dwa, aup_ref[...]))
    g = _dot(jax.nn.sigmoid(dg), gup_ref[...])

    kk = k * kk_ref[...]
    kk = kk * jnp.minimum(lax.rsqrt(_head_sum(kk * kk)), 1e12)
    k = k * (1.0 + (a - 1.0) * ka_ref[...])

    ti = lax.broadcasted_iota(jnp.int32, (C, C), 0)
    si = lax.broadcasted_iota(jnp.int32, (C, C), 1)
    tril = jnp.where(si <= ti, 1.0, 0.0).astype(BF16)
    lw_hi = logw.astype(BF16)
    lw_lo = (logw - lw_hi.astype(F32)).astype(BF16)
    cum = (jnp.dot(tril, lw_hi, preferred_element_type=F32)
           + jnp.dot(tril, lw_lo, preferred_element_type=F32))
    cum_last = cum[C - 1:C, :]
    e_neg = jnp.exp(-cum)
    abar = -kk * jnp.exp(cum - logw)
    rbar = r * jnp.exp(cum)
    kka = kk * a
    btil = kka * e_neg
    ktil = k * e_neg
    gamma_end = jnp.exp(cum_last)
    e_end = gamma_end * e_neg
    bend = kka * e_end
    kend = k * e_end
    return dict(r=r, k=k, v=v, g=g, abar=abar, rbar=rbar, btil=btil, ktil=ktil, bend=bend, kend=kend,
                gamma_end=gamma_end)


def _rwkv_chains(seqs, state_ref):
    C = CHUNK
    n_groups = state_ref.shape[1]
    t_idx = lax.broadcasted_iota(jnp.int32, (C, GROUP_W), 0)
    lane = lax.broadcasted_iota(jnp.int32, (C, GROUP_W), 1)
    s_idx = lane & (RWKV_HD - 1)
    head_of_lane = lane >> HD_SHIFT
    strict = s_idx < t_idx
    incl = s_idx <= t_idx
    bd_r = lax.broadcasted_iota(jnp.int32, (GROUP_W, GROUP_W), 0) >> HD_SHIFT
    bd_c = lax.broadcasted_iota(jnp.int32, (GROUP_W, GROUP_W), 1) >> HD_SHIFT
    on_diag = bd_r == bd_c

    chains = [(bi, gi) for bi in range(len(seqs)) for gi in range(n_groups)]
    part = lambda name: [seqs[bi][name][:, gi * GROUP_W:(gi + 1) * GROUP_W] for bi, gi in chains]
    mm = functools.partial(_packed_mm, head_of_lane=head_of_lane)
    v_g = part("v")
    lhs = [jnp.concatenate([p, q], axis=0) for p, q in zip(part("abar"), part("rbar"))]
    rhs = [jnp.concatenate([_block_diag(p, head_of_lane), _block_diag(q, head_of_lane)], axis=0)
           for p, q in zip(part("btil"), part("ktil"))]
    gram = [_dot_nt(p, q) for p, q in zip(lhs, rhs)]
    a_ab = [jnp.where(strict, gm[:C, :GROUP_W], 0.0) for gm in gram]
    a_ak = [jnp.where(strict, gm[:C, GROUP_W:], 0.0) for gm in gram]
    m_rb = [jnp.where(incl, gm[C:, :GROUP_W], 0.0) for gm in gram]
    m_rk = [jnp.where(incl, gm[C:, GROUP_W:], 0.0) for gm in gram]
    inv = _unit_lower_inverse(a_ab, t_idx, s_idx, head_of_lane)
    st = [state_ref[bi, gi] for bi, gi in chains]
    from_state = [_dot_nt(p, q) for p, q in zip(lhs, st)]
    local = mm([jnp.concatenate([p, q], axis=0) for p, q in zip(a_ak, m_rk)], v_g)
    u = mm(inv, [fs[:C] + lc[:C] for fs, lc in zip(from_state, local)])
    via_u = mm(m_rb, u)
    upd = [_dot_tn(jnp.concatenate([p, q], axis=0), jnp.concatenate([b, k], axis=0))
           for p, q, b, k in zip(u, v_g, part("bend"), part("kend"))]
    for (bi, gi), s_old, up, ge in zip(chains, st, upd, part("gamma_end")):
        state_ref[bi, gi] = s_old * ge + jnp.where(on_diag, up, 0.0)
    ys = [fs[C:] + p + lc[C:] for fs, p, lc in zip(from_state, via_u, local)]
    for bi, seq in enumerate(seqs):
        seq["y"] = jnp.concatenate(ys[bi * n_groups:(bi + 1) * n_groups], axis=1)


def _rwkv_finish(bi, seq, pr_ref, pk_ref, pv_ref, pl_ref, mur_ref, muk_ref, muv_ref, mul_ref,
                 wup_ref, aup_ref, gup_ref, w0_ref, a0_ref, kk_ref, ka_ref, rk_ref, lng_ref, lnb_ref,
                 o_ref, carry_r, carry_k, carry_v, carry_l, state_ref):
    y, r, k, v, g = seq["y"], seq["r"], seq["k"], seq["v"], seq["g"]
    inv_hd = 1.0 / RWKV_HD
    mu = _head_sum(y) * inv_hd
    d = y - mu
    var = _head_sum(d * d) * inv_hd
    yn = d * lax.rsqrt(var + RWKV_GN_EPS) * lng_ref[...] + lnb_ref[...]
    bonus = _head_sum(r * k * rk_ref[...]) * v
    o_ref[bi] = ((yn + bonus) * g).astype(o_ref.dtype)


def _rwkv(p_rwkv, col0, mu, w_up, w0, a_up, a0, g_up, k_k, k_a, r_k, ln_g, ln_b, nb=8):
    B, S, _ = p_rwkv.shape
    W = w0.shape[-1]
    C = CHUNK
    wup_pad = jnp.concatenate([w_up, jnp.zeros_like(a_up)], axis=0).astype(BF16)
    aup_pad = jnp.concatenate([jnp.zeros_like(w_up), a_up], axis=0).astype(BF16)
    row = lambda t: t.reshape(1, -1)
    vec = lambda n: pl.BlockSpec((1, n), lambda b, t: (0, 0))
    mat = lambda a: pl.BlockSpec(a.shape, lambda b, t: (0, 0))
    lw = mu.shape[-1] - 3 * W
    assert col0 % W == 0 and (col0 + 3 * W) % lw == 0 and B % nb == 0 and S % C == 0
    cblk = lambda c: pl.BlockSpec((nb, C, W), lambda b, t, c=c: (b, t, col0 // W + c))
    n_groups = W // GROUP_W
    gup = g_up.astype(BF16)
    return pl.pallas_call(
        _rwkv_kernel, grid=(B // nb, S // C),
        in_specs=[cblk(0), cblk(1), cblk(2),
                  pl.BlockSpec((nb, C, lw), lambda b, t: (b, t, (col0 + 3 * W) // lw)),
                  vec(W), vec(W), vec(W), vec(lw),
                  mat(wup_pad), mat(aup_pad), mat(gup),
                  vec(W), vec(W), vec(W), vec(W), vec(W), vec(W), vec(W)],
        out_specs=pl.BlockSpec((nb, C, W), lambda b, t: (b, t, 0)),
        out_shape=jax.ShapeDtypeStruct((B, S, W), BF16),
        scratch_shapes=[pltpu.VMEM((nb, 8, W), F32), pltpu.VMEM((nb, 8, W), F32), pltpu.VMEM((nb, 8, W), F32),
                        pltpu.VMEM((nb, 8, lw), F32),
                        pltpu.VMEM((nb, n_groups, GROUP_W, GROUP_W), F32)],
        compiler_params=_params("parallel", "arbitrary"), name="rwkv7",
    )(p_rwkv, p_rwkv, p_rwkv, p_rwkv,
      row(mu[:W]), row(mu[W:2 * W]), row(mu[2 * W:3 * W]), row(mu[3 * W:]),
      wup_pad, aup_pad, gup,
      row(w0), row(a0), row(k_k), row(k_a), row(r_k), row(ln_g), row(ln_b))


def _merge_kernel(*refs, alpha, with_ln):
    if with_ln:
        yr_ref, yw_ref, x_ref, pg_ref, pb_ref, wg_ref, wr_ref, ww_ref, wo_ref, g_ref, b_ref, o_ref = refs
    else:
        yr_ref, yw_ref, x_ref, wg_ref, wr_ref, ww_ref, wo_ref, g_ref, b_ref, o_ref = refs
    tm, D = x_ref.shape
    half = tm // 2

    def block_input(rs):
        x = x_ref[rs, :]
        return _ln(x, pg_ref[...], pb_ref[...]) if with_ln else x

    def branches(rs):
        return (jnp.dot(yr_ref[rs, :], wr_ref[...], preferred_element_type=F32),
                jnp.dot(yw_ref[rs, :], ww_ref[...], preferred_element_type=F32),
                jnp.dot(block_input(rs).astype(BF16), wg_ref[...], preferred_element_type=F32))

    def mix(rs, up):
        gate = jax.nn.sigmoid(up[2])
        return (gate[:, :D] * up[0] + gate[:, D:] * up[1]).astype(BF16)

    def out_proj(m):
        return jnp.dot(m, wo_ref[...], preferred_element_type=F32)

    def finish(rs, h):
        o_ref[rs, :] = _ln(alpha * block_input(rs) + h, g_ref[...], b_ref[...])

    rows_a, rows_b = slice(0, half), slice(half, tm)
    up_a = branches(rows_a)
    up_b = branches(rows_b)
    h_a = out_proj(mix(rows_a, up_a))
    h_b = out_proj(mix(rows_b, up_b))
    finish(rows_a, h_a)
    finish(rows_b, h_b)


def _merge(y_ret, y_rwkv, x, pre_ln, w_gate, w_ret_up, w_rwkv_up, w_out, ln_g, ln_b, alpha, tm=512):
    T, D = x.shape
    W = y_ret.shape[1]
    full = lambda a: pl.BlockSpec(a.shape, lambda i: (0, 0))
    rowblk = lambda n: pl.BlockSpec((tm, n), lambda i: (i, 0))
    vec = pl.BlockSpec((1, D), lambda i: (0, 0))
    with_ln = pre_ln is not None
    pre = (pre_ln[0].reshape(1, D), pre_ln[1].reshape(1, D)) if with_ln else ()
    return pl.pallas_call(
        functools.partial(_merge_kernel, alpha=alpha, with_ln=with_ln), grid=(T // tm,),
        in_specs=[rowblk(W), rowblk(W), rowblk(D)] + [vec] * len(pre)
        + [full(w_gate), full(w_ret_up), full(w_rwkv_up), full(w_out), vec, vec],
        out_specs=rowblk(D),
        out_shape=jax.ShapeDtypeStruct((T, D), F32),
        compiler_params=_params("parallel"), name="merge_out_ln",
    )(y_ret, y_rwkv, x, *pre, w_gate, w_ret_up, w_rwkv_up, w_out, ln_g.reshape(1, D), ln_b.reshape(1, D))


def _xattn_kernel(x_ref, mem_ref, wkv_ref, wq_ref, wo_ref, g_ref, b_ref, rwt_ref, rb_ref, o_ref, cls_ref,
                  wkvb_ref, kv_ref, *, alpha):
    tm, D = x_ref.shape[1:]
    hd = D // XA_HEADS
    half = tm // XA_ROW_PARTS

    @pl.when(jnp.logical_and(pl.program_id(0) == 0, pl.program_id(1) == 0))
    def _():
        wkvb_ref[...] = wkv_ref[0].astype(BF16)

    @pl.when(pl.program_id(1) == 0)
    def _():
        kv_ref[...] = jnp.dot(mem_ref[0].astype(BF16), wkvb_ref[...],
                              preferred_element_type=F32).astype(BF16)

    def project(rs):
        return jnp.dot(x_ref[0, rs, :].astype(BF16), wq_ref[...], preferred_element_type=F32)

    def scores(q):
        return [_dot_nt(q[:, h * hd:(h + 1) * hd], kv_ref[:, h * hd:(h + 1) * hd]) for h in range(XA_HEADS)]

    def attend(s_heads):
        outs = []
        for h, s in enumerate(s_heads):
            e = jnp.exp(s - jnp.max(s, axis=-1, keepdims=True))
            probs = e * (1.0 / jnp.sum(e, axis=-1, keepdims=True))
            outs.append(jnp.dot(probs.astype(BF16), kv_ref[:, D + h * hd:D + (h + 1) * hd],
                                preferred_element_type=F32))
        return jnp.concatenate(outs, axis=1)

    def out_proj(o):
        return jnp.dot(o.astype(BF16), wo_ref[...], preferred_element_type=F32)

    def finish(rs, h_out):
        x2 = _ln(alpha * x_ref[0, rs, :] + h_out, g_ref[...], b_ref[...])
        comb, cls = _route(x2, rwt_ref[...], rb_ref[...])
        E = comb.shape[0]
        comb_pad = jnp.concatenate([comb, jnp.zeros((LANES - E, half), F32)], axis=0)
        o_ref[0, rs, :D] = x2
        o_ref[0, rs, D:] = comb_pad.T
        cls_ref[0, :, rs] = cls

    rows = [slice(i * half, (i + 1) * half) for i in range(tm // half)]
    s_cur, h_prev = scores(project(rows[0])), None
    for i, rs in enumerate(rows):
        q_next = project(rows[i + 1]) if i + 1 < len(rows) else None
        o = attend(s_cur)
        if q_next is not None:
            s_cur = scores(q_next)
        if h_prev is not None:
            finish(rows[i - 1], h_prev)
        h_prev = out_proj(o)
    finish(rows[-1], h_prev)


def _cross_attention(x, mem, wkv, layer, wq, wo, ln_g, ln_b, router_w, router_bias, alpha, tm=1024):
    B, S, D = x.shape
    M = mem.shape[1]
    E = router_w.shape[1]
    full = lambda a: pl.BlockSpec(a.shape, lambda b, i: (0, 0))
    n_t = S // tm
    return pl.pallas_call(
        functools.partial(_xattn_kernel, alpha=alpha), grid=(B, n_t),
        in_specs=[pl.BlockSpec((1, tm, D), lambda b, i: (b, i, 0)),
                  pl.BlockSpec((1, M, D), lambda b, i: (b, 0, 0)),
                  pl.BlockSpec((1, D, 2 * D), lambda b, i: (layer, 0, 0), pipeline_mode=pl.Buffered(1)),
                  full(wq), full(wo),
                  pl.BlockSpec((1, D), lambda b, i: (0, 0)), pl.BlockSpec((1, D), lambda b, i: (0, 0)),
                  pl.BlockSpec((D, LANES), lambda b, i: (0, 0)), pl.BlockSpec((E, 1), lambda b, i: (0, 0))],
        out_specs=[pl.BlockSpec((1, tm, D + LANES), lambda b, i: (b, i, 0)),
                   pl.BlockSpec((1, 1, tm), lambda b, i: (b * n_t + i, 0, 0))],
        out_shape=[jax.ShapeDtypeStruct((B, S, D + LANES), F32),
                   jax.ShapeDtypeStruct((B * n_t, 1, tm), jnp.int32)],
        scratch_shapes=[pltpu.VMEM((D, 2 * D), BF16), pltpu.VMEM((M, 2 * D), BF16)],
        compiler_params=_params("arbitrary", "arbitrary"), name="xattn_ln_route",
    )(x, mem, wkv, wq, wo, ln_g.reshape(1, D), ln_b.reshape(1, D),
      jnp.pad(router_w, ((0, 0), (0, LANES - E))), router_bias.reshape(E, 1))


def _expert_pairs(per_group):
    todo = [(i, j) for i in range(per_group) for j in range(i + 1, per_group)]
    order = [todo.pop(0)]
    while todo:
        nxt = next((p for p in todo if set(p) & set(order[-1])), todo[0])
        todo.remove(nxt)
        order.append(nxt)
    return order


def _split_bf16(v):
    hi = v.astype(BF16)
    return hi, (v - hi.astype(F32)).astype(BF16)


def _route(x, w, bias):
    E = bias.shape[0]
    per_group = E // N_GROUPS
    x_hi, x_lo = _split_bf16(x)
    w_hi, w_lo = _split_bf16(w)
    logits = (jnp.dot(x_hi, w_hi, preferred_element_type=F32) + jnp.dot(x_hi, w_lo, preferred_element_type=F32)
              + jnp.dot(x_lo, w_hi, preferred_element_type=F32)).T[:E]
    aff = jax.nn.sigmoid(logits)
    choice = aff + bias
    rows = [choice[e:e + 1, :] for e in range(E)]
    scores = []
    for gidx in range(N_GROUPS):
        members = rows[gidx * per_group:(gidx + 1) * per_group]
        best = None
        for i in range(per_group):
            for j in range(i + 1, per_group):
                pair = members[i] + members[j]
                best = pair if best is None else jnp.maximum(best, pair)
        scores.append(best)
    top = scores[0]
    for s in scores[1:]:
        top = jnp.maximum(top, s)
    taken = jnp.zeros_like(top, dtype=jnp.bool_)
    in_best = []
    for s in scores:
        hit = jnp.logical_and(s == top, jnp.logical_not(taken))
        in_best.append(hit)
        taken = jnp.logical_or(taken, hit)
    sel_w = []
    cls = jnp.zeros(top.shape, jnp.int32)
    pairs = _expert_pairs(per_group)
    for gidx in range(N_GROUPS):
        members = rows[gidx * per_group:(gidx + 1) * per_group]
        chosen = []
        for i in range(per_group):
            rank = jnp.zeros_like(top)
            for j in range(per_group):
                if j == i:
                    continue
                ahead = (members[j] > members[i]) if j > i else (members[j] >= members[i])
                rank = rank + jnp.where(ahead, 1.0, 0.0)
            chosen.append(jnp.logical_and(in_best[gidx], rank < TOP_K))
            e = gidx * per_group + i
            sel_w.append(jnp.where(chosen[i], aff[e:e + 1, :], 0.0))
        for p, (i, j) in enumerate(pairs):
            cls = jnp.where(jnp.logical_and(chosen[i], chosen[j]), gidx * len(pairs) + p, cls)
    total = sel_w[0]
    for w in sel_w[1:]:
        total = total + w
    return jnp.concatenate(sel_w, axis=0) / total, cls


def _rank_kernel(cls_ref, rank_ref, cnt_ref, base_ref):
    @pl.when(pl.program_id(0) == 0)
    def _():
        base_ref[...] = jnp.zeros_like(base_ref)

    cls = cls_ref[0]
    tm = cls.shape[-1]
    n_cls = base_ref.shape[0]
    onehot = lax.broadcasted_iota(jnp.int32, (n_cls, tm), 0) == cls
    earlier = (lax.broadcasted_iota(jnp.int32, (tm, tm), 0)
               < lax.broadcasted_iota(jnp.int32, (tm, tm), 1))
    ones = jnp.where(onehot, 1.0, 0.0)
    before = jnp.dot(ones.astype(BF16), jnp.where(earlier, 1.0, 0.0).astype(BF16),
                     preferred_element_type=F32)
    base = base_ref[...]
    rank = jnp.sum(jnp.where(onehot, base + before, 0.0), axis=0, keepdims=True)
    rank_ref[0] = rank.astype(jnp.int32)
    total = base + jnp.sum(ones, axis=1, keepdims=True)
    base_ref[...] = total
    cnt_ref[...] = jnp.broadcast_to(total, cnt_ref.shape)


def _class_ranks(cls, n_cls_pad):
    n_t, _, tm = cls.shape
    return pl.pallas_call(
        _rank_kernel, grid=(n_t,),
        in_specs=[pl.BlockSpec((1, 1, tm), lambda i: (i, 0, 0))],
        out_specs=[pl.BlockSpec((1, 1, tm), lambda i: (i, 0, 0)),
                   pl.BlockSpec((n_cls_pad, LANES), lambda i: (0, 0))],
        out_shape=[jax.ShapeDtypeStruct((n_t, 1, tm), jnp.int32),
                   jax.ShapeDtypeStruct((n_cls_pad, LANES), F32)],
        scratch_shapes=[pltpu.VMEM((n_cls_pad, 1), F32)],
        compiler_params=_params("arbitrary"), name="moe_class_rank",
    )(cls)


def _row_copy(src, src_row, dst, dst_row, sem):
    return pltpu.make_async_copy(src.at[pl.ds(src_row, 1)], dst.at[pl.ds(dst_row, 1)], sem)


def _for_each_row(n_rows, fn):
    def group(g, carry):
        r0 = pl.multiple_of(g * SUBLANES, SUBLANES)
        for j in range(SUBLANES):
            fn(r0 + j)
        return carry

    lax.fori_loop(0, n_rows // SUBLANES, group, 0)


def _dispatch_kernel(dest_ref, x_ref, o_hbm, sem):
    tm = x_ref.shape[0]
    base = pl.program_id(0) * tm
    copy = lambda r: _row_copy(x_ref, r, o_hbm, dest_ref[base + r], sem)
    _for_each_row(tm, lambda r: copy(r).start())
    _for_each_row(tm, lambda r: copy(r).wait())


def _dispatch(xa, dest, tm=2048):
    T, W = xa.shape
    tm = min(tm, T)
    return pl.pallas_call(
        _dispatch_kernel,
        grid_spec=pltpu.PrefetchScalarGridSpec(
            num_scalar_prefetch=1, grid=(T // tm,),
            in_specs=[pl.BlockSpec((tm, W), lambda i, dest: (i, 0))],
            out_specs=pl.BlockSpec(memory_space=pl.ANY),
            scratch_shapes=[pltpu.SemaphoreType.DMA(())]),
        out_shape=jax.ShapeDtypeStruct((T, W), xa.dtype),
        compiler_params=_params("arbitrary"), name="moe_dispatch",
    )(dest, xa)


def _undispatch_kernel(dest_ref, y_hbm, o_ref, sem):
    tm = o_ref.shape[0]
    base = pl.program_id(0) * tm
    copy = lambda r: _row_copy(y_hbm, dest_ref[base + r], o_ref, r, sem)
    _for_each_row(tm, lambda r: copy(r).start())
    _for_each_row(tm, lambda r: copy(r).wait())


def _undispatch(ys, dest, tm=2048):
    T, D = ys.shape
    tm = min(tm, T)
    return pl.pallas_call(
        _undispatch_kernel,
        grid_spec=pltpu.PrefetchScalarGridSpec(
            num_scalar_prefetch=1, grid=(T // tm,),
            in_specs=[pl.BlockSpec(memory_space=pl.ANY)],
            out_specs=pl.BlockSpec((tm, D), lambda i, dest: (i, 0)),
            scratch_shapes=[pltpu.SemaphoreType.DMA(())]),
        out_shape=jax.ShapeDtypeStruct((T, D), ys.dtype),
        compiler_params=_params("arbitrary"), name="moe_undispatch",
    )(dest, ys)


STEP_FIRST, STEP_LAST, STEP_VALID, STEP_NEW_EXPERT = 1, 2, 4, 8


def _moe_kernel(tile_ref, exp_ref, flag_ref, xs_ref, wg_ref, wu_ref, wd_ref, g_ref, b_ref, o_ref,
                xbf_ref, acc_ref, wgb_ref, wub_ref, wdb_ref, *, alpha):
    step = pl.program_id(0)
    flags = flag_ref[step]
    e = exp_ref[step]
    D = o_ref.shape[-1]

    @pl.when((flags & STEP_NEW_EXPERT) != 0)
    def _():
        wgb_ref[...] = wg_ref[0, 0].astype(BF16)
        wub_ref[...] = wu_ref[0, 0].astype(BF16)

    @pl.when((flags & STEP_FIRST) != 0)
    def _():
        xbf_ref[...] = xs_ref[:, :D].astype(BF16)
        acc_ref[...] = jnp.zeros_like(acc_ref)

    @pl.when((flags & STEP_VALID) != 0)
    def _():
        half = xbf_ref.shape[0] // 2

        def up(rs):
            xb = xbf_ref[rs, :]
            return (jnp.dot(xb, wgb_ref[...], preferred_element_type=F32),
                    jnp.dot(xb, wub_ref[...], preferred_element_type=F32))

        def act(hg, hu):
            return (hg * jax.nn.sigmoid(hg) * hu).astype(BF16)

        def down(h):
            return jnp.dot(h, wdb_ref[...], preferred_element_type=F32)

        def accumulate(rs, y):
            comb = xs_ref[rs, D:]
            lane = lax.broadcasted_iota(jnp.int32, comb.shape, 1)
            w_e = jnp.sum(jnp.where(lane == e, comb, 0.0), axis=-1, keepdims=True)
            acc_ref[rs, :] += w_e * y

        rows_a, rows_b = slice(0, half), slice(half, 2 * half)
        up_a = up(rows_a)
        up_b = up(rows_b)
        wdb_ref[...] = wd_ref[0, 0].astype(BF16)
        y_a = down(act(*up_a))
        y_b = down(act(*up_b))
        accumulate(rows_a, y_a)
        accumulate(rows_b, y_b)

    @pl.when((flags & STEP_LAST) != 0)
    def _():
        o_ref[...] = _ln(alpha * xs_ref[:, :D] + acc_ref[...], g_ref[...], b_ref[...])


def _moe_steps(counts, n_experts, n_tiles, tm):
    n_cls = counts.shape[0]
    per_group = n_experts // N_GROUPS
    pairs = _expert_pairs(per_group)
    uses = [[0.0] * n_experts for _ in range(n_cls)]
    for c in range(n_cls):
        gidx, (i, j) = c // len(pairs), pairs[c % len(pairs)]
        uses[c][gidx * per_group + i] = 1.0
        uses[c][gidx * per_group + j] = 1.0
    uses = jnp.array(uses, F32)
    ends = jnp.cumsum(counts)
    starts = ends - counts
    lo = (jnp.arange(n_tiles, dtype=jnp.int32) * tm)[:, None]
    present = jnp.logical_and(starts[None, :] < lo + tm, ends[None, :] > lo)
    need = jnp.dot(present.astype(F32), uses) > 0
    odd = (jnp.arange(n_tiles, dtype=jnp.int32) % 2 == 1)[:, None]
    need = jnp.where(odd, need[:, ::-1], need).reshape(-1)
    n_steps = min(n_tiles * n_experts, TOP_K * (n_tiles + n_cls - 1))
    n_valid = jnp.sum(need.astype(jnp.int32))
    idx = jnp.nonzero(need, size=n_steps, fill_value=0)[0].astype(jnp.int32)
    k = jnp.arange(n_steps, dtype=jnp.int32)
    valid = k < n_valid
    idx = jnp.where(valid, idx, idx[n_valid - 1])
    tile, expert = idx // n_experts, idx % n_experts
    expert = jnp.where(tile % 2 == 1, n_experts - 1 - expert, expert)
    prev_tile = jnp.concatenate([jnp.full((1,), -1, jnp.int32), tile[:-1]])
    next_tile = jnp.concatenate([tile[1:], jnp.full((1,), -1, jnp.int32)])
    first = jnp.logical_and(valid, tile != prev_tile)
    last = jnp.logical_and(valid, jnp.logical_or(tile != next_tile, k == n_valid - 1))
    prev_expert = jnp.concatenate([jnp.full((1,), -1, jnp.int32), expert[:-1]])
    new_expert = jnp.logical_and(valid, expert != prev_expert)
    flags = (first * STEP_FIRST + last * STEP_LAST + valid * STEP_VALID
             + new_expert * STEP_NEW_EXPERT).astype(jnp.int32)
    return tile, expert, flags


def _moe_experts(xs, steps, layer, w_gate, w_up, w_down, ln_g, ln_b, alpha, tm):
    T, W = xs.shape
    D = W - LANES
    F = w_gate.shape[-1]
    tile, expert, flags = steps
    return pl.pallas_call(
        functools.partial(_moe_kernel, alpha=alpha),
        grid_spec=pltpu.PrefetchScalarGridSpec(
            num_scalar_prefetch=3, grid=(tile.shape[0],),
            in_specs=[pl.BlockSpec((tm, W), lambda s, t, e, f: (t[s], 0)),
                      pl.BlockSpec((1, 1, D, F), lambda s, t, e, f: (layer, e[s], 0, 0)),
                      pl.BlockSpec((1, 1, D, F), lambda s, t, e, f: (layer, e[s], 0, 0)),
                      pl.BlockSpec((1, 1, F, D), lambda s, t, e, f: (layer, e[s], 0, 0)),
                      pl.BlockSpec((1, D), lambda s, t, e, f: (0, 0)),
                      pl.BlockSpec((1, D), lambda s, t, e, f: (0, 0))],
            out_specs=pl.BlockSpec((tm, D), lambda s, t, e, f: (t[s], 0)),
            scratch_shapes=[pltpu.VMEM((tm, D), BF16), pltpu.VMEM((tm, D), F32),
                            pltpu.VMEM((D, F), BF16), pltpu.VMEM((D, F), BF16), pltpu.VMEM((F, D), BF16)]),
        out_shape=jax.ShapeDtypeStruct((T, D), F32),
        compiler_params=_params("arbitrary"), name="moe_experts_ln",
    )(tile, expert, flags, xs, w_gate, w_up, w_down, ln_g.reshape(1, D), ln_b.reshape(1, D))


def _moe(xa, cls, layer, w_gate, w_up, w_down, ln_g, ln_b, alpha, tm=512):
    T = xa.shape[0]
    E = w_gate.shape[1]
    n_cls = N_GROUPS * len(_expert_pairs(E // N_GROUPS))
    rank, cnt = _class_ranks(cls, -(-n_cls // 8) * 8)
    counts = cnt[:n_cls, 0].astype(jnp.int32)
    starts = jnp.cumsum(counts) - counts
    dest = (starts[cls.reshape(T)] + rank.reshape(T)).astype(jnp.int32)
    xs = _dispatch(xa, dest)
    ys = _moe_experts(xs, _moe_steps(counts, E, T // tm, tm), layer, w_gate, w_up, w_down, ln_g, ln_b, alpha, tm)
    return _undispatch(ys, dest)


def _rotary_tables(S):
    half = RET_HD // 2
    inv_freq = ROPE_BASE ** (-jnp.arange(half, dtype=F32) / half)
    ang = jnp.arange(S).astype(F32)[:, None] * inv_freq[None, :]
    cos, sin = jnp.cos(ang), jnp.sin(ang)
    return jnp.concatenate([cos, cos], axis=1), jnp.concatenate([-sin, sin], axis=1)


def kernel(x, mem, ln_in_g, ln_in_b, router_w, router_bias, w_in, ret_gn_g, rwkv_mu, rwkv_w_up, rwkv_w0,
           rwkv_a_up, rwkv_a0, rwkv_g_up, rwkv_k_k, rwkv_k_a, rwkv_r_k, rwkv_ln_g, rwkv_ln_b,
           w_ret_up, w_rwkv_up, w_out, ln1_g, ln1_b, xa_wq, xa_wkv, xa_wo, ln2_g, ln2_b,
           moe_w_gate, moe_w_up, moe_w_down, ln3_g, ln3_b):
    B, S, D = x.shape
    T = B * S
    depth = w_in.shape[0]
    alpha = (2 * depth) ** 0.25
    ret_w = ret_gn_g.shape[-1]
    rwkv_w = rwkv_w0.shape[-1]
    ret_cols = 4 * ret_w
    rwkv_cols = rwkv_mu.shape[-1]
    n_proj = -(-(ret_cols + rwkv_cols) // IN_PROJ_TN) * IN_PROJ_TN
    cos2, sin2 = _rotary_tables(S)

    xs = x.reshape(T, D)
    for l in range(depth):
        w_l = w_in[l]
        entry_ln = (ln_in_g, ln_in_b) if l == 0 else None
        p3 = _in_proj(xs, w_in, l, n_proj, ln=entry_ln).reshape(B, S, n_proj)
        y_ret = _retention(p3, cos2, sin2, ret_gn_g[l])
        y_rwkv = _rwkv(p3, ret_cols, rwkv_mu[l], rwkv_w_up[l], rwkv_w0[l], rwkv_a_up[l],
                       rwkv_a0[l], rwkv_g_up[l], rwkv_k_k[l], rwkv_k_a[l], rwkv_r_k[l].reshape(-1),
                       rwkv_ln_g[l], rwkv_ln_b[l])
        xs = _merge(y_ret.reshape(T, ret_w), y_rwkv.reshape(T, rwkv_w), xs, entry_ln,
                    w_l[:, ret_cols + rwkv_cols:].astype(BF16),
                    w_ret_up[l].astype(BF16), w_rwkv_up[l].astype(BF16), w_out[l].astype(BF16),
                    ln1_g[l], ln1_b[l], alpha)
        wq_scaled = (xa_wq[l] * ((D // XA_HEADS) ** -0.5)).astype(BF16)
        xa, cls = _cross_attention(xs.reshape(B, S, D), mem, xa_wkv, l, wq_scaled,
                                   xa_wo[l].astype(BF16), ln2_g[l], ln2_b[l], router_w, router_bias, alpha)
        xs = _moe(xa.reshape(T, D + LANES), cls, l, moe_w_gate, moe_w_up, moe_w_down, ln3_g[l], ln3_b[l], alpha)
    return xs.reshape(B, S, D)
```

```python
import functools
import math

import jax
import jax.numpy as jnp
from jax import lax
from jax.experimental import pallas as pl
from jax.experimental.pallas import tpu as pltpu

F32 = jnp.float32
BF16 = jnp.bfloat16

LANES = 128
SUBLANES = 8
VMEM_LIMIT = 56 * 1024 * 1024

CHUNK = 64
RET_HEADS = 4
RET_HD = 128
ROPE_BASE = 10000.0
RWKV_HD = 64
HD_SHIFT = 6
RWKV_GN_EPS = 64e-5
XA_HEADS = 4
XA_ROW_PARTS = 4
N_GROUPS = 4
TOP_K = 2
LN_EPS = 1e-5
IN_PROJ_TN = 2048
IN_PROJ_ROW_PARTS = 2
HEADS_PER_GROUP = 4
GROUP_W = HEADS_PER_GROUP * RWKV_HD


def _params(*sem):
    return pltpu.CompilerParams(dimension_semantics=sem, vmem_limit_bytes=VMEM_LIMIT)


def _ln(v, g, b, eps=LN_EPS):
    mu = jnp.mean(v, axis=-1, keepdims=True)
    d = v - mu
    var = jnp.mean(d * d, axis=-1, keepdims=True)
    return d * lax.rsqrt(var + eps) * g + b


def _dot(a, b):
    return jnp.dot(a.astype(BF16), b.astype(BF16), preferred_element_type=F32)


def _dot_nt(a, b):
    return lax.dot_general(a.astype(BF16), b.astype(BF16), (((1,), (1,)), ((), ())),
                           preferred_element_type=F32)


def _dot_tn(a, b):
    return lax.dot_general(a.astype(BF16), b.astype(BF16), (((0,), (0,)), ((), ())),
                           preferred_element_type=F32)


def _in_proj_kernel(*refs, with_ln):
    if with_ln:
        a_ref, g_ref, b_ref, w_ref, o_ref, wbf_ref = refs
    else:
        a_ref, w_ref, o_ref, wbf_ref = refs

    @pl.when(pl.program_id(1) == 0)
    def _():
        wbf_ref[...] = w_ref[0].astype(BF16)

    tm = a_ref.shape[0]
    part = tm // IN_PROJ_ROW_PARTS

    def prepared(rs):
        a = a_ref[rs, :]
        if with_ln:
            a = _ln(a, g_ref[...], b_ref[...])
        return a.astype(BF16)

    rows = [slice(i * part, (i + 1) * part) for i in range(IN_PROJ_ROW_PARTS)]
    cur = prepared(rows[0])
    for i, rs in enumerate(rows):
        nxt = prepared(rows[i + 1]) if i + 1 < len(rows) else None
        o_ref[rs, :] = jnp.dot(cur, wbf_ref[...], preferred_element_type=F32).astype(o_ref.dtype)
        cur = nxt


def _in_proj(a, w_in, layer, n_cols, ln=None, tm=1024):
    M, K = a.shape
    tn = IN_PROJ_TN
    assert n_cols % tn == 0 and n_cols <= w_in.shape[-1] and M % tm == 0
    with_ln = ln is not None
    vec = pl.BlockSpec((1, K), lambda j, i: (0, 0))
    in_specs = [pl.BlockSpec((tm, K), lambda j, i: (i, 0))]
    in_specs += [vec, vec] if with_ln else []
    in_specs += [pl.BlockSpec((1, K, tn), lambda j, i: (layer, 0, j), pipeline_mode=pl.Buffered(1))]
    args = (a,) + ((ln[0].reshape(1, K), ln[1].reshape(1, K)) if with_ln else ()) + (w_in,)
    return pl.pallas_call(
        functools.partial(_in_proj_kernel, with_ln=with_ln), grid=(n_cols // tn, M // tm),
        in_specs=in_specs, out_specs=pl.BlockSpec((tm, tn), lambda j, i: (i, j)),
        out_shape=jax.ShapeDtypeStruct((M, n_cols), BF16),
        scratch_shapes=[pltpu.VMEM((K, tn), BF16)],
        compiler_params=_params("arbitrary", "arbitrary"), name="ln_in_proj" if with_ln else "in_proj",
    )(*args)


def _ret_kernel(q_ref, k_ref, v_ref, g_ref, cos_ref, sin_ref, gn_ref, o_ref, state_ref, *, n_chunks):
    @pl.when(pl.program_id(1) == 0)
    def _():
        state_ref[...] = jnp.zeros_like(state_ref)

    C, hd = CHUNK, RET_HD
    ti = lax.broadcasted_iota(jnp.int32, (C, C), 0)
    si = lax.broadcasted_iota(jnp.int32, (C, C), 1)
    dist = jnp.abs(ti - si).astype(F32)
    cpos = lax.broadcasted_iota(jnp.int32, (C, 1), 0).astype(F32)
    log_gamma = [math.log(1.0 - 2.0 ** (-5.0 - h)) for h in range(RET_HEADS)]
    inner_mask = [jnp.exp(lg * dist) for lg in log_gamma]
    k_decay = [jnp.exp(lg * (C - 1.0 - cpos)) for lg in log_gamma]
    q_decay = [jnp.exp(lg * (cpos + 1.0)) for lg in log_gamma]
    pairs = [(c, h) for c in range(n_chunks) for h in range(RET_HEADS)]
    rows = lambda c: slice(c * C, (c + 1) * C)
    cols = lambda h: slice(h * hd, (h + 1) * hd)

    def rotated(ref, c, h):
        t = ref[0, rows(c), cols(h)].astype(F32)
        return t * cos_ref[rows(c), :] + pltpu.roll(t, hd // 2, axis=1) * sin_ref[rows(c), :]

    qr = [rotated(q_ref, c, h) for c, h in pairs]
    kr = [rotated(k_ref, c, h) * (hd ** -0.5) for c, h in pairs]
    v = [v_ref[0, rows(c), cols(h)] for c, h in pairs]
    scores = [_dot_nt(a, b) * inner_mask[h] for a, b, (c, h) in zip(qr, kr, pairs)]
    inner = [_dot(s, t) for s, t in zip(scores, v)]
    kv = [_dot_tn(b * k_decay[h], t) for b, t, (c, h) in zip(kr, v, pairs)]
    states = []
    for h in range(RET_HEADS):
        state = state_ref[h]
        for c in range(n_chunks):
            states.append(((c, h), state))
            state = math.exp(log_gamma[h] * C) * state + kv[pairs.index((c, h))]
        state_ref[h] = state
    entering = dict(states)
    cross = [_dot(a * q_decay[h], entering[(c, h)]) for a, (c, h) in zip(qr, pairs)]
    for y_in, y_cr, (c, h) in zip(inner, cross, pairs):
        y = y_in + y_cr
        mu = jnp.mean(y, axis=-1, keepdims=True)
        d = y - mu
        var = jnp.mean(d * d, axis=-1, keepdims=True)
        yn = d * lax.rsqrt(var + LN_EPS) * gn_ref[:, cols(h)]
        g = g_ref[0, rows(c), cols(h)].astype(F32)
        o_ref[0, rows(c), cols(h)] = (g * jax.nn.sigmoid(g) * yn).astype(o_ref.dtype)


def _retention(p_ret, cos2, sin2, gn_g, ts=256):
    B, S, _ = p_ret.shape
    W = gn_g.shape[-1]
    blk = lambda c: pl.BlockSpec((1, ts, W), lambda b, n, c=c: (b, n, c))
    return pl.pallas_call(
        functools.partial(_ret_kernel, n_chunks=ts // CHUNK), grid=(B, S // ts),
        in_specs=[blk(0), blk(1), blk(2), blk(3),
                  pl.BlockSpec((ts, RET_HD), lambda b, n: (n, 0)),
                  pl.BlockSpec((ts, RET_HD), lambda b, n: (n, 0)),
                  pl.BlockSpec((1, W), lambda b, n: (0, 0))],
        out_specs=pl.BlockSpec((1, ts, W), lambda b, n: (b, n, 0)),
        out_shape=jax.ShapeDtypeStruct((B, S, W), BF16),
        scratch_shapes=[pltpu.VMEM((RET_HEADS, RET_HD, RET_HD), F32)],
        compiler_params=_params("parallel", "arbitrary"), name="retention",
    )(p_ret, p_ret, p_ret, p_ret, cos2, sin2, gn_g.reshape(1, W))


def _head_sum(x):
    R, W = x.shape
    lo = lax.broadcasted_iota(jnp.int32, (R, LANES), 1) < RWKV_HD
    outs = []
    for j in range(W // LANES):
        xs = x[:, j * LANES:(j + 1) * LANES]
        s_lo = jnp.sum(jnp.where(lo, xs, 0.0), axis=-1, keepdims=True)
        s_hi = jnp.sum(jnp.where(lo, 0.0, xs), axis=-1, keepdims=True)
        outs.append(jnp.where(lo, s_lo, s_hi))
    return jnp.concatenate(outs, axis=1)


def _block_diag(x, head_of_lane):
    parts = [jnp.where(head_of_lane == h, x, 0.0) for h in range(HEADS_PER_GROUP)]
    return jnp.concatenate(parts, axis=0).astype(BF16)


def _packed_mm(xs, ys, head_of_lane):
    bds = [_block_diag(y, head_of_lane) for y in ys]
    return [jnp.dot(x.astype(BF16), bd, preferred_element_type=F32) for x, bd in zip(xs, bds)]


def _unit_lower_inverse(a_list, t_idx, s_idx, head_of_lane):
    mm = functools.partial(_packed_mm, head_of_lane=head_of_lane)
    add = lambda xs, ys: [x + y for x, y in zip(xs, ys)]
    lower = s_idx < t_idx
    same4 = (t_idx >> 2) == (s_idx >> 2)
    same16 = (t_idx >> 4) == (s_idx >> 4)
    m1 = lower & same4
    m2 = lower & same16 & jnp.logical_not(same4)
    m3 = lower & jnp.logical_not(same16)
    eye = jnp.where(t_idx == s_idx, 1.0, 0.0)
    a1 = [jnp.where(m1, a, 0.0) for a in a_list]
    a1sq = mm(a1, a1)
    a1cu = mm(a1, a1sq)
    inv = [eye + p + q + r for p, q, r in zip(a1, a1sq, a1cu)]
    for m in (m2, m3):
        n = mm(inv, [jnp.where(m, a, 0.0) for a in a_list])
        x = add(inv, mm(mm(n, n), inv))
        inv = add(x, mm(n, x))
    return inv


def _rwkv_kernel(*refs):
    state_ref = refs[-1]

    @pl.when(pl.program_id(1) == 0)
    def _():
        for ref in refs[-5:]:
            ref[...] = jnp.zeros_like(ref)

    seqs = [_rwkv_prepare(bi, *refs) for bi in range(state_ref.shape[0])]
    _rwkv_chains(seqs, state_ref)
    for bi, seq in enumerate(seqs):
        _rwkv_finish(bi, seq, *refs)


def _rwkv_prepare(bi, pr_ref, pk_ref, pv_ref, pl_ref, mur_ref, muk_ref, muv_ref, mul_ref,
                  wup_ref, aup_ref, gup_ref, w0_ref, a0_ref, kk_ref, ka_ref, rk_ref, lng_ref, lnb_ref,
                  o_ref, carry_r, carry_k, carry_v, carry_l, state_ref):
    C = CHUNK

    def shifted(p_ref, mu_ref, carry):
        p = p_ref[bi].astype(F32)
        row = lax.broadcasted_iota(jnp.int32, p.shape, 0)
        prev = jnp.where(row == 0, carry[bi, 0:1, :], pltpu.roll(p, 1, axis=0))
        carry[bi, 0:1, :] = p[C - 1:C, :]
        return p + mu_ref[...] * (prev - p)

    r = shifted(pr_ref, mur_ref, carry_r)
    k = shifted(pk_ref, muk_ref, carry_k)
    v = shifted(pv_ref, muv_ref, carry_v)
    lo = shifted(pl_ref, mul_ref, carry_l)
    dwa = lo[:, :LANES]
    dg = lo[:, LANES:]

    z = w0_ref[...] + _dot(jnp.tanh(dwa), wup_ref[...])
    logw = -math.exp(-0.5) * jax.nn.sigmoid(z)
    a = jax.nn.sigmoid(a0_ref[...] + _dot(dwa, aup_ref[...]))
    g = _dot(jax.nn.sigmoid(dg), gup_ref[...])

    kk = k * kk_ref[...]
    kk = kk * jnp.minimum(lax.rsqrt(_head_sum(kk * kk)), 1e12)
    k = k * (1.0 + (a - 1.0) * ka_ref[...])

    ti = lax.broadcasted_iota(jnp.int32, (C, C), 0)
    si = lax.broadcasted_iota(jnp.int32, (C, C), 1)
    tril = jnp.where(si <= ti, 1.0, 0.0).astype(BF16)
    lw_hi = logw.astype(BF16)
    lw_lo = (logw - lw_hi.astype(F32)).astype(BF16)
    cum = (jnp.dot(tril, lw_hi, preferred_element_type=F32)
           + jnp.dot(tril, lw_lo, preferred_element_type=F32))
    cum_last = cum[C - 1:C, :]
    e_neg = jnp.exp(-cum)
    abar = -kk * jnp.exp(cum - logw)
    rbar = r * jnp.exp(cum)
    kka = kk * a
    btil = kka * e_neg
    ktil = k * e_neg
    gamma_end = jnp.exp(cum_last)
    e_end = gamma_end * e_neg
    bend = kka * e_end
    kend = k * e_end
    return dict(r=r, k=k, v=v, g=g, abar=abar, rbar=rbar, btil=btil, ktil=ktil, bend=bend, kend=kend,
                gamma_end=gamma_end)


def _rwkv_chains(seqs, state_ref):
    C = CHUNK
    n_groups = state_ref.shape[1]
    t_idx = lax.broadcasted_iota(jnp.int32, (C, GROUP_W), 0)
    lane = lax.broadcasted_iota(jnp.int32, (C, GROUP_W), 1)
    s_idx = lane & (RWKV_HD - 1)
    head_of_lane = lane >> HD_SHIFT
    strict = s_idx < t_idx
    incl = s_idx <= t_idx
    bd_r = lax.broadcasted_iota(jnp.int32, (GROUP_W, GROUP_W), 0) >> HD_SHIFT
    bd_c = lax.broadcasted_iota(jnp.int32, (GROUP_W, GROUP_W), 1) >> HD_SHIFT
    on_diag = bd_r == bd_c

    chains = [(bi, gi) for bi in range(len(seqs)) for gi in range(n_groups)]
    part = lambda name: [seqs[bi][name][:, gi * GROUP_W:(gi + 1) * GROUP_W] for bi, gi in chains]
    mm = functools.partial(_packed_mm, head_of_lane=head_of_lane)
    v_g = part("v")
    lhs = [jnp.concatenate([p, q], axis=0) for p, q in zip(part("abar"), part("rbar"))]
    rhs = [jnp.concatenate([_block_diag(p, head_of_lane), _block_diag(q, head_of_lane)], axis=0)
           for p, q in zip(part("btil"), part("ktil"))]
    gram = [_dot_nt(p, q) for p, q in zip(lhs, rhs)]
    a_ab = [jnp.where(strict, gm[:C, :GROUP_W], 0.0) for gm in gram]
    a_ak = [jnp.where(strict, gm[:C, GROUP_W:], 0.0) for gm in gram]
    m_rb = [jnp.where(incl, gm[C:, :GROUP_W], 0.0) for gm in gram]
    m_rk = [jnp.where(incl, gm[C:, GROUP_W:], 0.0) for gm in gram]
    inv = _unit_lower_inverse(a_ab, t_idx, s_idx, head_of_lane)
    st = [state_ref[bi, gi] for bi, gi in chains]
    from_state = [_dot_nt(p, q) for p, q in zip(lhs, st)]
    local = mm([jnp.concatenate([p, q], axis=0) for p, q in zip(a_ak, m_rk)], v_g)
    u = mm(inv, [fs[:C] + lc[:C] for fs, lc in zip(from_state, local)])
    via_u = mm(m_rb, u)
    upd = [_dot_tn(jnp.concatenate([p, q], axis=0), jnp.concatenate([b, k], axis=0))
           for p, q, b, k in zip(u, v_g, part("bend"), part("kend"))]
    for (bi, gi), s_old, up, ge in zip(chains, st, upd, part("gamma_end")):
        state_ref[bi, gi] = s_old * ge + jnp.where(on_diag, up, 0.0)
    ys = [fs[C:] + p + lc[C:] for fs, p, lc in zip(from_state, via_u, local)]
    for bi, seq in enumerate(seqs):
        seq["y"] = jnp.concatenate(ys[bi * n_groups:(bi + 1) * n_groups], axis=1)


def _rwkv_finish(bi, seq, pr_ref, pk_ref, pv_ref, pl_ref, mur_ref, muk_ref, muv_ref, mul_ref,
                 wup_ref, aup_ref, gup_ref, w0_ref, a0_ref, kk_ref, ka_ref, rk_ref, lng_ref, lnb_ref,
                 o_ref, carry_r, carry_k, carry_v, carry_l, state_ref):
    y, r, k, v, g = seq["y"], seq["r"], seq["k"], seq["v"], seq["g"]
    inv_hd = 1.0 / RWKV_HD
    mu = _head_sum(y) * inv_hd
    d = y - mu
    var = _head_sum(d * d) * inv_hd
    yn = d * lax.rsqrt(var + RWKV_GN_EPS) * lng_ref[...] + lnb_ref[...]
    bonus = _head_sum(r * k * rk_ref[...]) * v
    o_ref[bi] = ((yn + bonus) * g).astype(o_ref.dtype)


def _rwkv(p_rwkv, col0, mu, w_up, w0, a_up, a0, g_up, k_k, k_a, r_k, ln_g, ln_b, nb=4):
    B, S, _ = p_rwkv.shape
    W = w0.shape[-1]
    C = CHUNK
    wup_pad = jnp.concatenate([w_up, jnp.zeros_like(a_up)], axis=0).astype(BF16)
    aup_pad = jnp.concatenate([jnp.zeros_like(w_up), a_up], axis=0).astype(BF16)
    row = lambda t: t.reshape(1, -1)
    vec = lambda n: pl.BlockSpec((1, n), lambda b, t: (0, 0))
    mat = lambda a: pl.BlockSpec(a.shape, lambda b, t: (0, 0))
    lw = mu.shape[-1] - 3 * W
    assert col0 % W == 0 and (col0 + 3 * W) % lw == 0 and B % nb == 0 and S % C == 0
    cblk = lambda c: pl.BlockSpec((nb, C, W), lambda b, t, c=c: (b, t, col0 // W + c))
    n_groups = W // GROUP_W
    gup = g_up.astype(BF16)
    return pl.pallas_call(
        _rwkv_kernel, grid=(B // nb, S // C),
        in_specs=[cblk(0), cblk(1), cblk(2),
                  pl.BlockSpec((nb, C, lw), lambda b, t: (b, t, (col0 + 3 * W) // lw)),
                  vec(W), vec(W), vec(W), vec(lw),
                  mat(wup_pad), mat(aup_pad), mat(gup),
                  vec(W), vec(W), vec(W), vec(W), vec(W), vec(W), vec(W)],
        out_specs=pl.BlockSpec((nb, C, W), lambda b, t: (b, t, 0)),
        out_shape=jax.ShapeDtypeStruct((B, S, W), BF16),
        scratch_shapes=[pltpu.VMEM((nb, 8, W), F32), pltpu.VMEM((nb, 8, W), F32), pltpu.VMEM((nb, 8, W), F32),
                        pltpu.VMEM((nb, 8, lw), F32),
                        pltpu.VMEM((nb, n_groups, GROUP_W, GROUP_W), F32)],
        compiler_params=_params("parallel", "arbitrary"), name="rwkv7",
    )(p_rwkv, p_rwkv, p_rwkv, p_rwkv,
      row(mu[:W]), row(mu[W:2 * W]), row(mu[2 * W:3 * W]), row(mu[3 * W:]),
      wup_pad, aup_pad, gup,
      row(w0), row(a0), row(k_k), row(k_a), row(r_k), row(ln_g), row(ln_b))


def _merge_kernel(*refs, alpha, with_ln):
    if with_ln:
        yr_ref, yw_ref, x_ref, pg_ref, pb_ref, wg_ref, wr_ref, ww_ref, wo_ref, g_ref, b_ref, o_ref = refs
    else:
        yr_ref, yw_ref, x_ref, wg_ref, wr_ref, ww_ref, wo_ref, g_ref, b_ref, o_ref = refs
    tm, D = x_ref.shape
    half = tm // 2

    def block_input(rs):
        x = x_ref[rs, :]
        return _ln(x, pg_ref[...], pb_ref[...]) if with_ln else x

    def branches(rs):
        return (jnp.dot(yr_ref[rs, :], wr_ref[...], preferred_element_type=F32),
                jnp.dot(yw_ref[rs, :], ww_ref[...], preferred_element_type=F32),
                jnp.dot(block_input(rs).astype(BF16), wg_ref[...], preferred_element_type=F32))

    def mix(rs, up):
        gate = jax.nn.sigmoid(up[2])
        return (gate[:, :D] * up[0] + gate[:, D:] * up[1]).astype(BF16)

    def out_proj(m):
        return jnp.dot(m, wo_ref[...], preferred_element_type=F32)

    def finish(rs, h):
        o_ref[rs, :] = _ln(alpha * block_input(rs) + h, g_ref[...], b_ref[...])

    rows_a, rows_b = slice(0, half), slice(half, tm)
    up_a = branches(rows_a)
    up_b = branches(rows_b)
    h_a = out_proj(mix(rows_a, up_a))
    h_b = out_proj(mix(rows_b, up_b))
    finish(rows_a, h_a)
    finish(rows_b, h_b)


def _merge(y_ret, y_rwkv, x, pre_ln, w_gate, w_ret_up, w_rwkv_up, w_out, ln_g, ln_b, alpha, tm=512):
    T, D = x.shape
    W = y_ret.shape[1]
    full = lambda a: pl.BlockSpec(a.shape, lambda i: (0, 0))
    rowblk = lambda n: pl.BlockSpec((tm, n), lambda i: (i, 0))
    vec = pl.BlockSpec((1, D), lambda i: (0, 0))
    with_ln = pre_ln is not None
    pre = (pre_ln[0].reshape(1, D), pre_ln[1].reshape(1, D)) if with_ln else ()
    return pl.pallas_call(
        functools.partial(_merge_kernel, alpha=alpha, with_ln=with_ln), grid=(T // tm,),
        in_specs=[rowblk(W), rowblk(W), rowblk(D)] + [vec] * len(pre)
        + [full(w_gate), full(w_ret_up), full(w_rwkv_up), full(w_out), vec, vec],
        out_specs=rowblk(D),
        out_shape=jax.ShapeDtypeStruct((T, D), F32),
        compiler_params=_params("parallel"), name="merge_out_ln",
    )(y_ret, y_rwkv, x, *pre, w_gate, w_ret_up, w_rwkv_up, w_out, ln_g.reshape(1, D), ln_b.reshape(1, D))


def _xattn_kernel(x_ref, mem_ref, wkv_ref, wq_ref, wo_ref, g_ref, b_ref, rwt_ref, rb_ref, o_ref, cls_ref,
                  wkvb_ref, kv_ref, *, alpha):
    tm, D = x_ref.shape[1:]
    hd = D // XA_HEADS
    half = tm // XA_ROW_PARTS

    @pl.when(jnp.logical_and(pl.program_id(0) == 0, pl.program_id(1) == 0))
    def _():
        wkvb_ref[...] = wkv_ref[0].astype(BF16)

    @pl.when(pl.program_id(1) == 0)
    def _():
        kv_ref[...] = jnp.dot(mem_ref[0].astype(BF16), wkvb_ref[...],
                              preferred_element_type=F32).astype(BF16)

    def project(rs):
        return jnp.dot(x_ref[0, rs, :].astype(BF16), wq_ref[...], preferred_element_type=F32)

    def scores(q):
        return [_dot_nt(q[:, h * hd:(h + 1) * hd], kv_ref[:, h * hd:(h + 1) * hd]) for h in range(XA_HEADS)]

    def attend(s_heads):
        outs = []
        for h, s in enumerate(s_heads):
            e = jnp.exp(s - jnp.max(s, axis=-1, keepdims=True))
            probs = e * (1.0 / jnp.sum(e, axis=-1, keepdims=True))
            outs.append(jnp.dot(probs.astype(BF16), kv_ref[:, D + h * hd:D + (h + 1) * hd],
                                preferred_element_type=F32))
        return jnp.concatenate(outs, axis=1)

    def out_proj(o):
        return jnp.dot(o.astype(BF16), wo_ref[...], preferred_element_type=F32)

    def finish(rs, h_out):
        x2 = _ln(alpha * x_ref[0, rs, :] + h_out, g_ref[...], b_ref[...])
        comb, cls = _route(x2, rwt_ref[...], rb_ref[...])
        E = comb.shape[0]
        comb_pad = jnp.concatenate([comb, jnp.zeros((LANES - E, half), F32)], axis=0)
        o_ref[0, rs, :D] = x2
        o_ref[0, rs, D:] = comb_pad.T
        cls_ref[0, :, rs] = cls

    rows = [slice(i * half, (i + 1) * half) for i in range(tm // half)]
    s_cur, h_prev = scores(project(rows[0])), None
    for i, rs in enumerate(rows):
        q_next = project(rows[i + 1]) if i + 1 < len(rows) else None
        o = attend(s_cur)
        if q_next is not None:
            s_cur = scores(q_next)
        if h_prev is not None:
            finish(rows[i - 1], h_prev)
        h_prev = out_proj(o)
    finish(rows[-1], h_prev)


def _cross_attention(x, mem, wkv, layer, wq, wo, ln_g, ln_b, router_w, router_bias, alpha, tm=1024):
    B, S, D = x.shape
    M = mem.shape[1]
    E = router_w.shape[1]
    full = lambda a: pl.BlockSpec(a.shape, lambda b, i: (0, 0))
    n_t = S // tm
    return pl.pallas_call(
        functools.partial(_xattn_kernel, alpha=alpha), grid=(B, n_t),
        in_specs=[pl.BlockSpec((1, tm, D), lambda b, i: (b, i, 0)),
                  pl.BlockSpec((1, M, D), lambda b, i: (b, 0, 0)),
                  pl.BlockSpec((1, D, 2 * D), lambda b, i: (layer, 0, 0), pipeline_mode=pl.Buffered(1)),
                  full(wq), full(wo),
                  pl.BlockSpec((1, D), lambda b, i: (0, 0)), pl.BlockSpec((1, D), lambda b, i: (0, 0)),
                  pl.BlockSpec((D, LANES), lambda b, i: (0, 0)), pl.BlockSpec((E, 1), lambda b, i: (0, 0))],
        out_specs=[pl.BlockSpec((1, tm, D + LANES), lambda b, i: (b, i, 0)),
                   pl.BlockSpec((1, 1, tm), lambda b, i: (b * n_t + i, 0, 0))],
        out_shape=[jax.ShapeDtypeStruct((B, S, D + LANES), F32),
                   jax.ShapeDtypeStruct((B * n_t, 1, tm), jnp.int32)],
        scratch_shapes=[pltpu.VMEM((D, 2 * D), BF16), pltpu.VMEM((M, 2 * D), BF16)],
        compiler_params=_params("arbitrary", "arbitrary"), name="xattn_ln_route",
    )(x, mem, wkv, wq, wo, ln_g.reshape(1, D), ln_b.reshape(1, D),
      jnp.pad(router_w, ((0, 0), (0, LANES - E))), router_bias.reshape(E, 1))


def _expert_pairs(per_group):
    todo = [(i, j) for i in range(per_group) for j in range(i + 1, per_group)]
    order = [todo.pop(0)]
    while todo:
        nxt = next((p for p in todo if set(p) & set(order[-1])), todo[0])
        todo.remove(nxt)
        order.append(nxt)
    return order


def _split_bf16(v):
    hi = v.astype(BF16)
    return hi, (v - hi.astype(F32)).astype(BF16)


def _route(x, w, bias):
    E = bias.shape[0]
    per_group = E // N_GROUPS
    x_hi, x_lo = _split_bf16(x)
    w_hi, w_lo = _split_bf16(w)
    logits = (jnp.dot(x_hi, w_hi, preferred_element_type=F32) + jnp.dot(x_hi, w_lo, preferred_element_type=F32)
              + jnp.dot(x_lo, w_hi, preferred_element_type=F32)).T[:E]
    aff = jax.nn.sigmoid(logits)
    choice = aff + bias
    rows = [choice[e:e + 1, :] for e in range(E)]
    scores = []
    for gidx in range(N_GROUPS):
        members = rows[gidx * per_group:(gidx + 1) * per_group]
        best = None
        for i in range(per_group):
            for j in range(i + 1, per_group):
                pair = members[i] + members[j]
                best = pair if best is None else jnp.maximum(best, pair)
        scores.append(best)
    top = scores[0]
    for s in scores[1:]:
        top = jnp.maximum(top, s)
    taken = jnp.zeros_like(top, dtype=jnp.bool_)
    in_best = []
    for s in scores:
        hit = jnp.logical_and(s == top, jnp.logical_not(taken))
        in_best.append(hit)
        taken = jnp.logical_or(taken, hit)
    sel_w = []
    cls = jnp.zeros(top.shape, jnp.int32)
    pairs = _expert_pairs(per_group)
    for gidx in range(N_GROUPS):
        members = rows[gidx * per_group:(gidx + 1) * per_group]
        chosen = []
        for i in range(per_group):
            rank = jnp.zeros_like(top)
            for j in range(per_group):
                if j == i:
                    continue
                ahead = (members[j] > members[i]) if j > i else (members[j] >= members[i])
                rank = rank + jnp.where(ahead, 1.0, 0.0)
            chosen.append(jnp.logical_and(in_best[gidx], rank < TOP_K))
            e = gidx * per_group + i
            sel_w.append(jnp.where(chosen[i], aff[e:e + 1, :], 0.0))
        for p, (i, j) in enumerate(pairs):
            cls = jnp.where(jnp.logical_and(chosen[i], chosen[j]), gidx * len(pairs) + p, cls)
    total = sel_w[0]
    for w in sel_w[1:]:
        total = total + w
    return jnp.concatenate(sel_w, axis=0) / total, cls


def _rank_kernel(cls_ref, rank_ref, cnt_ref, base_ref):
    @pl.when(pl.program_id(0) == 0)
    def _():
        base_ref[...] = jnp.zeros_like(base_ref)

    cls = cls_ref[0]
    tm = cls.shape[-1]
    n_cls = base_ref.shape[0]
    onehot = lax.broadcasted_iota(jnp.int32, (n_cls, tm), 0) == cls
    earlier = (lax.broadcasted_iota(jnp.int32, (tm, tm), 0)
               < lax.broadcasted_iota(jnp.int32, (tm, tm), 1))
    ones = jnp.where(onehot, 1.0, 0.0)
    before = jnp.dot(ones.astype(BF16), jnp.where(earlier, 1.0, 0.0).astype(BF16),
                     preferred_element_type=F32)
    base = base_ref[...]
    rank = jnp.sum(jnp.where(onehot, base + before, 0.0), axis=0, keepdims=True)
    rank_ref[0] = rank.astype(jnp.int32)
    total = base + jnp.sum(ones, axis=1, keepdims=True)
    base_ref[...] = total
    cnt_ref[...] = jnp.broadcast_to(total, cnt_ref.shape)


def _class_ranks(cls, n_cls_pad):
    n_t, _, tm = cls.shape
    return pl.pallas_call(
        _rank_kernel, grid=(n_t,),
        in_specs=[pl.BlockSpec((1, 1, tm), lambda i: (i, 0, 0))],
        out_specs=[pl.BlockSpec((1, 1, tm), lambda i: (i, 0, 0)),
                   pl.BlockSpec((n_cls_pad, LANES), lambda i: (0, 0))],
        out_shape=[jax.ShapeDtypeStruct((n_t, 1, tm), jnp.int32),
                   jax.ShapeDtypeStruct((n_cls_pad, LANES), F32)],
        scratch_shapes=[pltpu.VMEM((n_cls_pad, 1), F32)],
        compiler_params=_params("arbitrary"), name="moe_class_rank",
    )(cls)


def _row_copy(src, src_row, dst, dst_row, sem):
    return pltpu.make_async_copy(src.at[pl.ds(src_row, 1)], dst.at[pl.ds(dst_row, 1)], sem)


def _for_each_row(n_rows, fn):
    def group(g, carry):
        r0 = pl.multiple_of(g * SUBLANES, SUBLANES)
        for j in range(SUBLANES):
            fn(r0 + j)
        return carry

    lax.fori_loop(0, n_rows // SUBLANES, group, 0)


def _dispatch_kernel(dest_ref, x_ref, o_hbm, sem):
    tm = x_ref.shape[0]
    base = pl.program_id(0) * tm
    copy = lambda r: _row_copy(x_ref, r, o_hbm, dest_ref[base + r], sem)
    _for_each_row(tm, lambda r: copy(r).start())
    _for_each_row(tm, lambda r: copy(r).wait())


def _dispatch(xa, dest, tm=2048):
    T, W = xa.shape
    tm = min(tm, T)
    return pl.pallas_call(
        _dispatch_kernel,
        grid_spec=pltpu.PrefetchScalarGridSpec(
            num_scalar_prefetch=1, grid=(T // tm,),
            in_specs=[pl.BlockSpec((tm, W), lambda i, dest: (i, 0))],
            out_specs=pl.BlockSpec(memory_space=pl.ANY),
            scratch_shapes=[pltpu.SemaphoreType.DMA(())]),
        out_shape=jax.ShapeDtypeStruct((T, W), xa.dtype),
        compiler_params=_params("arbitrary"), name="moe_dispatch",
    )(dest, xa)


def _undispatch_kernel(dest_ref, y_hbm, o_ref, sem):
    tm = o_ref.shape[0]
    base = pl.program_id(0) * tm
    copy = lambda r: _row_copy(y_hbm, dest_ref[base + r], o_ref, r, sem)
    _for_each_row(tm, lambda r: copy(r).start())
    _for_each_row(tm, lambda r: copy(r).wait())


def _undispatch(ys, dest, tm=2048):
    T, D = ys.shape
    tm = min(tm, T)
    return pl.pallas_call(
        _undispatch_kernel,
        grid_spec=pltpu.PrefetchScalarGridSpec(
            num_scalar_prefetch=1, grid=(T // tm,),
            in_specs=[pl.BlockSpec(memory_space=pl.ANY)],
            out_specs=pl.BlockSpec((tm, D), lambda i, dest: (i, 0)),
            scratch_shapes=[pltpu.SemaphoreType.DMA(())]),
        out_shape=jax.ShapeDtypeStruct((T, D), ys.dtype),
        compiler_params=_params("arbitrary"), name="moe_undispatch",
    )(dest, ys)


STEP_FIRST, STEP_LAST, STEP_VALID, STEP_NEW_EXPERT = 1, 2, 4, 8


def _moe_kernel(tile_ref, exp_ref, flag_ref, xs_ref, wg_ref, wu_ref, wd_ref, g_ref, b_ref, o_ref,
                xbf_ref, acc_ref, wgb_ref, wub_ref, wdb_ref, *, alpha):
    step = pl.program_id(0)
    flags = flag_ref[step]
    e = exp_ref[step]
    D = o_ref.shape[-1]

    @pl.when((flags & STEP_NEW_EXPERT) != 0)
    def _():
        wgb_ref[...] = wg_ref[0, 0].astype(BF16)
        wub_ref[...] = wu_ref[0, 0].astype(BF16)

    @pl.when((flags & STEP_FIRST) != 0)
    def _():
        xbf_ref[...] = xs_ref[:, :D].astype(BF16)
        acc_ref[...] = jnp.zeros_like(acc_ref)

    @pl.when((flags & STEP_VALID) != 0)
    def _():
        half = xbf_ref.shape[0] // 2

        def up(rs):
            xb = xbf_ref[rs, :]
            return (jnp.dot(xb, wgb_ref[...], preferred_element_type=F32),
                    jnp.dot(xb, wub_ref[...], preferred_element_type=F32))

        def act(hg, hu):
            return (hg * jax.nn.sigmoid(hg) * hu).astype(BF16)

        def down(h):
            return jnp.dot(h, wdb_ref[...], preferred_element_type=F32)

        def accumulate(rs, y):
            comb = xs_ref[rs, D:]
            lane = lax.broadcasted_iota(jnp.int32, comb.shape, 1)
            w_e = jnp.sum(jnp.where(lane == e, comb, 0.0), axis=-1, keepdims=True)
            acc_ref[rs, :] += w_e * y

        rows_a, rows_b = slice(0, half), slice(half, 2 * half)
        up_a = up(rows_a)
        up_b = up(rows_b)
        wdb_ref[...] = wd_ref[0, 0].astype(BF16)
        y_a = down(act(*up_a))
        y_b = down(act(*up_b))
        accumulate(rows_a, y_a)
        accumulate(rows_b, y_b)

    @pl.when((flags & STEP_LAST) != 0)
    def _():
        o_ref[...] = _ln(alpha * xs_ref[:, :D] + acc_ref[...], g_ref[...], b_ref[...])


def _moe_steps(counts, n_experts, n_tiles, tm):
    n_cls = counts.shape[0]
    per_group = n_experts // N_GROUPS
    pairs = _expert_pairs(per_group)
    uses = [[0.0] * n_experts for _ in range(n_cls)]
    for c in range(n_cls):
        gidx, (i, j) = c // len(pairs), pairs[c % len(pairs)]
        uses[c][gidx * per_group + i] = 1.0
        uses[c][gidx * per_group + j] = 1.0
    uses = jnp.array(uses, F32)
    ends = jnp.cumsum(counts)
    starts = ends - counts
    lo = (jnp.arange(n_tiles, dtype=jnp.int32) * tm)[:, None]
    present = jnp.logical_and(starts[None, :] < lo + tm, ends[None, :] > lo)
    need = jnp.dot(present.astype(F32), uses) > 0
    odd = (jnp.arange(n_tiles, dtype=jnp.int32) % 2 == 1)[:, None]
    need = jnp.where(odd, need[:, ::-1], need).reshape(-1)
    n_steps = min(n_tiles * n_experts, TOP_K * (n_tiles + n_cls - 1))
    n_valid = jnp.sum(need.astype(jnp.int32))
    idx = jnp.nonzero(need, size=n_steps, fill_value=0)[0].astype(jnp.int32)
    k = jnp.arange(n_steps, dtype=jnp.int32)
    valid = k < n_valid
    idx = jnp.where(valid, idx, idx[n_valid - 1])
    tile, expert = idx // n_experts, idx % n_experts
    expert = jnp.where(tile % 2 == 1, n_experts - 1 - expert, expert)
    prev_tile = jnp.concatenate([jnp.full((1,), -1, jnp.int32), tile[:-1]])
    next_tile = jnp.concatenate([tile[1:], jnp.full((1,), -1, jnp.int32)])
    first = jnp.logical_and(valid, tile != prev_tile)
    last = jnp.logical_and(valid, jnp.logical_or(tile != next_tile, k == n_valid - 1))
    prev_expert = jnp.concatenate([jnp.full((1,), -1, jnp.int32), expert[:-1]])
    new_expert = jnp.logical_and(valid, expert != prev_expert)
    flags = (first * STEP_FIRST + last * STEP_LAST + valid * STEP_VALID
             + new_expert * STEP_NEW_EXPERT).astype(jnp.int32)
    return tile, expert, flags


def _moe_experts(xs, steps, layer, w_gate, w_up, w_down, ln_g, ln_b, alpha, tm):
    T, W = xs.shape
    D = W - LANES
    F = w_gate.shape[-1]
    tile, expert, flags = steps
    return pl.pallas_call(
        functools.partial(_moe_kernel, alpha=alpha),
        grid_spec=pltpu.PrefetchScalarGridSpec(
            num_scalar_prefetch=3, grid=(tile.shape[0],),
            in_specs=[pl.BlockSpec((tm, W), lambda s, t, e, f: (t[s], 0)),
                      pl.BlockSpec((1, 1, D, F), lambda s, t, e, f: (layer, e[s], 0, 0)),
                      pl.BlockSpec((1, 1, D, F), lambda s, t, e, f: (layer, e[s], 0, 0)),
                      pl.BlockSpec((1, 1, F, D), lambda s, t, e, f: (layer, e[s], 0, 0)),
                      pl.BlockSpec((1, D), lambda s, t, e, f: (0, 0)),
                      pl.BlockSpec((1, D), lambda s, t, e, f: (0, 0))],
            out_specs=pl.BlockSpec((tm, D), lambda s, t, e, f: (t[s], 0)),
            scratch_shapes=[pltpu.VMEM((tm, D), BF16), pltpu.VMEM((tm, D), F32),
                            pltpu.VMEM((D, F), BF16), pltpu.VMEM((D, F), BF16), pltpu.VMEM((F, D), BF16)]),
        out_shape=jax.ShapeDtypeStruct((T, D), F32),
        compiler_params=_params("arbitrary"), name="moe_experts_ln",
    )(tile, expert, flags, xs, w_gate, w_up, w_down, ln_g.reshape(1, D), ln_b.reshape(1, D))


def _moe(xa, cls, layer, w_gate, w_up, w_down, ln_g, ln_b, alpha, tm=512):
    T = xa.shape[0]
    E = w_gate.shape[1]
    n_cls = N_GROUPS * len(_expert_pairs(E // N_GROUPS))
    rank, cnt = _class_ranks(cls, -(-n_cls // 8) * 8)
    counts = cnt[:n_cls, 0].astype(jnp.int32)
    starts = jnp.cumsum(counts) - counts
    dest = (starts[cls.reshape(T)] + rank.reshape(T)).astype(jnp.int32)
    xs = _dispatch(xa, dest)
    ys = _moe_experts(xs, _moe_steps(counts, E, T // tm, tm), layer, w_gate, w_up, w_down, ln_g, ln_b, alpha, tm)
    return _undispatch(ys, dest)


def _rotary_tables(S):
    half = RET_HD // 2
    inv_freq = ROPE_BASE ** (-jnp.arange(half, dtype=F32) / half)
    ang = jnp.arange(S).astype(F32)[:, None] * inv_freq[None, :]
    cos, sin = jnp.cos(ang), jnp.sin(ang)
    return jnp.concatenate([cos, cos], axis=1), jnp.concatenate([-sin, sin], axis=1)


def kernel(x, mem, ln_in_g, ln_in_b, router_w, router_bias, w_in, ret_gn_g, rwkv_mu, rwkv_w_up, rwkv_w0,
           rwkv_a_up, rwkv_a0, rwkv_g_up, rwkv_k_k, rwkv_k_a, rwkv_r_k, rwkv_ln_g, rwkv_ln_b,
           w_ret_up, w_rwkv_up, w_out, ln1_g, ln1_b, xa_wq, xa_wkv, xa_wo, ln2_g, ln2_b,
           moe_w_gate, moe_w_up, moe_w_down, ln3_g, ln3_b):
    B, S, D = x.shape
    T = B * S
    depth = w_in.shape[0]
    alpha = (2 * depth) ** 0.25
    ret_w = ret_gn_g.shape[-1]
    rwkv_w = rwkv_w0.shape[-1]
    ret_cols = 4 * ret_w
    rwkv_cols = rwkv_mu.shape[-1]
    n_proj = -(-(ret_cols + rwkv_cols) // IN_PROJ_TN) * IN_PROJ_TN
    cos2, sin2 = _rotary_tables(S)

    xs = x.reshape(T, D)
    for l in range(depth):
        w_l = w_in[l]
        entry_ln = (ln_in_g, ln_in_b) if l == 0 else None
        p3 = _in_proj(xs, w_in, l, n_proj, ln=entry_ln).reshape(B, S, n_proj)
        y_ret = _retention(p3, cos2, sin2, ret_gn_g[l])
        y_rwkv = _rwkv(p3, ret_cols, rwkv_mu[l], rwkv_w_up[l], rwkv_w0[l], rwkv_a_up[l],
                       rwkv_a0[l], rwkv_g_up[l], rwkv_k_k[l], rwkv_k_a[l], rwkv_r_k[l].reshape(-1),
                       rwkv_ln_g[l], rwkv_ln_b[l])
        xs = _merge(y_ret.reshape(T, ret_w), y_rwkv.reshape(T, rwkv_w), xs, entry_ln,
                    w_l[:, ret_cols + rwkv_cols:].astype(BF16),
                    w_ret_up[l].astype(BF16), w_rwkv_up[l].astype(BF16), w_out[l].astype(BF16),
                    ln1_g[l], ln1_b[l], alpha)
        wq_scaled = (xa_wq[l] * ((D // XA_HEADS) ** -0.5)).astype(BF16)
        xa, cls = _cross_attention(xs.reshape(B, S, D), mem, xa_wkv, l, wq_scaled,
                                   xa_wo[l].astype(BF16), ln2_g[l], ln2_b[l], router_w, router_bias, alpha)
        xs = _moe(xa.reshape(T, D + LANES), cls, l, moe_w_gate, moe_w_up, moe_w_down, ln3_g[l], ln3_b[l], alpha)
    return xs.reshape(B, S, D)
```

```python
import functools
import math

import jax
import jax.numpy as jnp
from jax import lax
from jax.experimental import pallas as pl
from jax.experimental.pallas import tpu as pltpu

F32 = jnp.float32
BF16 = jnp.bfloat16

LANES = 128
SUBLANES = 8
VMEM_LIMIT = 56 * 1024 * 1024

CHUNK = 64
RET_HEADS = 4
RET_HD = 128
ROPE_BASE = 10000.0
RWKV_HD = 64
HD_SHIFT = 6
RWKV_GN_EPS = 64e-5
XA_HEADS = 4
XA_ROW_PARTS = 4
N_GROUPS = 4
TOP_K = 2
LN_EPS = 1e-5
IN_PROJ_TN = 2048
IN_PROJ_ROW_PARTS = 2
HEADS_PER_GROUP = 4
GROUP_W = HEADS_PER_GROUP * RWKV_HD


def _params(*sem):
    return pltpu.CompilerParams(dimension_semantics=sem, vmem_limit_bytes=VMEM_LIMIT)


def _ln(v, g, b, eps=LN_EPS):
    mu = jnp.mean(v, axis=-1, keepdims=True)
    d = v - mu
    var = jnp.mean(d * d, axis=-1, keepdims=True)
    return d * lax.rsqrt(var + eps) * g + b


def _dot(a, b):
    return jnp.dot(a.astype(BF16), b.astype(BF16), preferred_element_type=F32)


def _dot_nt(a, b):
    return lax.dot_general(a.astype(BF16), b.astype(BF16), (((1,), (1,)), ((), ())),
                           preferred_element_type=F32)


def _dot_tn(a, b):
    return lax.dot_general(a.astype(BF16), b.astype(BF16), (((0,), (0,)), ((), ())),
                           preferred_element_type=F32)


def _in_proj_kernel(*refs, with_ln):
    if with_ln:
        a_ref, g_ref, b_ref, w_ref, o_ref, wbf_ref = refs
    else:
        a_ref, w_ref, o_ref, wbf_ref = refs

    @pl.when(pl.program_id(1) == 0)
    def _():
        wbf_ref[...] = w_ref[0].astype(BF16)

    tm = a_ref.shape[0]
    part = tm // IN_PROJ_ROW_PARTS

    def prepared(rs):
        a = a_ref[rs, :]
        if with_ln:
            a = _ln(a, g_ref[...], b_ref[...])
        return a.astype(BF16)

    rows = [slice(i * part, (i + 1) * part) for i in range(IN_PROJ_ROW_PARTS)]
    cur = prepared(rows[0])
    for i, rs in enumerate(rows):
        nxt = prepared(rows[i + 1]) if i + 1 < len(rows) else None
        o_ref[rs, :] = jnp.dot(cur, wbf_ref[...], preferred_element_type=F32).astype(o_ref.dtype)
        cur = nxt


def _in_proj(a, w_in, layer, n_cols, ln=None, tm=1024):
    M, K = a.shape
    tn = IN_PROJ_TN
    assert n_cols % tn == 0 and n_cols <= w_in.shape[-1] and M % tm == 0
    with_ln = ln is not None
    vec = pl.BlockSpec((1, K), lambda j, i: (0, 0))
    in_specs = [pl.BlockSpec((tm, K), lambda j, i: (i, 0))]
    in_specs += [vec, vec] if with_ln else []
    in_specs += [pl.BlockSpec((1, K, tn), lambda j, i: (layer, 0, j), pipeline_mode=pl.Buffered(1))]
    args = (a,) + ((ln[0].reshape(1, K), ln[1].reshape(1, K)) if with_ln else ()) + (w_in,)
    return pl.pallas_call(
        functools.partial(_in_proj_kernel, with_ln=with_ln), grid=(n_cols // tn, M // tm),
        in_specs=in_specs, out_specs=pl.BlockSpec((tm, tn), lambda j, i: (i, j)),
        out_shape=jax.ShapeDtypeStruct((M, n_cols), BF16),
        scratch_shapes=[pltpu.VMEM((K, tn), BF16)],
        compiler_params=_params("arbitrary", "arbitrary"), name="ln_in_proj" if with_ln else "in_proj",
    )(*args)


def _ret_kernel(q_ref, k_ref, v_ref, g_ref, cos_ref, sin_ref, gn_ref, o_ref, state_ref, *, n_chunks):
    @pl.when(pl.program_id(1) == 0)
    def _():
        state_ref[...] = jnp.zeros_like(state_ref)

    C, hd = CHUNK, RET_HD
    ti = lax.broadcasted_iota(jnp.int32, (C, C), 0)
    si = lax.broadcasted_iota(jnp.int32, (C, C), 1)
    dist = jnp.abs(ti - si).astype(F32)
    cpos = lax.broadcasted_iota(jnp.int32, (C, 1), 0).astype(F32)
    log_gamma = [math.log(1.0 - 2.0 ** (-5.0 - h)) for h in range(RET_HEADS)]
    inner_mask = [jnp.exp(lg * dist) for lg in log_gamma]
    k_decay = [jnp.exp(lg * (C - 1.0 - cpos)) for lg in log_gamma]
    q_decay = [jnp.exp(lg * (cpos + 1.0)) for lg in log_gamma]
    pairs = [(c, h) for c in range(n_chunks) for h in range(RET_HEADS)]
    rows = lambda c: slice(c * C, (c + 1) * C)
    cols = lambda h: slice(h * hd, (h + 1) * hd)

    def rotated(ref, c, h):
        t = ref[0, rows(c), cols(h)].astype(F32)
        return t * cos_ref[rows(c), :] + pltpu.roll(t, hd // 2, axis=1) * sin_ref[rows(c), :]

    qr = [rotated(q_ref, c, h) for c, h in pairs]
    kr = [rotated(k_ref, c, h) * (hd ** -0.5) for c, h in pairs]
    v = [v_ref[0, rows(c), cols(h)] for c, h in pairs]
    scores = [_dot_nt(a, b) * inner_mask[h] for a, b, (c, h) in zip(qr, kr, pairs)]
    inner = [_dot(s, t) for s, t in zip(scores, v)]
    kv = [_dot_tn(b * k_decay[h], t) for b, t, (c, h) in zip(kr, v, pairs)]
    states = []
    for h in range(RET_HEADS):
        state = state_ref[h]
        for c in range(n_chunks):
            states.append(((c, h), state))
            state = math.exp(log_gamma[h] * C) * state + kv[pairs.index((c, h))]
        state_ref[h] = state
    entering = dict(states)
    cross = [_dot(a * q_decay[h], entering[(c, h)]) for a, (c, h) in zip(qr, pairs)]
    for y_in, y_cr, (c, h) in zip(inner, cross, pairs):
        y = y_in + y_cr
        mu = jnp.mean(y, axis=-1, keepdims=True)
        d = y - mu
        var = jnp.mean(d * d, axis=-1, keepdims=True)
        yn = d * lax.rsqrt(var + LN_EPS) * gn_ref[:, cols(h)]
        g = g_ref[0, rows(c), cols(h)].astype(F32)
        o_ref[0, rows(c), cols(h)] = (g * jax.nn.sigmoid(g) * yn).astype(o_ref.dtype)


def _retention(p_ret, cos2, sin2, gn_g, ts=256):
    B, S, _ = p_ret.shape
    W = gn_g.shape[-1]
    blk = lambda c: pl.BlockSpec((1, ts, W), lambda b, n, c=c: (b, n, c))
    return pl.pallas_call(
        functools.partial(_ret_kernel, n_chunks=ts // CHUNK), grid=(B, S // ts),
        in_specs=[blk(0), blk(1), blk(2), blk(3),
                  pl.BlockSpec((ts, RET_HD), lambda b, n: (n, 0)),
                  pl.BlockSpec((ts, RET_HD), lambda b, n: (n, 0)),
                  pl.BlockSpec((1, W), lambda b, n: (0, 0))],
        out_specs=pl.BlockSpec((1, ts, W), lambda b, n: (b, n, 0)),
        out_shape=jax.ShapeDtypeStruct((B, S, W), BF16),
        scratch_shapes=[pltpu.VMEM((RET_HEADS, RET_HD, RET_HD), F32)],
        compiler_params=_params("parallel", "arbitrary"), name="retention",
    )(p_ret, p_ret, p_ret, p_ret, cos2, sin2, gn_g.reshape(1, W))


def _head_sum(x):
    R, W = x.shape
    lo = lax.broadcasted_iota(jnp.int32, (R, LANES), 1) < RWKV_HD
    outs = []
    for j in range(W // LANES):
        xs = x[:, j * LANES:(j + 1) * LANES]
        s_lo = jnp.sum(jnp.where(lo, xs, 0.0), axis=-1, keepdims=True)
        s_hi = jnp.sum(jnp.where(lo, 0.0, xs), axis=-1, keepdims=True)
        outs.append(jnp.where(lo, s_lo, s_hi))
    return jnp.concatenate(outs, axis=1)


def _block_diag(x, head_of_lane):
    parts = [jnp.where(head_of_lane == h, x, 0.0) for h in range(HEADS_PER_GROUP)]
    return jnp.concatenate(parts, axis=0).astype(BF16)


def _packed_mm(xs, ys, head_of_lane):
    bds = [_block_diag(y, head_of_lane) for y in ys]
    return [jnp.dot(x.astype(BF16), bd, preferred_element_type=F32) for x, bd in zip(xs, bds)]


def _unit_lower_inverse(a_list, t_idx, s_idx, head_of_lane):
    mm = functools.partial(_packed_mm, head_of_lane=head_of_lane)
    add = lambda xs, ys: [x + y for x, y in zip(xs, ys)]
    lower = s_idx < t_idx
    same4 = (t_idx >> 2) == (s_idx >> 2)
    same16 = (t_idx >> 4) == (s_idx >> 4)
    m1 = lower & same4
    m2 = lower & same16 & jnp.logical_not(same4)
    m3 = lower & jnp.logical_not(same16)
    eye = jnp.where(t_idx == s_idx, 1.0, 0.0)
    a1 = [jnp.where(m1, a, 0.0) for a in a_list]
    a1sq = mm(a1, a1)
    a1cu = mm(a1, a1sq)
    inv = [eye + p + q + r for p, q, r in zip(a1, a1sq, a1cu)]
    for m in (m2, m3):
        n = mm(inv, [jnp.where(m, a, 0.0) for a in a_list])
        x = add(inv, mm(mm(n, n), inv))
        inv = add(x, mm(n, x))
    return inv


def _rwkv_kernel(*refs):
    state_ref = refs[-1]

    @pl.when(pl.program_id(1) == 0)
    def _():
        for ref in refs[-5:]:
            ref[...] = jnp.zeros_like(ref)

    seqs = [_rwkv_prepare(bi, *refs) for bi in range(state_ref.shape[0])]
    _rwkv_chains(seqs, state_ref)
    for bi, seq in enumerate(seqs):
        _rwkv_finish(bi, seq, *refs)


def _rwkv_prepare(bi, pr_ref, pk_ref, pv_ref, pl_ref, mur_ref, muk_ref, muv_ref, mul_ref,
                  wup_ref, aup_ref, gup_ref, w0_ref, a0_ref, kk_ref, ka_ref, rk_ref, lng_ref, lnb_ref,
                  o_ref, carry_r, carry_k, carry_v, carry_l, state_ref):
    C = CHUNK

    def shifted(p_ref, mu_ref, carry):
        p = p_ref[bi].astype(F32)
        row = lax.broadcasted_iota(jnp.int32, p.shape, 0)
        prev = jnp.where(row == 0, carry[bi, 0:1, :], pltpu.roll(p, 1, axis=0))
        carry[bi, 0:1, :] = p[C - 1:C, :]
        return p + mu_ref[...] * (prev - p)

    r = shifted(pr_ref, mur_ref, carry_r)
    k = shifted(pk_ref, muk_ref, carry_k)
    v = shifted(pv_ref, muv_ref, carry_v)
    lo = shifted(pl_ref, mul_ref, carry_l)
    dwa = lo[:, :LANES]
    dg = lo[:, LANES:]

    z = w0_ref[...] + _dot(jnp.tanh(dwa), wup_ref[...])
    logw = -math.exp(-0.5) * jax.nn.sigmoid(z)
    a = jax.nn.sigmoid(a0_ref[...] + _dot(dwa, aup_ref[...]))
    g = _dot(jax.nn.sigmoid(dg), gup_ref[...])

    kk = k * kk_ref[...]
    kk = kk * jnp.minimum(lax.rsqrt(_head_sum(kk * kk)), 1e12)
    k = k * (1.0 + (a - 1.0) * ka_ref[...])

    ti = lax.broadcasted_iota(jnp.int32, (C, C), 0)
    si = lax.broadcasted_iota(jnp.int32, (C, C), 1)
    tril = jnp.where(si <= ti, 1.0, 0.0).astype(BF16)
    lw_hi = logw.astype(BF16)
    lw_lo = (logw - lw_hi.astype(F32)).astype(BF16)
    cum = (jnp.dot(tril, lw_hi, preferred_element_type=F32)
           + jnp.dot(tril, lw_lo, preferred_element_type=F32))
    cum_last = cum[C - 1:C, :]
    e_neg = jnp.exp(-cum)
    abar = -kk * jnp.exp(cum - logw)
    rbar = r * jnp.exp(cum)
    kka = kk * a
    btil = kka * e_neg
    ktil = k * e_neg
    gamma_end = jnp.exp(cum_last)
    e_end = gamma_end * e_neg
    bend = kka * e_end
    kend = k * e_end
    return dict(r=r, k=k, v=v, g=g, abar=abar, rbar=rbar, btil=btil, ktil=ktil, bend=bend, kend=kend,
                gamma_end=gamma_end)


def _rwkv_chains(seqs, state_ref):
    C = CHUNK
    n_groups = state_ref.shape[1]
    t_idx = lax.broadcasted_iota(jnp.int32, (C, GROUP_W), 0)
    lane = lax.broadcasted_iota(jnp.int32, (C, GROUP_W), 1)
    s_idx = lane & (RWKV_HD - 1)
    head_of_lane = lane >> HD_SHIFT
    strict = s_idx < t_idx
    incl = s_idx <= t_idx
    bd_r = lax.broadcasted_iota(jnp.int32, (GROUP_W, GROUP_W), 0) >> HD_SHIFT
    bd_c = lax.broadcasted_iota(jnp.int32, (GROUP_W, GROUP_W), 1) >> HD_SHIFT
    on_diag = bd_r == bd_c

    chains = [(bi, gi) for bi in range(len(seqs)) for gi in range(n_groups)]
    part = lambda name: [seqs[bi][name][:, gi * GROUP_W:(gi + 1) * GROUP_W] for bi, gi in chains]
    mm = functools.partial(_packed_mm, head_of_lane=head_of_lane)
    v_g = part("v")
    lhs = [jnp.concatenate([p, q], axis=0) for p, q in zip(part("abar"), part("rbar"))]
    rhs = [jnp.concatenate([_block_diag(p, head_of_lane), _block_diag(q, head_of_lane)], axis=0)
           for p, q in zip(part("btil"), part("ktil"))]
    gram = [_dot_nt(p, q) for p, q in zip(lhs, rhs)]
    a_ab = [jnp.where(strict, gm[:C, :GROUP_W], 0.0) for gm in gram]
    a_ak = [jnp.where(strict, gm[:C, GROUP_W:], 0.0) for gm in gram]
    m_rb = [jnp.where(incl, gm[C:, :GROUP_W], 0.0) for gm in gram]
    m_rk = [jnp.where(incl, gm[C:, GROUP_W:], 0.0) for gm in gram]
    inv = _unit_lower_inverse(a_ab, t_idx, s_idx, head_of_lane)
    st = [state_ref[bi, gi] for bi, gi in chains]
    from_state = [_dot_nt(p, q) for p, q in zip(lhs, st)]
    local = mm([jnp.concatenate([p, q], axis=0) for p, q in zip(a_ak, m_rk)], v_g)
    u = mm(inv, [fs[:C] + lc[:C] for fs, lc in zip(from_state, local)])
    via_u = mm(m_rb, u)
    upd = [_dot_tn(jnp.concatenate([p, q], axis=0), jnp.concatenate([b, k], axis=0))
           for p, q, b, k in zip(u, v_g, part("bend"), part("kend"))]
    for (bi, gi), s_old, up, ge in zip(chains, st, upd, part("gamma_end")):
        state_ref[bi, gi] = s_old * ge + jnp.where(on_diag, up, 0.0)
    ys = [fs[C:] + p + lc[C:] for fs, p, lc in zip(from_state, via_u, local)]
    for bi, seq in enumerate(seqs):
        seq["y"] = jnp.concatenate(ys[bi * n_groups:(bi + 1) * n_groups], axis=1)


def _rwkv_finish(bi, seq, pr_ref, pk_ref, pv_ref, pl_ref, mur_ref, muk_ref, muv_ref, mul_ref,
                 wup_ref, aup_ref, gup_ref, w0_ref, a0_ref, kk_ref, ka_ref, rk_ref, lng_ref, lnb_ref,
                 o_ref, carry_r, carry_k, carry_v, carry_l, state_ref):
    y, r, k, v, g = seq["y"], seq["r"], seq["k"], seq["v"], seq["g"]
    inv_hd = 1.0 / RWKV_HD
    mu = _head_sum(y) * inv_hd
    d = y - mu
    var = _head_sum(d * d) * inv_hd
    yn = d * lax.rsqrt(var + RWKV_GN_EPS) * lng_ref[...] + lnb_ref[...]
    bonus = _head_sum(r * k * rk_ref[...]) * v
    o_ref[bi] = ((yn + bonus) * g).astype(o_ref.dtype)


def _rwkv(p_rwkv, col0, mu, w_up, w0, a_up, a0, g_up, k_k, k_a, r_k, ln_g, ln_b, nb=4):
    B, S, _ = p_rwkv.shape
    W = w0.shape[-1]
    C = CHUNK
    wup_pad = jnp.concatenate([w_up, jnp.zeros_like(a_up)], axis=0).astype(BF16)
    aup_pad = jnp.concatenate([jnp.zeros_like(w_up), a_up], axis=0).astype(BF16)
    row = lambda t: t.reshape(1, -1)
    vec = lambda n: pl.BlockSpec((1, n), lambda b, t: (0, 0))
    mat = lambda a: pl.BlockSpec(a.shape, lambda b, t: (0, 0))
    lw = mu.shape[-1] - 3 * W
    assert col0 % W == 0 and (col0 + 3 * W) % lw == 0 and B % nb == 0 and S % C == 0
    cblk = lambda c: pl.BlockSpec((nb, C, W), lambda b, t, c=c: (b, t, col0 // W + c))
    n_groups = W // GROUP_W
    gup = g_up.astype(BF16)
    return pl.pallas_call(
        _rwkv_kernel, grid=(B // nb, S // C),
        in_specs=[cblk(0), cblk(1), cblk(2),
                  pl.BlockSpec((nb, C, lw), lambda b, t: (b, t, (col0 + 3 * W) // lw)),
                  vec(W), vec(W), vec(W), vec(lw),
                  mat(wup_pad), mat(aup_pad), mat(gup),
                  vec(W), vec(W), vec(W), vec(W), vec(W), vec(W), vec(W)],
        out_specs=pl.BlockSpec((nb, C, W), lambda b, t: (b, t, 0)),
        out_shape=jax.ShapeDtypeStruct((B, S, W), BF16),
        scratch_shapes=[pltpu.VMEM((nb, 8, W), F32), pltpu.VMEM((nb, 8, W), F32), pltpu.VMEM((nb, 8, W), F32),
                        pltpu.VMEM((nb, 8, lw), F32),
                        pltpu.VMEM((nb, n_groups, GROUP_W, GROUP_W), F32)],
        compiler_params=_params("parallel", "arbitrary"), name="rwkv7",
    )(p_rwkv, p_rwkv, p_rwkv, p_rwkv,
      row(mu[:W]), row(mu[W:2 * W]), row(mu[2 * W:3 * W]), row(mu[3 * W:]),
      wup_pad, aup_pad, gup,
      row(w0), row(a0), row(k_k), row(k_a), row(r_k), row(ln_g), row(ln_b))


def _merge_kernel(*refs, alpha, with_ln):
    if with_ln:
        yr_ref, yw_ref, x_ref, pg_ref, pb_ref, wg_ref, wr_ref, ww_ref, wo_ref, g_ref, b_ref, o_ref = refs
    else:
        yr_ref, yw_ref, x_ref, wg_ref, wr_ref, ww_ref, wo_ref, g_ref, b_ref, o_ref = refs
    tm, D = x_ref.shape
    half = tm // 2

    def block_input(rs):
        x = x_ref[rs, :]
        return _ln(x, pg_ref[...], pb_ref[...]) if with_ln else x

    def branches(rs):
        return (jnp.dot(yr_ref[rs, :], wr_ref[...], preferred_element_type=F32),
                jnp.dot(yw_ref[rs, :], ww_ref[...], preferred_element_type=F32),
                jnp.dot(block_input(rs).astype(BF16), wg_ref[...], preferred_element_type=F32))

    def mix(rs, up):
        gate = jax.nn.sigmoid(up[2])
        return (gate[:, :D] * up[0] + gate[:, D:] * up[1]).astype(BF16)

    def out_proj(m):
        return jnp.dot(m, wo_ref[...], preferred_element_type=F32)

    def finish(rs, h):
        o_ref[rs, :] = _ln(alpha * block_input(rs) + h, g_ref[...], b_ref[...])

    rows_a, rows_b = slice(0, half), slice(half, tm)
    up_a = branches(rows_a)
    up_b = branches(rows_b)
    h_a = out_proj(mix(rows_a, up_a))
    h_b = out_proj(mix(rows_b, up_b))
    finish(rows_a, h_a)
    finish(rows_b, h_b)


def _merge(y_ret, y_rwkv, x, pre_ln, w_gate, w_ret_up, w_rwkv_up, w_out, ln_g, ln_b, alpha, tm=512):
    T, D = x.shape
    W = y_ret.shape[1]
    full = lambda a: pl.BlockSpec(a.shape, lambda i: (0, 0))
    rowblk = lambda n: pl.BlockSpec((tm, n), lambda i: (i, 0))
    vec = pl.BlockSpec((1, D), lambda i: (0, 0))
    with_ln = pre_ln is not None
    pre = (pre_ln[0].reshape(1, D), pre_ln[1].reshape(1, D)) if with_ln else ()
    return pl.pallas_call(
        functools.partial(_merge_kernel, alpha=alpha, with_ln=with_ln), grid=(T // tm,),
        in_specs=[rowblk(W), rowblk(W), rowblk(D)] + [vec] * len(pre)
        + [full(w_gate), full(w_ret_up), full(w_rwkv_up), full(w_out), vec, vec],
        out_specs=rowblk(D),
        out_shape=jax.ShapeDtypeStruct((T, D), F32),
        compiler_params=_params("parallel"), name="merge_out_ln",
    )(y_ret, y_rwkv, x, *pre, w_gate, w_ret_up, w_rwkv_up, w_out, ln_g.reshape(1, D), ln_b.reshape(1, D))


def _xattn_kernel(x_ref, mem_ref, wkv_ref, wq_ref, wo_ref, g_ref, b_ref, rwt_ref, rb_ref, o_ref, cls_ref,
                  wkvb_ref, kv_ref, *, alpha):
    tm, D = x_ref.shape[1:]
    hd = D // XA_HEADS
    half = tm // XA_ROW_PARTS

    @pl.when(jnp.logical_and(pl.program_id(0) == 0, pl.program_id(1) == 0))
    def _():
        wkvb_ref[...] = wkv_ref[0].astype(BF16)

    @pl.when(pl.program_id(1) == 0)
    def _():
        kv_ref[...] = jnp.dot(mem_ref[0].astype(BF16), wkvb_ref[...],
                              preferred_element_type=F32).astype(BF16)

    def project(rs):
        return jnp.dot(x_ref[0, rs, :].astype(BF16), wq_ref[...], preferred_element_type=F32)

    def scores(q):
        return [_dot_nt(q[:, h * hd:(h + 1) * hd], kv_ref[:, h * hd:(h + 1) * hd]) for h in range(XA_HEADS)]

    def attend(s_heads):
        outs = []
        for h, s in enumerate(s_heads):
            e = jnp.exp(s - jnp.max(s, axis=-1, keepdims=True))
            probs = e * (1.0 / jnp.sum(e, axis=-1, keepdims=True))
            outs.append(jnp.dot(probs.astype(BF16), kv_ref[:, D + h * hd:D + (h + 1) * hd],
                                preferred_element_type=F32))
        return jnp.concatenate(outs, axis=1)

    def out_proj(o):
        return jnp.dot(o.astype(BF16), wo_ref[...], preferred_element_type=F32)

    def finish(rs, h_out):
        x2 = _ln(alpha * x_ref[0, rs, :] + h_out, g_ref[...], b_ref[...])
        comb, cls = _route(x2, rwt_ref[...], rb_ref[...])
        E = comb.shape[0]
        comb_pad = jnp.concatenate([comb, jnp.zeros((LANES - E, half), F32)], axis=0)
        o_ref[0, rs, :D] = x2
        o_ref[0, rs, D:] = comb_pad.T
        cls_ref[0, :, rs] = cls

    rows = [slice(i * half, (i + 1) * half) for i in range(tm // half)]
    s_cur, h_prev = scores(project(rows[0])), None
    for i, rs in enumerate(rows):
        q_next = project(rows[i + 1]) if i + 1 < len(rows) else None
        o = attend(s_cur)
        if q_next is not None:
            s_cur = scores(q_next)
        if h_prev is not None:
            finish(rows[i - 1], h_prev)
        h_prev = out_proj(o)
    finish(rows[-1], h_prev)


def _cross_attention(x, mem, wkv, layer, wq, wo, ln_g, ln_b, router_w, router_bias, alpha, tm=1024):
    B, S, D = x.shape
    M = mem.shape[1]
    E = router_w.shape[1]
    full = lambda a: pl.BlockSpec(a.shape, lambda b, i: (0, 0))
    n_t = S // tm
    return pl.pallas_call(
        functools.partial(_xattn_kernel, alpha=alpha), grid=(B, n_t),
        in_specs=[pl.BlockSpec((1, tm, D), lambda b, i: (b, i, 0)),
                  pl.BlockSpec((1, M, D), lambda b, i: (b, 0, 0)),
                  pl.BlockSpec((1, D, 2 * D), lambda b, i: (layer, 0, 0), pipeline_mode=pl.Buffered(1)),
                  full(wq), full(wo),
                  pl.BlockSpec((1, D), lambda b, i: (0, 0)), pl.BlockSpec((1, D), lambda b, i: (0, 0)),
                  pl.BlockSpec((D, LANES), lambda b, i: (0, 0)), pl.BlockSpec((E, 1), lambda b, i: (0, 0))],
        out_specs=[pl.BlockSpec((1, tm, D + LANES), lambda b, i: (b, i, 0)),
                   pl.BlockSpec((1, 1, tm), lambda b, i: (b * n_t + i, 0, 0))],
        out_shape=[jax.ShapeDtypeStruct((B, S, D + LANES), F32),
                   jax.ShapeDtypeStruct((B * n_t, 1, tm), jnp.int32)],
        scratch_shapes=[pltpu.VMEM((D, 2 * D), BF16), pltpu.VMEM((M, 2 * D), BF16)],
        compiler_params=_params("arbitrary", "arbitrary"), name="xattn_ln_route",
    )(x, mem, wkv, wq, wo, ln_g.reshape(1, D), ln_b.reshape(1, D),
      jnp.pad(router_w, ((0, 0), (0, LANES - E))), router_bias.reshape(E, 1))


def _expert_pairs(per_group):
    todo = [(i, j) for i in range(per_group) for j in range(i + 1, per_group)]
    order = [todo.pop(0)]
    while todo:
        nxt = next((p for p in todo if set(p) & set(order[-1])), todo[0])
        todo.remove(nxt)
        order.append(nxt)
    return order


def _split_bf16(v):
    hi = v.astype(BF16)
    return hi, (v - hi.astype(F32)).astype(BF16)


def _route(x, w, bias):
    E = bias.shape[0]
    per_group = E // N_GROUPS
    x_hi, x_lo = _split_bf16(x)
    w_hi, w_lo = _split_bf16(w)
    logits = (jnp.dot(x_hi, w_hi, preferred_element_type=F32) + jnp.dot(x_hi, w_lo, preferred_element_type=F32)
              + jnp.dot(x_lo, w_hi, preferred_element_type=F32)).T[:E]
    aff = jax.nn.sigmoid(logits)
    choice = aff + bias
    rows = [choice[e:e + 1, :] for e in range(E)]
    scores = []
    for gidx in range(N_GROUPS):
        members = rows[gidx * per_group:(gidx + 1) * per_group]
        best = None
        for i in range(per_group):
            for j in range(i + 1, per_group):
                pair = members[i] + members[j]
                best = pair if best is None else jnp.maximum(best, pair)
        scores.append(best)
    top = scores[0]
    for s in scores[1:]:
        top = jnp.maximum(top, s)
    taken = jnp.zeros_like(top, dtype=jnp.bool_)
    in_best = []
    for s in scores:
        hit = jnp.logical_and(s == top, jnp.logical_not(taken))
        in_best.append(hit)
        taken = jnp.logical_or(taken, hit)
    sel_w = []
    cls = jnp.zeros(top.shape, jnp.int32)
    pairs = _expert_pairs(per_group)
    for gidx in range(N_GROUPS):
        members = rows[gidx * per_group:(gidx + 1) * per_group]
        chosen = []
        for i in range(per_group):
            rank = jnp.zeros_like(top)
            for j in range(per_group):
                if j == i:
                    continue
                ahead = (members[j] > members[i]) if j > i else (members[j] >= members[i])
                rank = rank + jnp.where(ahead, 1.0, 0.0)
            chosen.append(jnp.logical_and(in_best[gidx], rank < TOP_K))
            e = gidx * per_group + i
            sel_w.append(jnp.where(chosen[i], aff[e:e + 1, :], 0.0))
        for p, (i, j) in enumerate(pairs):
            cls = jnp.where(jnp.logical_and(chosen[i], chosen[j]), gidx * len(pairs) + p, cls)
    total = sel_w[0]
    for w in sel_w[1:]:
        total = total + w
    return jnp.concatenate(sel_w, axis=0) / total, cls


def _rank_kernel(cls_ref, rank_ref, cnt_ref, base_ref):
    @pl.when(pl.program_id(0) == 0)
    def _():
        base_ref[...] = jnp.zeros_like(base_ref)

    cls = cls_ref[0]
    tm = cls.shape[-1]
    n_cls = base_ref.shape[0]
    onehot = lax.broadcasted_iota(jnp.int32, (n_cls, tm), 0) == cls
    earlier = (lax.broadcasted_iota(jnp.int32, (tm, tm), 0)
               < lax.broadcasted_iota(jnp.int32, (tm, tm), 1))
    ones = jnp.where(onehot, 1.0, 0.0)
    before = jnp.dot(ones.astype(BF16), jnp.where(earlier, 1.0, 0.0).astype(BF16),
                     preferred_element_type=F32)
    base = base_ref[...]
    rank = jnp.sum(jnp.where(onehot, base + before, 0.0), axis=0, keepdims=True)
    rank_ref[0] = rank.astype(jnp.int32)
    total = base + jnp.sum(ones, axis=1, keepdims=True)
    base_ref[...] = total
    cnt_ref[...] = jnp.broadcast_to(total, cnt_ref.shape)


def _class_ranks(cls, n_cls_pad):
    n_t, _, tm = cls.shape
    return pl.pallas_call(
        _rank_kernel, grid=(n_t,),
        in_specs=[pl.BlockSpec((1, 1, tm), lambda i: (i, 0, 0))],
        out_specs=[pl.BlockSpec((1, 1, tm), lambda i: (i, 0, 0)),
                   pl.BlockSpec((n_cls_pad, LANES), lambda i: (0, 0))],
        out_shape=[jax.ShapeDtypeStruct((n_t, 1, tm), jnp.int32),
                   jax.ShapeDtypeStruct((n_cls_pad, LANES), F32)],
        scratch_shapes=[pltpu.VMEM((n_cls_pad, 1), F32)],
        compiler_params=_params("arbitrary"), name="moe_class_rank",
    )(cls)


def _row_copy(src, src_row, dst, dst_row, sem):
    return pltpu.make_async_copy(src.at[pl.ds(src_row, 1)], dst.at[pl.ds(dst_row, 1)], sem)


def _for_each_row(n_rows, fn):
    def group(g, carry):
        r0 = pl.multiple_of(g * SUBLANES, SUBLANES)
        for j in range(SUBLANES):
            fn(r0 + j, j)
        return carry

    lax.fori_loop(0, n_rows // SUBLANES, group, 0)


def _dispatch_kernel(dest_ref, x_ref, o_hbm, sem):
    tm = x_ref.shape[0]
    base = pl.program_id(0) * tm
    copy = lambda r: _row_copy(x_ref, r, o_hbm, dest_ref[base + r], sem)
    _for_each_row(tm, lambda r, j: copy(r).start(priority=j % 2))
    _for_each_row(tm, lambda r, j: copy(r).wait())


def _dispatch(xa, dest, tm=2048):
    T, W = xa.shape
    tm = min(tm, T)
    return pl.pallas_call(
        _dispatch_kernel,
        grid_spec=pltpu.PrefetchScalarGridSpec(
            num_scalar_prefetch=1, grid=(T // tm,),
            in_specs=[pl.BlockSpec((tm, W), lambda i, dest: (i, 0))],
            out_specs=pl.BlockSpec(memory_space=pl.ANY),
            scratch_shapes=[pltpu.SemaphoreType.DMA(())]),
        out_shape=jax.ShapeDtypeStruct((T, W), xa.dtype),
        compiler_params=_params("arbitrary"), name="moe_dispatch",
    )(dest, xa)


def _undispatch_kernel(dest_ref, y_hbm, o_ref, sem):
    tm = o_ref.shape[0]
    base = pl.program_id(0) * tm
    copy = lambda r: _row_copy(y_hbm, dest_ref[base + r], o_ref, r, sem)
    _for_each_row(tm, lambda r, j: copy(r).start(priority=j % 2))
    _for_each_row(tm, lambda r, j: copy(r).wait())


def _undispatch(ys, dest, tm=2048):
    T, D = ys.shape
    tm = min(tm, T)
    return pl.pallas_call(
        _undispatch_kernel,
        grid_spec=pltpu.PrefetchScalarGridSpec(
            num_scalar_prefetch=1, grid=(T // tm,),
            in_specs=[pl.BlockSpec(memory_space=pl.ANY)],
            out_specs=pl.BlockSpec((tm, D), lambda i, dest: (i, 0)),
            scratch_shapes=[pltpu.SemaphoreType.DMA(())]),
        out_shape=jax.ShapeDtypeStruct((T, D), ys.dtype),
        compiler_params=_params("arbitrary"), name="moe_undispatch",
    )(dest, ys)


STEP_FIRST, STEP_LAST, STEP_VALID, STEP_NEW_EXPERT = 1, 2, 4, 8


def _moe_kernel(tile_ref, exp_ref, flag_ref, xs_ref, wg_ref, wu_ref, wd_ref, g_ref, b_ref, o_ref,
                xbf_ref, acc_ref, wgb_ref, wub_ref, wdb_ref, *, alpha):
    step = pl.program_id(0)
    flags = flag_ref[step]
    e = exp_ref[step]
    D = o_ref.shape[-1]

    @pl.when((flags & STEP_NEW_EXPERT) != 0)
    def _():
        wgb_ref[...] = wg_ref[0, 0].astype(BF16)
        wub_ref[...] = wu_ref[0, 0].astype(BF16)

    @pl.when((flags & STEP_FIRST) != 0)
    def _():
        xbf_ref[...] = xs_ref[:, :D].astype(BF16)
        acc_ref[...] = jnp.zeros_like(acc_ref)

    @pl.when((flags & STEP_VALID) != 0)
    def _():
        half = xbf_ref.shape[0] // 2

        def up(rs):
            xb = xbf_ref[rs, :]
            return (jnp.dot(xb, wgb_ref[...], preferred_element_type=F32),
                    jnp.dot(xb, wub_ref[...], preferred_element_type=F32))

        def act(hg, hu):
            return (hg * jax.nn.sigmoid(hg) * hu).astype(BF16)

        def down(h):
            return jnp.dot(h, wdb_ref[...], preferred_element_type=F32)

        def accumulate(rs, y):
            comb = xs_ref[rs, D:]
            lane = lax.broadcasted_iota(jnp.int32, comb.shape, 1)
            w_e = jnp.sum(jnp.where(lane == e, comb, 0.0), axis=-1, keepdims=True)
            acc_ref[rs, :] += w_e * y

        rows_a, rows_b = slice(0, half), slice(half, 2 * half)
        up_a = up(rows_a)
        up_b = up(rows_b)
        wdb_ref[...] = wd_ref[0, 0].astype(BF16)
        y_a = down(act(*up_a))
        y_b = down(act(*up_b))
        accumulate(rows_a, y_a)
        accumulate(rows_b, y_b)

    @pl.when((flags & STEP_LAST) != 0)
    def _():
        o_ref[...] = _ln(alpha * xs_ref[:, :D] + acc_ref[...], g_ref[...], b_ref[...])


def _moe_steps(counts, n_experts, n_tiles, tm):
    n_cls = counts.shape[0]
    per_group = n_experts // N_GROUPS
    pairs = _expert_pairs(per_group)
    uses = [[0.0] * n_experts for _ in range(n_cls)]
    for c in range(n_cls):
        gidx, (i, j) = c // len(pairs), pairs[c % len(pairs)]
        uses[c][gidx * per_group + i] = 1.0
        uses[c][gidx * per_group + j] = 1.0
    uses = jnp.array(uses, F32)
    ends = jnp.cumsum(counts)
    starts = ends - counts
    lo = (jnp.arange(n_tiles, dtype=jnp.int32) * tm)[:, None]
    present = jnp.logical_and(starts[None, :] < lo + tm, ends[None, :] > lo)
    need = jnp.dot(present.astype(F32), uses) > 0
    odd = (jnp.arange(n_tiles, dtype=jnp.int32) % 2 == 1)[:, None]
    need = jnp.where(odd, need[:, ::-1], need).reshape(-1)
    n_steps = min(n_tiles * n_experts, TOP_K * (n_tiles + n_cls - 1))
    n_valid = jnp.sum(need.astype(jnp.int32))
    idx = jnp.nonzero(need, size=n_steps, fill_value=0)[0].astype(jnp.int32)
    k = jnp.arange(n_steps, dtype=jnp.int32)
    valid = k < n_valid
    idx = jnp.where(valid, idx, idx[n_valid - 1])
    tile, expert = idx // n_experts, idx % n_experts
    expert = jnp.where(tile % 2 == 1, n_experts - 1 - expert, expert)
    prev_tile = jnp.concatenate([jnp.full((1,), -1, jnp.int32), tile[:-1]])
    next_tile = jnp.concatenate([tile[1:], jnp.full((1,), -1, jnp.int32)])
    first = jnp.logical_and(valid, tile != prev_tile)
    last = jnp.logical_and(valid, jnp.logical_or(tile != next_tile, k == n_valid - 1))
    prev_expert = jnp.concatenate([jnp.full((1,), -1, jnp.int32), expert[:-1]])
    new_expert = jnp.logical_and(valid, expert != prev_expert)
    flags = (first * STEP_FIRST + last * STEP_LAST + valid * STEP_VALID
             + new_expert * STEP_NEW_EXPERT).astype(jnp.int32)
    return tile, expert, flags


def _moe_experts(xs, steps, layer, w_gate, w_up, w_down, ln_g, ln_b, alpha, tm):
    T, W = xs.shape
    D = W - LANES
    F = w_gate.shape[-1]
    tile, expert, flags = steps
    return pl.pallas_call(
        functools.partial(_moe_kernel, alpha=alpha),
        grid_spec=pltpu.PrefetchScalarGridSpec(
            num_scalar_prefetch=3, grid=(tile.shape[0],),
            in_specs=[pl.BlockSpec((tm, W), lambda s, t, e, f: (t[s], 0)),
                      pl.BlockSpec((1, 1, D, F), lambda s, t, e, f: (layer, e[s], 0, 0)),
                      pl.BlockSpec((1, 1, D, F), lambda s, t, e, f: (layer, e[s], 0, 0)),
                      pl.BlockSpec((1, 1, F, D), lambda s, t, e, f: (layer, e[s], 0, 0)),
                      pl.BlockSpec((1, D), lambda s, t, e, f: (0, 0)),
                      pl.BlockSpec((1, D), lambda s, t, e, f: (0, 0))],
            out_specs=pl.BlockSpec((tm, D), lambda s, t, e, f: (t[s], 0)),
            scratch_shapes=[pltpu.VMEM((tm, D), BF16), pltpu.VMEM((tm, D), F32),
                            pltpu.VMEM((D, F), BF16), pltpu.VMEM((D, F), BF16), pltpu.VMEM((F, D), BF16)]),
        out_shape=jax.ShapeDtypeStruct((T, D), F32),
        compiler_params=_params("arbitrary"), name="moe_experts_ln",
    )(tile, expert, flags, xs, w_gate, w_up, w_down, ln_g.reshape(1, D), ln_b.reshape(1, D))


def _moe(xa, cls, layer, w_gate, w_up, w_down, ln_g, ln_b, alpha, tm=512):
    T = xa.shape[0]
    E = w_gate.shape[1]
    n_cls = N_GROUPS * len(_expert_pairs(E // N_GROUPS))
    rank, cnt = _class_ranks(cls, -(-n_cls // 8) * 8)
    counts = cnt[:n_cls, 0].astype(jnp.int32)
    starts = jnp.cumsum(counts) - counts
    dest = (starts[cls.reshape(T)] + rank.reshape(T)).astype(jnp.int32)
    xs = _dispatch(xa, dest)
    ys = _moe_experts(xs, _moe_steps(counts, E, T // tm, tm), layer, w_gate, w_up, w_down, ln_g, ln_b, alpha, tm)
    return _undispatch(ys, dest)


def _rotary_tables(S):
    half = RET_HD // 2
    inv_freq = ROPE_BASE ** (-jnp.arange(half, dtype=F32) / half)
    ang = jnp.arange(S).astype(F32)[:, None] * inv_freq[None, :]
    cos, sin = jnp.cos(ang), jnp.sin(ang)
    return jnp.concatenate([cos, cos], axis=1), jnp.concatenate([-sin, sin], axis=1)


def kernel(x, mem, ln_in_g, ln_in_b, router_w, router_bias, w_in, ret_gn_g, rwkv_mu, rwkv_w_up, rwkv_w0,
           rwkv_a_up, rwkv_a0, rwkv_g_up, rwkv_k_k, rwkv_k_a, rwkv_r_k, rwkv_ln_g, rwkv_ln_b,
           w_ret_up, w_rwkv_up, w_out, ln1_g, ln1_b, xa_wq, xa_wkv, xa_wo, ln2_g, ln2_b,
           moe_w_gate, moe_w_up, moe_w_down, ln3_g, ln3_b):
    B, S, D = x.shape
    T = B * S
    depth = w_in.shape[0]
    alpha = (2 * depth) ** 0.25
    ret_w = ret_gn_g.shape[-1]
    rwkv_w = rwkv_w0.shape[-1]
    ret_cols = 4 * ret_w
    rwkv_cols = rwkv_mu.shape[-1]
    n_proj = -(-(ret_cols + rwkv_cols) // IN_PROJ_TN) * IN_PROJ_TN
    cos2, sin2 = _rotary_tables(S)

    xs = x.reshape(T, D)
    for l in range(depth):
        w_l = w_in[l]
        entry_ln = (ln_in_g, ln_in_b) if l == 0 else None
        p3 = _in_proj(xs, w_in, l, n_proj, ln=entry_ln).reshape(B, S, n_proj)
        y_ret = _retention(p3, cos2, sin2, ret_gn_g[l])
        y_rwkv = _rwkv(p3, ret_cols, rwkv_mu[l], rwkv_w_up[l], rwkv_w0[l], rwkv_a_up[l],
                       rwkv_a0[l], rwkv_g_up[l], rwkv_k_k[l], rwkv_k_a[l], rwkv_r_k[l].reshape(-1),
                       rwkv_ln_g[l], rwkv_ln_b[l])
        xs = _merge(y_ret.reshape(T, ret_w), y_rwkv.reshape(T, rwkv_w), xs, entry_ln,
                    w_l[:, ret_cols + rwkv_cols:].astype(BF16),
                    w_ret_up[l].astype(BF16), w_rwkv_up[l].astype(BF16), w_out[l].astype(BF16),
                    ln1_g[l], ln1_b[l], alpha)
        wq_scaled = (xa_wq[l] * ((D // XA_HEADS) ** -0.5)).astype(BF16)
        xa, cls = _cross_attention(xs.reshape(B, S, D), mem, xa_wkv, l, wq_scaled,
                                   xa_wo[l].astype(BF16), ln2_g[l], ln2_b[l], router_w, router_bias, alpha)
        xs = _moe(xa.reshape(T, D + LANES), cls, l, moe_w_gate, moe_w_up, moe_w_down, ln3_g[l], ln3_b[l], alpha)
    return xs.reshape(B, S, D)
```
